```python
import jax, jax.numpy as jnp
from jax import lax
import numpy as np

D_MODEL = 4096
BATCH = 2
SEQ = 8192
DEPTH = 1

MEM_LEN = 256
D_FF = 11008
POOL_WINDOWS = (2, 4, 8, 16)
N_POOL_GROUPS = len(POOL_WINDOWS)
POOL_GROUP = D_MODEL // 8
POOL_WIDTH = POOL_GROUP * N_POOL_GROUPS
MLA_HEADS = D_MODEL // 256
Q_LORA = D_MODEL // 4
KV_LORA = 512
QK_NOPE = 128
QK_ROPE = 64
V_HEAD = 128
QK_HEAD = QK_NOPE + QK_ROPE
MLA_WIDTH = MLA_HEADS * V_HEAD
X_HEADS = 4
X_HEAD_DIM = 128
X_WIDTH = X_HEADS * X_HEAD_DIM
ROPE_THETA = 10000.0
EPS = 1e-6
Q_BLOCK = 128
IN_SPLITS = (POOL_WIDTH, Q_LORA, KV_LORA, QK_ROPE, D_MODEL, D_MODEL)
IN_WIDTH = sum(IN_SPLITS)

kernel_name = "hybrid_pool_mla_macaron_block"


def rmsnorm(x, g):
    xf = x.astype(jnp.float32)
    y = xf * lax.rsqrt(jnp.mean(xf * xf, axis=-1, keepdims=True) + EPS)
    return (y * g.astype(jnp.float32)).astype(x.dtype)


def rope_tables(positions):
    half = QK_ROPE // 2
    inv = 1.0 / (ROPE_THETA ** (jnp.arange(half, dtype=jnp.float32) * (2.0 / QK_ROPE)))
    ang = positions.astype(jnp.float32)[:, :, None] * inv[None, None, :]
    return jnp.cos(ang)[:, :, None, :], jnp.sin(ang)[:, :, None, :]


def apply_rope(x, cos, sin):
    x1, x2 = jnp.split(x.astype(jnp.float32), 2, axis=-1)
    return jnp.concatenate([x1 * cos - x2 * sin, x2 * cos + x1 * sin], axis=-1).astype(x.dtype)


def swiglu(x, w_gu, w_down):
    g, u = jnp.split(x @ w_gu, 2, axis=-1)
    return (jax.nn.silu(g) * u) @ w_down


def causal_multiscale_pool(p, w_pool, pool_scale):
    B, S, _ = p.shape
    t = jnp.arange(S)
    outs = []
    for g, w in zip(jnp.split(p.astype(jnp.float32), N_POOL_GROUPS, axis=-1), POOL_WINDOWS):
        c = jnp.cumsum(g, axis=1)
        c_lag = jnp.pad(c, ((0, 0), (w, 0), (0, 0)))[:, :S]
        cnt = jnp.minimum(t + 1, w).astype(jnp.float32)[None, :, None]
        outs.append((c - c_lag) / cnt - g)
    mixed = jnp.stack(outs, axis=2).astype(p.dtype)
    mixed = jnp.einsum('bsgc,gcd->bsgd', mixed, w_pool)
    return mixed.reshape(B, S, POOL_WIDTH) * pool_scale


def causal_attention(q, k, v):
    B, H, S, Dk = q.shape
    nb = S // Q_BLOCK
    qb = q.reshape(B, H, nb, Q_BLOCK, Dk).transpose(2, 0, 1, 3, 4)
    kpos = jnp.arange(S)
    scale = Dk ** -0.5

    def one_block(args):
        qi, i = args
        s = jnp.einsum('bhqd,bhkd->bhqk', qi, k).astype(jnp.float32) * scale
        qpos = i * Q_BLOCK + jnp.arange(Q_BLOCK)
        s = jnp.where(kpos[None, :] <= qpos[:, None], s, -jnp.inf)
        pr = jax.nn.softmax(s, axis=-1).astype(v.dtype)
        return jnp.einsum('bhqk,bhkd->bhqd', pr, v)

    out = lax.map(one_block, (qb, jnp.arange(nb)))
    return out.transpose(1, 2, 0, 3, 4).reshape(B, H, S, v.shape[-1])


def setup_inputs(seed: int = 0) -> dict:
    key = jax.random.key(seed)
    ks = iter(jax.random.split(key, 40))

    def w(shape, fan_in):
        return jax.random.normal(next(ks), shape, jnp.float32) * (fan_in ** -0.5)

    def gain(n):
        return 1.0 + 0.02 * jax.random.normal(next(ks), (DEPTH, n), jnp.float32)

    L = DEPTH
    x = jax.random.normal(next(ks), (BATCH, SEQ, D_MODEL), jnp.float32)
    mem = jax.random.normal(next(ks), (BATCH, MEM_LEN, D_MODEL), jnp.float32)
    offset = jax.random.randint(next(ks), (BATCH, 1), 0, 1024, dtype=jnp.int32)
    positions = offset + jnp.arange(SEQ, dtype=jnp.int32)[None, :]
    return {
        "x": x, "mem": mem, "positions": positions,
        "ffn1_norm": gain(D_MODEL),
        "ffn1_w_gu": w((L, D_MODEL, 2 * D_FF), D_MODEL),
        "ffn1_w_down": w((L, D_FF, D_MODEL), D_FF),
        "mix_norm": gain(D_MODEL),
        "w_in": w((L, D_MODEL, IN_WIDTH), D_MODEL),
        "w_pool": w((L, N_POOL_GROUPS, POOL_GROUP, POOL_GROUP), POOL_GROUP),
        "pool_scale": gain(POOL_WIDTH),
        "q_latent_norm": gain(Q_LORA),
        "kv_latent_norm": gain(KV_LORA),
        "w_uq": w((L, Q_LORA, MLA_HEADS * QK_HEAD), Q_LORA),
        "w_ukv": w((L, KV_LORA, MLA_HEADS * (QK_NOPE + V_HEAD)), KV_LORA),
        "q_nope_norm": gain(QK_NOPE),
        "k_nope_norm": gain(QK_NOPE),
        "q_rope_norm": gain(QK_ROPE),
        "k_rope_norm": gain(QK_ROPE),
        "w_branch_pool": w((L, POOL_WIDTH, D_MODEL), POOL_WIDTH),
        "w_branch_mla": w((L, MLA_WIDTH, D_MODEL), MLA_WIDTH),
        "w_out": w((L, D_MODEL, D_MODEL), D_MODEL),
        "x_norm": gain(D_MODEL),
        "mem_norm": gain(D_MODEL),
        "w_xq": w((L, D_MODEL, X_WIDTH), D_MODEL),
        "w_xkv": w((L, D_MODEL, 2 * X_WIDTH), D_MODEL),
        "xq_norm": gain(X_HEAD_DIM),
        "xk_norm": gain(X_HEAD_DIM),
        "w_xo": w((L, X_WIDTH, D_MODEL), X_WIDTH),
        "ffn2_norm": gain(D_MODEL),
        "ffn2_w_gu": w((L, D_MODEL, 2 * D_FF), D_MODEL),
        "ffn2_w_down": w((L, D_FF, D_MODEL), D_FF),
    }


def reference(x, mem, positions, ffn1_norm, ffn1_w_gu, ffn1_w_down, mix_norm, w_in, w_pool,
              pool_scale, q_latent_norm, kv_latent_norm, w_uq, w_ukv, q_nope_norm, k_nope_norm,
              q_rope_norm, k_rope_norm, w_branch_pool, w_branch_mla, w_out, x_norm, mem_norm,
              w_xq, w_xkv, xq_norm, xk_norm, w_xo, ffn2_norm, ffn2_w_gu, ffn2_w_down):
    B, S, _ = x.shape
    M = mem.shape[1]
    cos, sin = rope_tables(positions)
    split_idx = [int(v) for v in np.cumsum(IN_SPLITS)[:-1]]
    h = x
    for l in range(DEPTH):
        h = h + 0.5 * swiglu(rmsnorm(h, ffn1_norm[l]), ffn1_w_gu[l], ffn1_w_down[l])

        u = rmsnorm(h, mix_norm[l])
        z_pool, z_q, z_kv, z_kr, g_pool, g_mla = jnp.split(u @ w_in[l], split_idx, axis=-1)

        a_out = causal_multiscale_pool(z_pool, w_pool[l], pool_scale[l])

        c_q = rmsnorm(z_q, q_latent_norm[l])
        q = (c_q @ w_uq[l]).reshape(B, S, MLA_HEADS, QK_HEAD)
        q_nope = rmsnorm(q[..., :QK_NOPE], q_nope_norm[l])
        q_rope = apply_rope(rmsnorm(q[..., QK_NOPE:], q_rope_norm[l]), cos, sin)
        c_kv = rmsnorm(z_kv, kv_latent_norm[l])
        kv = (c_kv @ w_ukv[l]).reshape(B, S, MLA_HEADS, QK_NOPE + V_HEAD)
        k_nope = rmsnorm(kv[..., :QK_NOPE], k_nope_norm[l])
        v = kv[..., QK_NOPE:]
        k_rope = apply_rope(rmsnorm(z_kr, k_rope_norm[l])[:, :, None, :], cos, sin)
        q_full = jnp.concatenate([q_nope, q_rope], axis=-1)
        k_full = jnp.concatenate([k_nope, jnp.broadcast_to(k_rope, (B, S, MLA_HEADS, QK_ROPE))], axis=-1)
        attn = causal_attention(q_full.transpose(0, 2, 1, 3), k_full.transpose(0, 2, 1, 3),
                                v.transpose(0, 2, 1, 3))
        b_out = attn.transpose(0, 2, 1, 3).reshape(B, S, MLA_WIDTH)

        merged = (jax.nn.sigmoid(g_pool) * (a_out @ w_branch_pool[l])
                  + jax.nn.sigmoid(g_mla) * (b_out @ w_branch_mla[l]))
        h = h + merged @ w_out[l]

        uq = rmsnorm(h, x_norm[l])
        mn = rmsnorm(mem, mem_norm[l])
        xq = rmsnorm((uq @ w_xq[l]).reshape(B, S, X_HEADS, X_HEAD_DIM), xq_norm[l])
        xk, xv = jnp.split(mn @ w_xkv[l], 2, axis=-1)
        xk = rmsnorm(xk.reshape(B, M, X_HEADS, X_HEAD_DIM), xk_norm[l])
        xv = xv.reshape(B, M, X_HEADS, X_HEAD_DIM)
        s = jnp.einsum('bshd,bmhd->bhsm', xq, xk).astype(jnp.float32) * (X_HEAD_DIM ** -0.5)
        pr = jax.nn.softmax(s, axis=-1).astype(xv.dtype)
        xo = jnp.einsum('bhsm,bmhd->bshd', pr, xv).reshape(B, S, X_WIDTH)
        h = h + xo @ w_xo[l]

        h = h + 0.5 * swiglu(rmsnorm(h, ffn2_norm[l]), ffn2_w_gu[l], ffn2_w_down[l])
    return h
```

```python
import functools
import math

import jax
import jax.numpy as jnp
from jax import lax
from jax.experimental import pallas as pl
from jax.experimental.pallas import tpu as pltpu

F32 = jnp.float32
BF16 = jnp.bfloat16

POOL_WINDOWS = (2, 4, 8, 16)
POOL_HALO = 16
QK_NOPE = 128
QK_ROPE = 64
QK_HEAD = QK_NOPE + QK_ROPE
QK_PAD = 256
V_HEAD = 128
X_HEADS = 4
X_HEAD_DIM = 128
ROPE_THETA = 10000.0
EPS = 1e-6
LANES = 128
VMEM_LIMIT = 60 * 1024 * 1024


def _params(*sem):
    return pltpu.CompilerParams(dimension_semantics=sem, vmem_limit_bytes=VMEM_LIMIT)


def _tile(n, pref):
    if n <= pref:
        return n
    t = (pref // LANES) * LANES
    while t >= LANES:
        if n % t == 0:
            return t
        t -= LANES
    raise ValueError(f"no lane-aligned tile of {n} below {pref}")


def _rms(x, gain):
    ms = jnp.mean(x * x, axis=-1, keepdims=True)
    return x * lax.rsqrt(ms + EPS) * gain


def _ffn_kernel(x_ref, gain_ref, wg_ref, wu_ref, wd_ref, o_ref, xn_ref):
    @pl.when(pl.program_id(1) == 0)
    def _():
        x = x_ref[...]
        xn_ref[...] = _rms(x, gain_ref[...]).astype(BF16)
        o_ref[...] = x

    xn = xn_ref[...]
    g = jnp.dot(xn, wg_ref[...], preferred_element_type=F32)
    u = jnp.dot(xn, wu_ref[...], preferred_element_type=F32)
    hid = (g * jax.nn.sigmoid(g) * u * 0.5).astype(BF16)
    o_ref[...] += jnp.dot(hid, wd_ref[...], preferred_element_type=F32)


def _ffn(x, gain, w_gu, w_down, *, bm=512, bf=256):
    t, d = x.shape
    f = w_down.shape[0]
    bm, bf = _tile(t, bm), _tile(f, bf)
    nf = f // bf
    return pl.pallas_call(
        _ffn_kernel,
        grid=(t // bm, nf),
        in_specs=[
            pl.BlockSpec((bm, d), lambda i, j: (i, 0)),
            pl.BlockSpec((1, d), lambda i, j: (0, 0)),
            pl.BlockSpec((d, bf), lambda i, j: (0, j)),
            pl.BlockSpec((d, bf), lambda i, j: (0, j + nf)),
            pl.BlockSpec((bf, d), lambda i, j: (j, 0)),
        ],
        out_specs=pl.BlockSpec((bm, d), lambda i, j: (i, 0)),
        out_shape=jax.ShapeDtypeStruct((t, d), F32),
        scratch_shapes=[pltpu.VMEM((bm, d), BF16)],
        compiler_params=_params("parallel", "arbitrary"),
        name="ffn",
    )(x, gain.reshape(1, d), w_gu, w_gu, w_down)


def _norm_matmul_kernel(x_ref, gain_ref, w_ref, o_ref, xn_ref):
    @pl.when(pl.program_id(1) == 0)
    def _():
        xn_ref[...] = _rms(x_ref[...], gain_ref[...]).astype(BF16)

    o_ref[...] = jnp.dot(xn_ref[...], w_ref[...], preferred_element_type=F32)


def _norm_matmul(x, gain, w, *, bm=512, bn=1024, name):
    t, d = x.shape
    n = w.shape[1]
    bm, bn = _tile(t, bm), _tile(n, bn)
    return pl.pallas_call(
        _norm_matmul_kernel,
        grid=(t // bm, n // bn),
        in_specs=[
            pl.BlockSpec((bm, d), lambda i, j: (i, 0)),
            pl.BlockSpec((1, d), lambda i, j: (0, 0)),
            pl.BlockSpec((d, bn), lambda i, j: (0, j)),
        ],
        out_specs=pl.BlockSpec((bm, bn), lambda i, j: (i, j)),
        out_shape=jax.ShapeDtypeStruct((t, n), F32),
        scratch_shapes=[pltpu.VMEM((bm, d), BF16)],
        compiler_params=_params("parallel", "arbitrary"),
        name=name,
    )(x, gain.reshape(1, d), w)


def _pool_kernel(zc_ref, zp_ref, wp_ref, sc_ref, o_ref, ext_ref, *, bm, c):
    i = pl.program_id(1)
    ext_ref[0:POOL_HALO, :] = jnp.where(i > 0, zp_ref[...], 0.0)
    ext_ref[POOL_HALO:POOL_HALO + bm, :] = zc_ref[...]
    t = i * bm + lax.broadcasted_iota(jnp.int32, (bm, 1), 0)
    for g, w in enumerate(POOL_WINDOWS):
        cols = slice(g * c, (g + 1) * c)
        x = zc_ref[:, cols]
        acc = x
        for k in range(1, w):
            acc = acc + ext_ref[POOL_HALO - k:POOL_HALO - k + bm, cols]
        cnt = jnp.minimum(t + 1, w).astype(F32)
        mixed = (acc / cnt - x).astype(BF16)
        y = jnp.dot(mixed, wp_ref[g], preferred_element_type=F32) * sc_ref[:, cols]
        o_ref[:, cols] = y.astype(BF16)


def _pool(z_main, w_pool, pool_scale, batch, seq, *, bm=512):
    g, c, _ = w_pool.shape
    pw = g * c
    bm = _tile(seq, bm)
    nb = seq // bm
    halo_blocks = bm // POOL_HALO
    return pl.pallas_call(
        functools.partial(_pool_kernel, bm=bm, c=c),
        grid=(batch, nb),
        in_specs=[
            pl.BlockSpec((bm, pw), lambda b, i: (b * nb + i, 0)),
            pl.BlockSpec((POOL_HALO, pw),
                         lambda b, i: (jnp.maximum((b * nb + i) * halo_blocks - 1, 0), 0)),
            pl.BlockSpec((g, c, c), lambda b, i: (0, 0, 0)),
            pl.BlockSpec((1, pw), lambda b, i: (0, 0)),
        ],
        out_specs=pl.BlockSpec((bm, pw), lambda b, i: (b * nb + i, 0)),
        out_shape=jax.ShapeDtypeStruct((batch * seq, pw), BF16),
        scratch_shapes=[pltpu.VMEM((POOL_HALO + bm, pw), F32)],
        compiler_params=_params("parallel", "parallel"),
        name="pool",
    )(z_main, z_main, w_pool, pool_scale.reshape(1, pw))


def _rope_tables(pos, inv):
    ang = pos.astype(F32) * inv
    cos, sin = jnp.cos(ang), jnp.sin(ang)
    lane = lax.broadcasted_iota(jnp.int32, ang.shape, 1)
    half = QK_ROPE // 2
    c = jnp.where(lane < QK_ROPE, cos, 0.0)
    sa = jnp.where(lane < half, -sin, 0.0)
    sb = jnp.where((lane >= half) & (lane < QK_ROPE), sin, 0.0)
    return c, sa, sb


def _rope_norm(r, gain, c, sa, sb):
    ms = jnp.sum(r * r, axis=-1, keepdims=True) * (1.0 / QK_ROPE)
    r = r * lax.rsqrt(ms + EPS) * gain
    half = QK_ROPE // 2
    return r * c + pltpu.roll(r, LANES - half, 1) * sa + pltpu.roll(r, half, 1) * sb


def _qkv_kernel(zq_ref, zkv_ref, zkr_ref, pos_ref, inv_ref, gql_ref, gkvl_ref, gqn_ref, gkn_ref,
                gqr_ref, gkr_ref, wuq_ref, wukv_ref, q_ref, k_ref, v_ref,
                cq_ref, ckv_ref, kr_ref, c_ref, sa_ref, sb_ref):
    @pl.when(pl.program_id(1) == 0)
    def _():
        cq_ref[...] = _rms(zq_ref[...], gql_ref[...]).astype(BF16)
        ckv_ref[...] = _rms(zkv_ref[...], gkvl_ref[...]).astype(BF16)
        c, sa, sb = _rope_tables(pos_ref[...], inv_ref[...])
        c_ref[...], sa_ref[...], sb_ref[...] = c, sa, sb
        kr_ref[...] = _rope_norm(zkr_ref[...], gkr_ref[...], c, sa, sb).astype(BF16)

    q = jnp.dot(cq_ref[...], wuq_ref[...], preferred_element_type=F32)
    q_ref[:, :QK_NOPE] = _rms(q[:, :QK_NOPE], gqn_ref[...]).astype(BF16)
    q_ref[:, QK_NOPE:] = _rope_norm(q[:, QK_NOPE:], gqr_ref[...], c_ref[...], sa_ref[...],
                                    sb_ref[...]).astype(BF16)
    kv = jnp.dot(ckv_ref[...], wukv_ref[...], preferred_element_type=F32)
    k_ref[:, :QK_NOPE] = _rms(kv[:, :QK_NOPE], gkn_ref[...]).astype(BF16)
    k_ref[:, QK_NOPE:] = kr_ref[...]
    v_ref[...] = kv[:, QK_NOPE:].astype(BF16)


def _qkv(z_main, z_tail, pos, inv, gql, gkvl, gqn, gkn, gqr, gkr, wuq, wukv, batch, seq, pw, *, bm=512):
    h, ql, _ = wuq.shape
    kvl = wukv.shape[1]
    bm = _tile(seq, bm)
    nb = seq // bm
    t = batch * seq
    row = lambda i, hh: (i, 0)
    const = lambda i, hh: (0, 0)
    out_map = lambda i, hh: (i // nb, hh, i % nb, 0)
    return pl.pallas_call(
        _qkv_kernel,
        grid=(t // bm, h),
        in_specs=[
            pl.BlockSpec((bm, ql), lambda i, hh: (i, pw // ql)),
            pl.BlockSpec((bm, kvl), row),
            pl.BlockSpec((bm, LANES), lambda i, hh: (i, kvl // LANES)),
            pl.BlockSpec((bm, 1), row),
            pl.BlockSpec((1, LANES), const),
            pl.BlockSpec((1, ql), const),
            pl.BlockSpec((1, kvl), const),
            pl.BlockSpec((1, QK_NOPE), const),
            pl.BlockSpec((1, QK_NOPE), const),
            pl.BlockSpec((1, LANES), const),
            pl.BlockSpec((1, LANES), const),
            pl.BlockSpec((None, ql, QK_PAD), lambda i, hh: (hh, 0, 0)),
            pl.BlockSpec((None, kvl, QK_NOPE + V_HEAD), lambda i, hh: (hh, 0, 0)),
        ],
        out_specs=[
            pl.BlockSpec((None, None, bm, QK_PAD), out_map),
            pl.BlockSpec((None, None, bm, QK_PAD), out_map),
            pl.BlockSpec((None, None, bm, V_HEAD), out_map),
        ],
        out_shape=[
            jax.ShapeDtypeStruct((batch, h, seq, QK_PAD), BF16),
            jax.ShapeDtypeStruct((batch, h, seq, QK_PAD), BF16),
            jax.ShapeDtypeStruct((batch, h, seq, V_HEAD), BF16),
        ],
        scratch_shapes=[
            pltpu.VMEM((bm, ql), BF16),
            pltpu.VMEM((bm, kvl), BF16),
            pltpu.VMEM((bm, LANES), BF16),
            pltpu.VMEM((bm, LANES), F32),
            pltpu.VMEM((bm, LANES), F32),
            pltpu.VMEM((bm, LANES), F32),
        ],
        compiler_params=_params("parallel", "arbitrary"),
        name="qkv_prep",
    )(z_main, z_tail, z_tail, pos, inv, gql, gkvl, gqn, gkn, gqr, gkr, wuq, wukv)


def _flash_kernel(q_ref, k_ref, v_ref, o_ref, m_ref, l_ref, acc_ref, *, blk, scale):
    qi = pl.program_id(2)
    q = q_ref[...]
    m_ref[...] = jnp.full(m_ref.shape, -jnp.inf, F32)
    l_ref[...] = jnp.zeros(l_ref.shape, F32)
    acc_ref[...] = jnp.zeros(acc_ref.shape, F32)

    def step(kj, masked):
        start = pl.multiple_of(kj * blk, blk)
        k = k_ref[pl.ds(start, blk), :]
        v = v_ref[pl.ds(start, blk), :]
        s = lax.dot_general(q, k, (((1,), (1,)), ((), ())), preferred_element_type=F32) * scale
        if masked:
            r = lax.broadcasted_iota(jnp.int32, s.shape, 0)
            c = lax.broadcasted_iota(jnp.int32, s.shape, 1)
            s = jnp.where(c <= r, s, -jnp.inf)
        m_old = m_ref[...]
        m_new = jnp.maximum(m_old, jnp.max(s, axis=-1, keepdims=True))
        p = jnp.exp(s - m_new)
        alpha = jnp.exp(m_old - m_new)
        l_ref[...] = alpha * l_ref[...] + jnp.sum(p, axis=-1, keepdims=True)
        acc_ref[...] = alpha * acc_ref[...] + jnp.dot(p.astype(BF16), v, preferred_element_type=F32)
        m_ref[...] = m_new

    def body(kj, carry):
        step(kj, False)
        return carry

    lax.fori_loop(0, qi, body, 0)
    step(qi, True)
    o_ref[...] = (acc_ref[...] / l_ref[...]).astype(o_ref.dtype)


def _flash(q, k, v, *, blk=512):
    b, h, s, _ = q.shape
    blk = _tile(s, blk)
    return pl.pallas_call(
        functools.partial(_flash_kernel, blk=blk, scale=QK_HEAD ** -0.5),
        grid=(b, h, s // blk),
        in_specs=[
            pl.BlockSpec((None, None, blk, QK_PAD), lambda bi, hi, qi: (bi, hi, qi, 0)),
            pl.BlockSpec((None, None, s, QK_PAD), lambda bi, hi, qi: (bi, hi, 0, 0)),
            pl.BlockSpec((None, None, s, V_HEAD), lambda bi, hi, qi: (bi, hi, 0, 0)),
        ],
        out_specs=pl.BlockSpec((None, blk, V_HEAD), lambda bi, hi, qi: (bi, qi, hi)),
        out_shape=jax.ShapeDtypeStruct((b, s, h * V_HEAD), BF16),
        scratch_shapes=[
            pltpu.VMEM((blk, 1), F32),
            pltpu.VMEM((blk, 1), F32),
            pltpu.VMEM((blk, V_HEAD), F32),
        ],
        compiler_params=_params("parallel", "parallel", "arbitrary"),
        name="mla_flash",
    )(q, k, v)


def _merge_kernel(a_ref, b_ref, wa_ref, wb_ref, gp_ref, gm_ref, o_ref):
    pa = jnp.dot(a_ref[...], wa_ref[...], preferred_element_type=F32)
    pb = jnp.dot(b_ref[...], wb_ref[...], preferred_element_type=F32)
    o_ref[...] = (jax.nn.sigmoid(gp_ref[...]) * pa + jax.nn.sigmoid(gm_ref[...]) * pb).astype(BF16)


def _merge(a, b, wa, wb, z_main, gate_off, *, bm=512, bn=1024):
    t, pw = a.shape
    mw = b.shape[1]
    d = wa.shape[1]
    bm, bn = _tile(t, bm), _tile(math.gcd(d, gate_off), bn)
    gp0, gm0 = gate_off // bn, (gate_off + d) // bn
    return pl.pallas_call(
        _merge_kernel,
        grid=(t // bm, d // bn),
        in_specs=[
            pl.BlockSpec((bm, pw), lambda i, j: (i, 0)),
            pl.BlockSpec((bm, mw), lambda i, j: (i, 0)),
            pl.BlockSpec((pw, bn), lambda i, j: (0, j)),
            pl.BlockSpec((mw, bn), lambda i, j: (0, j)),
            pl.BlockSpec((bm, bn), lambda i, j: (i, gp0 + j)),
            pl.BlockSpec((bm, bn), lambda i, j: (i, gm0 + j)),
        ],
        out_specs=pl.BlockSpec((bm, bn), lambda i, j: (i, j)),
        out_shape=jax.ShapeDtypeStruct((t, d), BF16),
        compiler_params=_params("parallel", "arbitrary"),
        name="branch_merge",
    )(a, b, wa, wb, z_main, z_main)


def _proj_residual_kernel(a_ref, w_ref, h_ref, o_ref):
    o_ref[...] = h_ref[...] + jnp.dot(a_ref[...], w_ref[...], preferred_element_type=F32)


def _proj_residual(a, w, h, *, bm=512, bn=1024):
    t, kdim = a.shape
    d = w.shape[1]
    bm, bn = _tile(t, bm), _tile(d, bn)
    return pl.pallas_call(
        _proj_residual_kernel,
        grid=(t // bm, d // bn),
        in_specs=[
            pl.BlockSpec((bm, kdim), lambda i, j: (i, 0)),
            pl.BlockSpec((kdim, bn), lambda i, j: (0, j)),
            pl.BlockSpec((bm, bn), lambda i, j: (i, j)),
        ],
        out_specs=pl.BlockSpec((bm, bn), lambda i, j: (i, j)),
        out_shape=jax.ShapeDtypeStruct((t, d), F32),
        compiler_params=_params("parallel", "arbitrary"),
        name="out_proj",
    )(a, w, h)


def _mem_kv_kernel(mem_ref, gain_ref, w_ref, gk_ref, k_ref, v_ref):
    mn = _rms(mem_ref[...], gain_ref[...]).astype(BF16)
    kv = jnp.dot(mn, w_ref[...], preferred_element_type=F32)
    xw = X_HEADS * X_HEAD_DIM
    for hh in range(X_HEADS):
        cols = slice(hh * X_HEAD_DIM, (hh + 1) * X_HEAD_DIM)
        k_ref[:, cols] = _rms(kv[:, cols], gk_ref[...]).astype(BF16)
    v_ref[...] = kv[:, xw:].astype(BF16)


def _mem_kv(mem, gain, w_xkv, gk):
    b, m, d = mem.shape
    xw = X_HEADS * X_HEAD_DIM
    return pl.pallas_call(
        _mem_kv_kernel,
        grid=(b,),
        in_specs=[
            pl.BlockSpec((None, m, d), lambda bi: (bi, 0, 0)),
            pl.BlockSpec((1, d), lambda bi: (0, 0)),
            pl.BlockSpec((d, 2 * xw), lambda bi: (0, 0)),
            pl.BlockSpec((1, X_HEAD_DIM), lambda bi: (0, 0)),
        ],
        out_specs=[
            pl.BlockSpec((None, m, xw), lambda bi: (bi, 0, 0)),
            pl.BlockSpec((None, m, xw), lambda bi: (bi, 0, 0)),
        ],
        out_shape=[
            jax.ShapeDtypeStruct((b, m, xw), BF16),
            jax.ShapeDtypeStruct((b, m, xw), BF16),
        ],
        compiler_params=_params("parallel"),
        name="mem_kv",
    )(mem, gain.reshape(1, d), w_xkv, gk)


def _cross_kernel(h_ref, gain_ref, wq_ref, gq_ref, k_ref, v_ref, wo_ref, o_ref, xo_ref):
    h = h_ref[...]
    uq = _rms(h, gain_ref[...]).astype(BF16)
    xq = jnp.dot(uq, wq_ref[...], preferred_element_type=F32)
    scale = X_HEAD_DIM ** -0.5
    for hh in range(X_HEADS):
        cols = slice(hh * X_HEAD_DIM, (hh + 1) * X_HEAD_DIM)
        qh = _rms(xq[:, cols], gq_ref[...]).astype(BF16)
        s = lax.dot_general(qh, k_ref[:, cols], (((1,), (1,)), ((), ())),
                            preferred_element_type=F32) * scale
        p = jnp.exp(s - jnp.max(s, axis=-1, keepdims=True))
        p = (p / jnp.sum(p, axis=-1, keepdims=True)).astype(BF16)
        xo_ref[:, cols] = jnp.dot(p, v_ref[:, cols], preferred_element_type=F32).astype(BF16)
    o_ref[...] = h + jnp.dot(xo_ref[...], wo_ref[...], preferred_element_type=F32)


def _cross(h, gain, wq, gq, xk, xv, wo, seq, *, bm=256):
    t, d = h.shape
    m = xk.shape[1]
    xw = X_HEADS * X_HEAD_DIM
    bm = _tile(seq, bm)
    nb = seq // bm
    return pl.pallas_call(
        _cross_kernel,
        grid=(t // bm,),
        in_specs=[
            pl.BlockSpec((bm, d), lambda i: (i, 0)),
            pl.BlockSpec((1, d), lambda i: (0, 0)),
            pl.BlockSpec((d, xw), lambda i: (0, 0)),
            pl.BlockSpec((1, X_HEAD_DIM), lambda i: (0, 0)),
            pl.BlockSpec((None, m, xw), lambda i: (i // nb, 0, 0)),
            pl.BlockSpec((None, m, xw), lambda i: (i // nb, 0, 0)),
            pl.BlockSpec((xw, d), lambda i: (0, 0)),
        ],
        out_specs=pl.BlockSpec((bm, d), lambda i: (i, 0)),
        out_shape=jax.ShapeDtypeStruct((t, d), F32),
        scratch_shapes=[pltpu.VMEM((bm, xw), BF16)],
        compiler_params=_params("parallel"),
        name="cross_attn",
    )(h, gain.reshape(1, d), wq, gq, xk, xv, wo)


def _pad_lanes(v, width):
    return jnp.pad(v, (0, width - v.shape[0])).reshape(1, width)


def kernel(x, mem, positions, ffn1_norm, ffn1_w_gu, ffn1_w_down, mix_norm, w_in, w_pool, pool_scale,
           q_latent_norm, kv_latent_norm, w_uq, w_ukv, q_nope_norm, k_nope_norm, q_rope_norm,
           k_rope_norm, w_branch_pool, w_branch_mla, w_out, x_norm, mem_norm, w_xq, w_xkv, xq_norm,
           xk_norm, w_xo, ffn2_norm, ffn2_w_gu, ffn2_w_down):
    batch, seq, d = x.shape
    t = batch * seq
    depth = w_in.shape[0]
    pw = w_branch_pool.shape[1]
    ql = q_latent_norm.shape[1]
    kvl = kv_latent_norm.shape[1]
    heads = w_uq.shape[2] // QK_HEAD

    half = QK_ROPE // 2
    inv = 1.0 / (ROPE_THETA ** (jnp.arange(half, dtype=F32) * (2.0 / QK_ROPE)))
    inv = _pad_lanes(jnp.concatenate([inv, inv]), LANES)
    pos = positions.reshape(t, 1)

    h = x.reshape(t, d)
    for l in range(depth):
        h = _ffn(h, ffn1_norm[l], ffn1_w_gu[l].astype(BF16), ffn1_w_down[l].astype(BF16))

        o_q, o_kv, o_kr, o_gp = pw, pw + ql, pw + ql + kvl, pw + ql + kvl + QK_ROPE
        wi = w_in[l]
        w_main = jnp.concatenate([wi[:, :o_kv], wi[:, o_gp:]], axis=1).astype(BF16)
        w_tail = jnp.pad(wi[:, o_kv:o_gp], ((0, 0), (0, LANES - QK_ROPE))).astype(BF16)
        z_main = _norm_matmul(h, mix_norm[l], w_main, name="in_proj_main")
        z_tail = _norm_matmul(h, mix_norm[l], w_tail, name="in_proj_tail")

        a_out = _pool(z_main, w_pool[l].astype(BF16), pool_scale[l], batch, seq)

        wuq = jnp.pad(w_uq[l].reshape(ql, heads, QK_HEAD), ((0, 0), (0, 0), (0, QK_PAD - QK_HEAD)))
        wuq = wuq.transpose(1, 0, 2).astype(BF16)
        wukv = w_ukv[l].reshape(kvl, heads, QK_NOPE + V_HEAD).transpose(1, 0, 2).astype(BF16)
        q, k, v = _qkv(z_main, z_tail, pos, inv,
                       q_latent_norm[l].reshape(1, ql), kv_latent_norm[l].reshape(1, kvl),
                       q_nope_norm[l].reshape(1, QK_NOPE), k_nope_norm[l].reshape(1, QK_NOPE),
                       _pad_lanes(q_rope_norm[l], LANES), _pad_lanes(k_rope_norm[l], LANES),
                       wuq, wukv, batch, seq, pw)
        b_out = _flash(q, k, v).reshape(t, heads * V_HEAD)

        merged = _merge(a_out, b_out, w_branch_pool[l].astype(BF16), w_branch_mla[l].astype(BF16),
                        z_main, pw + ql)
        h = _proj_residual(merged, w_out[l].astype(BF16), h)

        xk, xv = _mem_kv(mem, mem_norm[l], w_xkv[l].astype(BF16), xk_norm[l].reshape(1, X_HEAD_DIM))
        h = _cross(h, x_norm[l], w_xq[l].astype(BF16), xq_norm[l].reshape(1, X_HEAD_DIM), xk, xv,
                   w_xo[l].astype(BF16), seq)

        h = _ffn(h, ffn2_norm[l], ffn2_w_gu[l].astype(BF16), ffn2_w_down[l].astype(BF16))
    return h.reshape(batch, seq, d)
```

```python
import functools
import math

import jax
import jax.numpy as jnp
from jax import lax
from jax.experimental import pallas as pl
from jax.experimental.pallas import tpu as pltpu

F32 = jnp.float32
BF16 = jnp.bfloat16

POOL_WINDOWS = (2, 4, 8, 16)
POOL_HALO = 16
QK_NOPE = 128
QK_ROPE = 64
QK_HEAD = QK_NOPE + QK_ROPE
QK_PAD = 256
V_HEAD = 128
X_HEADS = 4
X_HEAD_DIM = 128
ROPE_THETA = 10000.0
EPS = 1e-6
LANES = 128
VMEM_LIMIT = 60 * 1024 * 1024


def _params(*sem):
    return pltpu.CompilerParams(dimension_semantics=sem, vmem_limit_bytes=VMEM_LIMIT)


def _tile(n, pref):
    if n <= pref:
        return n
    t = (pref // LANES) * LANES
    while t >= LANES:
        if n % t == 0:
            return t
        t -= LANES
    raise ValueError(f"no lane-aligned tile of {n} below {pref}")


def _rms(x, gain):
    ms = jnp.mean(x * x, axis=-1, keepdims=True)
    return x * lax.rsqrt(ms + EPS) * gain


def _ffn_kernel(x_ref, gain_ref, wg_ref, wu_ref, wd_ref, o_ref, xn_ref):
    @pl.when(pl.program_id(1) == 0)
    def _():
        x = x_ref[...]
        xn_ref[...] = _rms(x, gain_ref[...]).astype(BF16)
        o_ref[...] = x

    xn = xn_ref[...]
    g = jnp.dot(xn, wg_ref[...], preferred_element_type=F32)
    u = jnp.dot(xn, wu_ref[...], preferred_element_type=F32)
    hid = (g * jax.nn.sigmoid(g) * u * 0.5).astype(BF16)
    o_ref[...] += jnp.dot(hid, wd_ref[...], preferred_element_type=F32)


def _ffn(x, gain, w_gu, w_down, *, bm=512, bf=256):
    t, d = x.shape
    f = w_down.shape[0]
    bm, bf = _tile(t, bm), _tile(f, bf)
    nf = f // bf
    return pl.pallas_call(
        _ffn_kernel,
        grid=(t // bm, nf),
        in_specs=[
            pl.BlockSpec((bm, d), lambda i, j: (i, 0)),
            pl.BlockSpec((1, d), lambda i, j: (0, 0)),
            pl.BlockSpec((d, bf), lambda i, j: (0, j)),
            pl.BlockSpec((d, bf), lambda i, j: (0, j + nf)),
            pl.BlockSpec((bf, d), lambda i, j: (j, 0)),
        ],
        out_specs=pl.BlockSpec((bm, d), lambda i, j: (i, 0)),
        out_shape=jax.ShapeDtypeStruct((t, d), F32),
        scratch_shapes=[pltpu.VMEM((bm, d), BF16)],
        compiler_params=_params("parallel", "arbitrary"),
        name="ffn",
    )(x, gain.reshape(1, d), w_gu, w_gu, w_down)


def _norm_matmul_kernel(x_ref, gain_ref, w_ref, o_ref, xn_ref):
    @pl.when(pl.program_id(1) == 0)
    def _():
        xn_ref[...] = _rms(x_ref[...], gain_ref[...]).astype(BF16)

    o_ref[...] = jnp.dot(xn_ref[...], w_ref[...], preferred_element_type=F32)


def _norm_matmul(x, gain, w, *, bm=512, bn=1024, name):
    t, d = x.shape
    n = w.shape[1]
    bm, bn = _tile(t, bm), _tile(n, bn)
    return pl.pallas_call(
        _norm_matmul_kernel,
        grid=(t // bm, n // bn),
        in_specs=[
            pl.BlockSpec((bm, d), lambda i, j: (i, 0)),
            pl.BlockSpec((1, d), lambda i, j: (0, 0)),
            pl.BlockSpec((d, bn), lambda i, j: (0, j)),
        ],
        out_specs=pl.BlockSpec((bm, bn), lambda i, j: (i, j)),
        out_shape=jax.ShapeDtypeStruct((t, n), F32),
        scratch_shapes=[pltpu.VMEM((bm, d), BF16)],
        compiler_params=_params("parallel", "arbitrary"),
        name=name,
    )(x, gain.reshape(1, d), w)


def _pool_kernel(zc_ref, zp_ref, wp_ref, sc_ref, o_ref, ext_ref, *, bm, c):
    i = pl.program_id(1)
    ext_ref[0:POOL_HALO, :] = jnp.where(i > 0, zp_ref[...], 0.0)
    ext_ref[POOL_HALO:POOL_HALO + bm, :] = zc_ref[...]
    t = i * bm + lax.broadcasted_iota(jnp.int32, (bm, 1), 0)
    for g, w in enumerate(POOL_WINDOWS):
        cols = slice(g * c, (g + 1) * c)
        x = zc_ref[:, cols]
        acc = x
        for k in range(1, w):
            acc = acc + ext_ref[POOL_HALO - k:POOL_HALO - k + bm, cols]
        cnt = jnp.minimum(t + 1, w).astype(F32)
        mixed = (acc / cnt - x).astype(BF16)
        y = jnp.dot(mixed, wp_ref[g], preferred_element_type=F32) * sc_ref[:, cols]
        o_ref[:, cols] = y.astype(BF16)


def _pool(z_main, w_pool, pool_scale, batch, seq, *, bm=512):
    g, c, _ = w_pool.shape
    pw = g * c
    bm = _tile(seq, bm)
    nb = seq // bm
    halo_blocks = bm // POOL_HALO
    return pl.pallas_call(
        functools.partial(_pool_kernel, bm=bm, c=c),
        grid=(batch, nb),
        in_specs=[
            pl.BlockSpec((bm, pw), lambda b, i: (b * nb + i, 0)),
            pl.BlockSpec((POOL_HALO, pw),
                         lambda b, i: (jnp.maximum((b * nb + i) * halo_blocks - 1, 0), 0)),
            pl.BlockSpec((g, c, c), lambda b, i: (0, 0, 0)),
            pl.BlockSpec((1, pw), lambda b, i: (0, 0)),
        ],
        out_specs=pl.BlockSpec((bm, pw), lambda b, i: (b * nb + i, 0)),
        out_shape=jax.ShapeDtypeStruct((batch * seq, pw), BF16),
        scratch_shapes=[pltpu.VMEM((POOL_HALO + bm, pw), F32)],
        compiler_params=_params("parallel", "parallel"),
        name="pool",
    )(z_main, z_main, w_pool, pool_scale.reshape(1, pw))


def _rope_tables(pos, inv):
    ang = pos.astype(F32) * inv
    cos, sin = jnp.cos(ang), jnp.sin(ang)
    lane = lax.broadcasted_iota(jnp.int32, ang.shape, 1)
    half = QK_ROPE // 2
    c = jnp.where(lane < QK_ROPE, cos, 0.0)
    sa = jnp.where(lane < half, -sin, 0.0)
    sb = jnp.where((lane >= half) & (lane < QK_ROPE), sin, 0.0)
    return c, sa, sb


def _rope_norm(r, gain, c, sa, sb):
    ms = jnp.sum(r * r, axis=-1, keepdims=True) * (1.0 / QK_ROPE)
    r = r * lax.rsqrt(ms + EPS) * gain
    half = QK_ROPE // 2
    return r * c + pltpu.roll(r, LANES - half, 1) * sa + pltpu.roll(r, half, 1) * sb


def _qkv_kernel(zq_ref, zkv_ref, zkr_ref, pos_ref, inv_ref, gql_ref, gkvl_ref, gqn_ref, gkn_ref,
                gqr_ref, gkr_ref, wuq_ref, wukv_ref, qt_ref, k_ref, vt_ref,
                cq_ref, ckv_ref, kr_ref, c_ref, sa_ref, sb_ref):
    @pl.when(pl.program_id(1) == 0)
    def _():
        cq_ref[...] = _rms(zq_ref[...], gql_ref[...]).astype(BF16)
        ckv_ref[...] = _rms(zkv_ref[...], gkvl_ref[...]).astype(BF16)
        c, sa, sb = _rope_tables(pos_ref[...], inv_ref[...])
        c_ref[...], sa_ref[...], sb_ref[...] = c, sa, sb
        kr_ref[...] = _rope_norm(zkr_ref[...], gkr_ref[...], c, sa, sb).astype(BF16)

    q = jnp.dot(cq_ref[...], wuq_ref[...], preferred_element_type=F32)
    qt_ref[:QK_NOPE, :] = _rms(q[:, :QK_NOPE], gqn_ref[...]).T.astype(BF16)
    qt_ref[QK_NOPE:, :] = _rope_norm(q[:, QK_NOPE:], gqr_ref[...], c_ref[...], sa_ref[...],
                                     sb_ref[...]).T.astype(BF16)
    kv = jnp.dot(ckv_ref[...], wukv_ref[...], preferred_element_type=F32)
    k_ref[:, :QK_NOPE] = _rms(kv[:, :QK_NOPE], gkn_ref[...]).astype(BF16)
    k_ref[:, QK_NOPE:] = kr_ref[...]
    vt_ref[...] = kv[:, QK_NOPE:].T.astype(BF16)


def _qkv(z_main, z_tail, pos, inv, gql, gkvl, gqn, gkn, gqr, gkr, wuq, wukv, batch, seq, pw, *, bm=512):
    h, ql, _ = wuq.shape
    kvl = wukv.shape[1]
    bm = _tile(seq, bm)
    nb = seq // bm
    t = batch * seq
    row = lambda i, hh: (i, 0)
    const = lambda i, hh: (0, 0)
    out_map = lambda i, hh: (i // nb, hh, i % nb, 0)
    out_map_t = lambda i, hh: (i // nb, hh, i % nb, 0, 0)
    return pl.pallas_call(
        _qkv_kernel,
        grid=(t // bm, h),
        in_specs=[
            pl.BlockSpec((bm, ql), lambda i, hh: (i, pw // ql)),
            pl.BlockSpec((bm, kvl), row),
            pl.BlockSpec((bm, LANES), lambda i, hh: (i, kvl // LANES)),
            pl.BlockSpec((bm, 1), row),
            pl.BlockSpec((1, LANES), const),
            pl.BlockSpec((1, ql), const),
            pl.BlockSpec((1, kvl), const),
            pl.BlockSpec((1, QK_NOPE), const),
            pl.BlockSpec((1, QK_NOPE), const),
            pl.BlockSpec((1, LANES), const),
            pl.BlockSpec((1, LANES), const),
            pl.BlockSpec((None, ql, QK_PAD), lambda i, hh: (hh, 0, 0)),
            pl.BlockSpec((None, kvl, QK_NOPE + V_HEAD), lambda i, hh: (hh, 0, 0)),
        ],
        out_specs=[
            pl.BlockSpec((None, None, None, QK_PAD, bm), out_map_t),
            pl.BlockSpec((None, None, bm, QK_PAD), out_map),
            pl.BlockSpec((None, None, None, V_HEAD, bm), out_map_t),
        ],
        out_shape=[
            jax.ShapeDtypeStruct((batch, h, nb, QK_PAD, bm), BF16),
            jax.ShapeDtypeStruct((batch, h, seq, QK_PAD), BF16),
            jax.ShapeDtypeStruct((batch, h, nb, V_HEAD, bm), BF16),
        ],
        scratch_shapes=[
            pltpu.VMEM((bm, ql), BF16),
            pltpu.VMEM((bm, kvl), BF16),
            pltpu.VMEM((bm, LANES), BF16),
            pltpu.VMEM((bm, LANES), F32),
            pltpu.VMEM((bm, LANES), F32),
            pltpu.VMEM((bm, LANES), F32),
        ],
        compiler_params=_params("parallel", "arbitrary"),
        name="qkv_prep",
    )(z_main, z_tail, z_tail, pos, inv, gql, gkvl, gqn, gkn, gqr, gkr, wuq, wukv)


def _flash_kernel(qt_ref, k_ref, vt_ref, o_ref, sa_ref, sb_ref, m_ref, l_ref, acc_ref, *, blk, chunk, scale):
    qi = pl.program_id(2)
    nch = blk // chunk
    c2 = scale * math.log2(math.e)
    m_ref[...] = jnp.full(m_ref.shape, -jnp.inf, F32)
    l_ref[...] = jnp.zeros(l_ref.shape, F32)
    acc_ref[...] = jnp.zeros(acc_ref.shape, F32)

    def scores(kj, s_ref):
        start = pl.multiple_of(kj * blk, blk)
        k = k_ref[pl.ds(start, blk), :]
        for c in range(nch):
            s_ref[c] = jnp.dot(k, qt_ref[:, c * chunk:(c + 1) * chunk], preferred_element_type=F32)

    def update(kj, s_ref, masked):
        vt = vt_ref[kj]
        for c in range(nch):
            cols = slice(c * chunk, (c + 1) * chunk)
            st = s_ref[c] * c2
            if masked:
                key = lax.broadcasted_iota(jnp.int32, st.shape, 0)
                qry = lax.broadcasted_iota(jnp.int32, st.shape, 1) + c * chunk
                st = jnp.where(key <= qry, st, -jnp.inf)
            m_old = m_ref[:, cols]
            m_new = jnp.maximum(m_old, jnp.max(st, axis=0, keepdims=True))
            p = jnp.exp2(st - m_new)
            alpha = jnp.exp2(m_old - m_new)
            l_ref[:, cols] = alpha * l_ref[:, cols] + jnp.sum(p, axis=0, keepdims=True)
            acc_ref[:, cols] = alpha * acc_ref[:, cols] + jnp.dot(vt, p.astype(BF16),
                                                                  preferred_element_type=F32)
            m_ref[:, cols] = m_new

    scores(0, sa_ref)

    def pair(t, carry):
        scores(2 * t + 1, sb_ref)
        update(2 * t, sa_ref, False)
        scores(2 * t + 2, sa_ref)
        update(2 * t + 1, sb_ref, False)
        return carry

    lax.fori_loop(0, qi // 2, pair, 0)

    @pl.when(qi % 2 == 0)
    def _():
        update(qi, sa_ref, True)

    @pl.when(qi % 2 == 1)
    def _():
        scores(qi, sb_ref)
        update(qi - 1, sa_ref, False)
        update(qi, sb_ref, True)

    o_ref[...] = (acc_ref[...] / l_ref[...]).T.astype(o_ref.dtype)


def _flash(qt, k, vt, *, chunk=256):
    b, h, nb, _, blk = qt.shape
    s = nb * blk
    return pl.pallas_call(
        functools.partial(_flash_kernel, blk=blk, chunk=min(chunk, blk), scale=QK_HEAD ** -0.5),
        grid=(b, h, nb),
        in_specs=[
            pl.BlockSpec((None, None, None, QK_PAD, blk), lambda bi, hi, qi: (bi, hi, qi, 0, 0)),
            pl.BlockSpec((None, None, s, QK_PAD), lambda bi, hi, qi: (bi, hi, 0, 0)),
            pl.BlockSpec((None, None, nb, V_HEAD, blk), lambda bi, hi, qi: (bi, hi, 0, 0, 0)),
        ],
        out_specs=pl.BlockSpec((None, blk, V_HEAD), lambda bi, hi, qi: (bi, qi, hi)),
        out_shape=jax.ShapeDtypeStruct((b, s, h * V_HEAD), BF16),
        scratch_shapes=[
            pltpu.VMEM((blk // min(chunk, blk), blk, min(chunk, blk)), F32),
            pltpu.VMEM((blk // min(chunk, blk), blk, min(chunk, blk)), F32),
            pltpu.VMEM((1, blk), F32),
            pltpu.VMEM((1, blk), F32),
            pltpu.VMEM((V_HEAD, blk), F32),
        ],
        compiler_params=_params("parallel", "parallel", "arbitrary"),
        name="mla_flash",
    )(qt, k, vt)


def _merge_kernel(a_ref, b_ref, wa_ref, wb_ref, gp_ref, gm_ref, o_ref):
    pa = jnp.dot(a_ref[...], wa_ref[...], preferred_element_type=F32)
    pb = jnp.dot(b_ref[...], wb_ref[...], preferred_element_type=F32)
    o_ref[...] = (jax.nn.sigmoid(gp_ref[...]) * pa + jax.nn.sigmoid(gm_ref[...]) * pb).astype(BF16)


def _merge(a, b, wa, wb, z_main, gate_off, *, bm=512, bn=1024):
    t, pw = a.shape
    mw = b.shape[1]
    d = wa.shape[1]
    bm, bn = _tile(t, bm), _tile(math.gcd(d, gate_off), bn)
    gp0, gm0 = gate_off // bn, (gate_off + d) // bn
    return pl.pallas_call(
        _merge_kernel,
        grid=(t // bm, d // bn),
        in_specs=[
            pl.BlockSpec((bm, pw), lambda i, j: (i, 0)),
            pl.BlockSpec((bm, mw), lambda i, j: (i, 0)),
            pl.BlockSpec((pw, bn), lambda i, j: (0, j)),
            pl.BlockSpec((mw, bn), lambda i, j: (0, j)),
            pl.BlockSpec((bm, bn), lambda i, j: (i, gp0 + j)),
            pl.BlockSpec((bm, bn), lambda i, j: (i, gm0 + j)),
        ],
        out_specs=pl.BlockSpec((bm, bn), lambda i, j: (i, j)),
        out_shape=jax.ShapeDtypeStruct((t, d), BF16),
        compiler_params=_params("parallel", "arbitrary"),
        name="branch_merge",
    )(a, b, wa, wb, z_main, z_main)


def _proj_residual_kernel(a_ref, w_ref, h_ref, o_ref):
    o_ref[...] = h_ref[...] + jnp.dot(a_ref[...], w_ref[...], preferred_element_type=F32)


def _proj_residual(a, w, h, *, bm=512, bn=1024):
    t, kdim = a.shape
    d = w.shape[1]
    bm, bn = _tile(t, bm), _tile(d, bn)
    return pl.pallas_call(
        _proj_residual_kernel,
        grid=(t // bm, d // bn),
        in_specs=[
            pl.BlockSpec((bm, kdim), lambda i, j: (i, 0)),
            pl.BlockSpec((kdim, bn), lambda i, j: (0, j)),
            pl.BlockSpec((bm, bn), lambda i, j: (i, j)),
        ],
        out_specs=pl.BlockSpec((bm, bn), lambda i, j: (i, j)),
        out_shape=jax.ShapeDtypeStruct((t, d), F32),
        compiler_params=_params("parallel", "arbitrary"),
        name="out_proj",
    )(a, w, h)


def _mem_kv_kernel(mem_ref, gain_ref, w_ref, gk_ref, k_ref, v_ref):
    mn = _rms(mem_ref[...], gain_ref[...]).astype(BF16)
    kv = jnp.dot(mn, w_ref[...], preferred_element_type=F32)
    xw = X_HEADS * X_HEAD_DIM
    for hh in range(X_HEADS):
        cols = slice(hh * X_HEAD_DIM, (hh + 1) * X_HEAD_DIM)
        k_ref[:, cols] = _rms(kv[:, cols], gk_ref[...]).astype(BF16)
    v_ref[...] = kv[:, xw:].astype(BF16)


def _mem_kv(mem, gain, w_xkv, gk):
    b, m, d = mem.shape
    xw = X_HEADS * X_HEAD_DIM
    return pl.pallas_call(
        _mem_kv_kernel,
        grid=(b,),
        in_specs=[
            pl.BlockSpec((None, m, d), lambda bi: (bi, 0, 0)),
            pl.BlockSpec((1, d), lambda bi: (0, 0)),
            pl.BlockSpec((d, 2 * xw), lambda bi: (0, 0)),
            pl.BlockSpec((1, X_HEAD_DIM), lambda bi: (0, 0)),
        ],
        out_specs=[
            pl.BlockSpec((None, m, xw), lambda bi: (bi, 0, 0)),
            pl.BlockSpec((None, m, xw), lambda bi: (bi, 0, 0)),
        ],
        out_shape=[
            jax.ShapeDtypeStruct((b, m, xw), BF16),
            jax.ShapeDtypeStruct((b, m, xw), BF16),
        ],
        compiler_params=_params("parallel"),
        name="mem_kv",
    )(mem, gain.reshape(1, d), w_xkv, gk)


def _cross_kernel(h_ref, gain_ref, wq_ref, gq_ref, k_ref, v_ref, wo_ref, o_ref, xo_ref):
    h = h_ref[...]
    uq = _rms(h, gain_ref[...]).astype(BF16)
    xq = jnp.dot(uq, wq_ref[...], preferred_element_type=F32)
    scale = X_HEAD_DIM ** -0.5
    for hh in range(X_HEADS):
        cols = slice(hh * X_HEAD_DIM, (hh + 1) * X_HEAD_DIM)
        qh = _rms(xq[:, cols], gq_ref[...]).astype(BF16)
        s = lax.dot_general(qh, k_ref[:, cols], (((1,), (1,)), ((), ())),
                            preferred_element_type=F32) * scale
        p = jnp.exp(s - jnp.max(s, axis=-1, keepdims=True))
        p = (p / jnp.sum(p, axis=-1, keepdims=True)).astype(BF16)
        xo_ref[:, cols] = jnp.dot(p, v_ref[:, cols], preferred_element_type=F32).astype(BF16)
    o_ref[...] = h + jnp.dot(xo_ref[...], wo_ref[...], preferred_element_type=F32)


def _cross(h, gain, wq, gq, xk, xv, wo, seq, *, bm=256):
    t, d = h.shape
    m = xk.shape[1]
    xw = X_HEADS * X_HEAD_DIM
    bm = _tile(seq, bm)
    nb = seq // bm
    return pl.pallas_call(
        _cross_kernel,
        grid=(t // bm,),
        in_specs=[
            pl.BlockSpec((bm, d), lambda i: (i, 0)),
            pl.BlockSpec((1, d), lambda i: (0, 0)),
            pl.BlockSpec((d, xw), lambda i: (0, 0)),
            pl.BlockSpec((1, X_HEAD_DIM), lambda i: (0, 0)),
            pl.BlockSpec((None, m, xw), lambda i: (i // nb, 0, 0)),
            pl.BlockSpec((None, m, xw), lambda i: (i // nb, 0, 0)),
            pl.BlockSpec((xw, d), lambda i: (0, 0)),
        ],
        out_specs=pl.BlockSpec((bm, d), lambda i: (i, 0)),
        out_shape=jax.ShapeDtypeStruct((t, d), F32),
        scratch_shapes=[pltpu.VMEM((bm, xw), BF16)],
        compiler_params=_params("parallel"),
        name="cross_attn",
    )(h, gain.reshape(1, d), wq, gq, xk, xv, wo)


def _pad_lanes(v, width):
    return jnp.pad(v, (0, width - v.shape[0])).reshape(1, width)


def kernel(x, mem, positions, ffn1_norm, ffn1_w_gu, ffn1_w_down, mix_norm, w_in, w_pool, pool_scale,
           q_latent_norm, kv_latent_norm, w_uq, w_ukv, q_nope_norm, k_nope_norm, q_rope_norm,
           k_rope_norm, w_branch_pool, w_branch_mla, w_out, x_norm, mem_norm, w_xq, w_xkv, xq_norm,
           xk_norm, w_xo, ffn2_norm, ffn2_w_gu, ffn2_w_down):
    batch, seq, d = x.shape
    t = batch * seq
    depth = w_in.shape[0]
    pw = w_branch_pool.shape[1]
    ql = q_latent_norm.shape[1]
    kvl = kv_latent_norm.shape[1]
    heads = w_uq.shape[2] // QK_HEAD

    half = QK_ROPE // 2
    inv = 1.0 / (ROPE_THETA ** (jnp.arange(half, dtype=F32) * (2.0 / QK_ROPE)))
    inv = _pad_lanes(jnp.concatenate([inv, inv]), LANES)
    pos = positions.reshape(t, 1)

    h = x.reshape(t, d)
    for l in range(depth):
        h = _ffn(h, ffn1_norm[l], ffn1_w_gu[l].astype(BF16), ffn1_w_down[l].astype(BF16))

        o_q, o_kv, o_kr, o_gp = pw, pw + ql, pw + ql + kvl, pw + ql + kvl + QK_ROPE
        wi = w_in[l]
        w_main = jnp.concatenate([wi[:, :o_kv], wi[:, o_gp:]], axis=1).astype(BF16)
        w_tail = jnp.pad(wi[:, o_kv:o_gp], ((0, 0), (0, LANES - QK_ROPE))).astype(BF16)
        z_main = _norm_matmul(h, mix_norm[l], w_main, name="in_proj_main")
        z_tail = _norm_matmul(h, mix_norm[l], w_tail, name="in_proj_tail")

        a_out = _pool(z_main, w_pool[l].astype(BF16), pool_scale[l], batch, seq)

        wuq = jnp.pad(w_uq[l].reshape(ql, heads, QK_HEAD), ((0, 0), (0, 0), (0, QK_PAD - QK_HEAD)))
        wuq = wuq.transpose(1, 0, 2).astype(BF16)
        wukv = w_ukv[l].reshape(kvl, heads, QK_NOPE + V_HEAD).transpose(1, 0, 2).astype(BF16)
        qt, k, vt = _qkv(z_main, z_tail, pos, inv,
                       q_latent_norm[l].reshape(1, ql), kv_latent_norm[l].reshape(1, kvl),
                       q_nope_norm[l].reshape(1, QK_NOPE), k_nope_norm[l].reshape(1, QK_NOPE),
                       _pad_lanes(q_rope_norm[l], LANES), _pad_lanes(k_rope_norm[l], LANES),
                       wuq, wukv, batch, seq, pw)
        b_out = _flash(qt, k, vt).reshape(t, heads * V_HEAD)

        merged = _merge(a_out, b_out, w_branch_pool[l].astype(BF16), w_branch_mla[l].astype(BF16),
                        z_main, pw + ql)
        h = _proj_residual(merged, w_out[l].astype(BF16), h)

        xk, xv = _mem_kv(mem, mem_norm[l], w_xkv[l].astype(BF16), xk_norm[l].reshape(1, X_HEAD_DIM))
        h = _cross(h, x_norm[l], w_xq[l].astype(BF16), xq_norm[l].reshape(1, X_HEAD_DIM), xk, xv,
                   w_xo[l].astype(BF16), seq)

        h = _ffn(h, ffn2_norm[l], ffn2_w_gu[l].astype(BF16), ffn2_w_down[l].astype(BF16))
    return h.reshape(batch, seq, d)
```

```python
import functools
import math

import jax
import jax.numpy as jnp
from jax import lax
from jax.experimental import pallas as pl
from jax.experimental.pallas import tpu as pltpu

F32 = jnp.float32
BF16 = jnp.bfloat16

POOL_WINDOWS = (2, 4, 8, 16)
POOL_HALO = 16
QK_NOPE = 128
QK_ROPE = 64
QK_HEAD = QK_NOPE + QK_ROPE
QK_PAD = 256
V_HEAD = 128
X_HEADS = 4
X_HEAD_DIM = 128
ROPE_THETA = 10000.0
EPS = 1e-6
LANES = 128
VMEM_LIMIT = 60 * 1024 * 1024


def _params(*sem):
    return pltpu.CompilerParams(dimension_semantics=sem, vmem_limit_bytes=VMEM_LIMIT)


def _tile(n, pref):
    if n <= pref:
        return n
    t = (pref // LANES) * LANES
    while t >= LANES:
        if n % t == 0:
            return t
        t -= LANES
    raise ValueError(f"no lane-aligned tile of {n} below {pref}")


def _rms(x, gain):
    ms = jnp.mean(x * x, axis=-1, keepdims=True)
    return x * lax.rsqrt(ms + EPS) * gain


def _ffn_kernel(x_ref, gain_ref, wgu_ref, wd_ref, o_ref, xn_ref, hid0_ref, hid1_ref, *, bf, nf):
    j = pl.program_id(1)
    hid_refs = (hid0_ref, hid1_ref)

    def gate_up():
        gu = jnp.dot(xn_ref[...], wgu_ref[...], preferred_element_type=F32)
        return gu[:, :bf], gu[:, bf:]

    def act(g, u):
        return (g * jax.nn.sigmoid(g) * u * 0.5).astype(BF16)

    def down(hid_ref):
        o_ref[...] += jnp.dot(hid_ref[...], wd_ref[...], preferred_element_type=F32)

    @pl.when(j == 0)
    def _():
        x = x_ref[...]
        xn_ref[...] = _rms(x, gain_ref[...]).astype(BF16)
        o_ref[...] = x
        hid0_ref[...] = act(*gate_up())

    for parity in (0, 1):
        @pl.when((j > 0) & (j < nf) & (j % 2 == parity))
        def _():
            g, u = gate_up()
            down(hid_refs[1 - parity])
            hid_refs[parity][...] = act(g, u)

    @pl.when(j == nf)
    def _():
        down(hid_refs[(nf - 1) % 2])


def _ffn(x, gain, w_gu, w_down, *, bm=512, bf=256):
    t, d = x.shape
    nf, _, bf2 = w_gu.shape
    bf = bf2 // 2
    bm = _tile(t, bm)
    return pl.pallas_call(
        functools.partial(_ffn_kernel, bf=bf, nf=nf),
        grid=(t // bm, nf + 1),
        in_specs=[
            pl.BlockSpec((bm, d), lambda i, j: (i, 0)),
            pl.BlockSpec((1, d), lambda i, j: (0, 0)),
            pl.BlockSpec((None, d, 2 * bf), lambda i, j: (jnp.minimum(j, nf - 1), 0, 0)),
            pl.BlockSpec((bf, d), lambda i, j: (jnp.maximum(j - 1, 0), 0)),
        ],
        out_specs=pl.BlockSpec((bm, d), lambda i, j: (i, 0)),
        out_shape=jax.ShapeDtypeStruct((t, d), F32),
        scratch_shapes=[pltpu.VMEM((bm, d), BF16), pltpu.VMEM((bm, bf), BF16), pltpu.VMEM((bm, bf), BF16)],
        compiler_params=_params("parallel", "arbitrary"),
        name="ffn",
    )(x, gain.reshape(1, d), w_gu, w_down)


def _tile_gate_up(w_gu, bf=256):
    d, f2 = w_gu.shape
    bf = _tile(f2 // 2, bf)
    nf = f2 // 2 // bf
    return w_gu.reshape(d, 2, nf, bf).transpose(2, 0, 1, 3).reshape(nf, d, 2 * bf).astype(BF16)


NORM_ROWS = 256


def _norm_matmul_kernel(x_ref, gain_ref, w_ref, o_ref, xn_ref):
    @pl.when(pl.program_id(1) == 0)
    def _():
        for r in range(0, x_ref.shape[0], NORM_ROWS):
            rows = slice(r, min(r + NORM_ROWS, x_ref.shape[0]))
            xn_ref[rows, :] = _rms(x_ref[rows, :], gain_ref[...]).astype(BF16)

    o_ref[...] = jnp.dot(xn_ref[...], w_ref[...], preferred_element_type=F32)


def _norm_matmul(x, gain, w, *, bm=1024, bn=1024, name):
    t, d = x.shape
    n = w.shape[1]
    bm, bn = _tile(t, bm), _tile(n, bn)
    return pl.pallas_call(
        _norm_matmul_kernel,
        grid=(t // bm, n // bn),
        in_specs=[
            pl.BlockSpec((bm, d), lambda i, j: (i, 0), pipeline_mode=pl.Buffered(1)),
            pl.BlockSpec((1, d), lambda i, j: (0, 0)),
            pl.BlockSpec((d, bn), lambda i, j: (0, j)),
        ],
        out_specs=pl.BlockSpec((bm, bn), lambda i, j: (i, j)),
        out_shape=jax.ShapeDtypeStruct((t, n), F32),
        scratch_shapes=[pltpu.VMEM((bm, d), BF16)],
        compiler_params=_params("parallel", "arbitrary"),
        name=name,
    )(x, gain.reshape(1, d), w)


def _pool_kernel(zc_ref, zp_ref, wp_ref, sc_ref, o_ref, ext_ref, *, bm, c):
    i = pl.program_id(1)
    ext_ref[0:POOL_HALO, :] = jnp.where(i > 0, zp_ref[...], 0.0)
    ext_ref[POOL_HALO:POOL_HALO + bm, :] = zc_ref[...]
    t = i * bm + lax.broadcasted_iota(jnp.int32, (bm, 1), 0)
    for g, w in enumerate(POOL_WINDOWS):
        cols = slice(g * c, (g + 1) * c)
        x = zc_ref[:, cols]
        acc = x
        for k in range(1, w):
            acc = acc + ext_ref[POOL_HALO - k:POOL_HALO - k + bm, cols]
        cnt = jnp.minimum(t + 1, w).astype(F32)
        mixed = (acc / cnt - x).astype(BF16)
        y = jnp.dot(mixed, wp_ref[g], preferred_element_type=F32) * sc_ref[:, cols]
        o_ref[:, cols] = y.astype(BF16)


def _pool(z_main, w_pool, pool_scale, batch, seq, *, bm=512):
    g, c, _ = w_pool.shape
    pw = g * c
    bm = _tile(seq, bm)
    nb = seq // bm
    halo_blocks = bm // POOL_HALO
    return pl.pallas_call(
        functools.partial(_pool_kernel, bm=bm, c=c),
        grid=(batch, nb),
        in_specs=[
            pl.BlockSpec((bm, pw), lambda b, i: (b * nb + i, 0)),
            pl.BlockSpec((POOL_HALO, pw),
                         lambda b, i: (jnp.maximum((b * nb + i) * halo_blocks - 1, 0), 0)),
            pl.BlockSpec((g, c, c), lambda b, i: (0, 0, 0)),
            pl.BlockSpec((1, pw), lambda b, i: (0, 0)),
        ],
        out_specs=pl.BlockSpec((bm, pw), lambda b, i: (b * nb + i, 0)),
        out_shape=jax.ShapeDtypeStruct((batch * seq, pw), BF16),
        scratch_shapes=[pltpu.VMEM((POOL_HALO + bm, pw), F32)],
        compiler_params=_params("parallel", "parallel"),
        name="pool",
    )(z_main, z_main, w_pool, pool_scale.reshape(1, pw))


def _rope_tables(pos, inv):
    ang = pos.astype(F32) * inv
    cos, sin = jnp.cos(ang), jnp.sin(ang)
    lane = lax.broadcasted_iota(jnp.int32, ang.shape, 1)
    half = QK_ROPE // 2
    c = jnp.where(lane < QK_ROPE, cos, 0.0)
    sa = jnp.where(lane < half, -sin, 0.0)
    sb = jnp.where((lane >= half) & (lane < QK_ROPE), sin, 0.0)
    return c, sa, sb


def _rope_norm(r, gain, c, sa, sb):
    ms = jnp.sum(r * r, axis=-1, keepdims=True) * (1.0 / QK_ROPE)
    r = r * lax.rsqrt(ms + EPS) * gain
    half = QK_ROPE // 2
    return r * c + pltpu.roll(r, LANES - half, 1) * sa + pltpu.roll(r, half, 1) * sb


def _qkv_kernel(zq_ref, zkv_ref, zkr_ref, pos_ref, inv_ref, gql_ref, gkvl_ref, gqn_ref, gkn_ref,
                gqr_ref, gkr_ref, wuq_ref, wukv_ref, qt_ref, k_ref, vt_ref,
                cq_ref, ckv_ref, kr_ref, c_ref, sa_ref, sb_ref):
    @pl.when(pl.program_id(1) == 0)
    def _():
        cq_ref[...] = _rms(zq_ref[...], gql_ref[...]).astype(BF16)
        ckv_ref[...] = _rms(zkv_ref[...], gkvl_ref[...]).astype(BF16)
        c, sa, sb = _rope_tables(pos_ref[...], inv_ref[...])
        c_ref[...], sa_ref[...], sb_ref[...] = c, sa, sb
        kr_ref[...] = _rope_norm(zkr_ref[...], gkr_ref[...], c, sa, sb).astype(BF16)

    q = jnp.dot(cq_ref[...], wuq_ref[...], preferred_element_type=F32)
    qt_ref[:QK_NOPE, :] = _rms(q[:, :QK_NOPE], gqn_ref[...]).T.astype(BF16)
    qt_ref[QK_NOPE:, :] = _rope_norm(q[:, QK_NOPE:], gqr_ref[...], c_ref[...], sa_ref[...],
                                     sb_ref[...]).T.astype(BF16)
    kv = jnp.dot(ckv_ref[...], wukv_ref[...], preferred_element_type=F32)
    k_ref[:, :QK_NOPE] = _rms(kv[:, :QK_NOPE], gkn_ref[...]).astype(BF16)
    k_ref[:, QK_NOPE:] = kr_ref[...]
    vt_ref[...] = kv[:, QK_NOPE:].T.astype(BF16)


def _qkv(z_main, z_tail, pos, inv, gql, gkvl, gqn, gkn, gqr, gkr, wuq, wukv, batch, seq, pw, *, bm=512):
    h, ql, _ = wuq.shape
    kvl = wukv.shape[1]
    bm = _tile(seq, bm)
    nb = seq // bm
    t = batch * seq
    row = lambda i, hh: (i, 0)
    const = lambda i, hh: (0, 0)
    out_map = lambda i, hh: (i // nb, hh, i % nb, 0)
    out_map_t = lambda i, hh: (i // nb, hh, i % nb, 0, 0)
    return pl.pallas_call(
        _qkv_kernel,
        grid=(t // bm, h),
        in_specs=[
            pl.BlockSpec((bm, ql), lambda i, hh: (i, pw // ql)),
            pl.BlockSpec((bm, kvl), row),
            pl.BlockSpec((bm, LANES), lambda i, hh: (i, kvl // LANES)),
            pl.BlockSpec((bm, 1), row),
            pl.BlockSpec((1, LANES), const),
            pl.BlockSpec((1, ql), const),
            pl.BlockSpec((1, kvl), const),
            pl.BlockSpec((1, QK_NOPE), const),
            pl.BlockSpec((1, QK_NOPE), const),
            pl.BlockSpec((1, LANES), const),
            pl.BlockSpec((1, LANES), const),
            pl.BlockSpec((None, ql, QK_PAD), lambda i, hh: (hh, 0, 0)),
            pl.BlockSpec((None, kvl, QK_NOPE + V_HEAD), lambda i, hh: (hh, 0, 0)),
        ],
        out_specs=[
            pl.BlockSpec((None, None, None, QK_PAD, bm), out_map_t),
            pl.BlockSpec((None, None, bm, QK_PAD), out_map),
            pl.BlockSpec((None, None, None, V_HEAD, bm), out_map_t),
        ],
        out_shape=[
            jax.ShapeDtypeStruct((batch, h, nb, QK_PAD, bm), BF16),
            jax.ShapeDtypeStruct((batch, h, seq, QK_PAD), BF16),
            jax.ShapeDtypeStruct((batch, h, nb, V_HEAD, bm), BF16),
        ],
        scratch_shapes=[
            pltpu.VMEM((bm, ql), BF16),
            pltpu.VMEM((bm, kvl), BF16),
            pltpu.VMEM((bm, LANES), BF16),
            pltpu.VMEM((bm, LANES), F32),
            pltpu.VMEM((bm, LANES), F32),
            pltpu.VMEM((bm, LANES), F32),
        ],
        compiler_params=_params("parallel", "arbitrary"),
        name="qkv_prep",
    )(z_main, z_tail, z_tail, pos, inv, gql, gkvl, gqn, gkn, gqr, gkr, wuq, wukv)


def _flash_kernel(qt_ref, k_ref, vt_ref, o_ref, sa_ref, sb_ref, m_ref, l_ref, acc_ref, *, blk, chunk, scale):
    qi = pl.program_id(2)
    nch = blk // chunk
    c2 = scale * math.log2(math.e)
    m_ref[...] = jnp.full(m_ref.shape, -jnp.inf, F32)
    l_ref[...] = jnp.zeros(l_ref.shape, F32)
    acc_ref[...] = jnp.zeros(acc_ref.shape, F32)

    def scores(kj, s_ref):
        start = pl.multiple_of(kj * blk, blk)
        k = k_ref[pl.ds(start, blk), :]
        for c in range(nch):
            s_ref[c] = jnp.dot(k, qt_ref[:, c * chunk:(c + 1) * chunk], preferred_element_type=F32)

    def update(kj, s_ref, masked):
        vt = vt_ref[kj]
        for c in range(nch):
            cols = slice(c * chunk, (c + 1) * chunk)
            st = s_ref[c] * c2
            if masked:
                key = lax.broadcasted_iota(jnp.int32, st.shape, 0)
                qry = lax.broadcasted_iota(jnp.int32, st.shape, 1) + c * chunk
                st = jnp.where(key <= qry, st, -jnp.inf)
            m_old = m_ref[:, cols]
            m_new = jnp.maximum(m_old, jnp.max(st, axis=0, keepdims=True))
            p = jnp.exp2(st - m_new)
            alpha = jnp.exp2(m_old - m_new)
            l_ref[:, cols] = alpha * l_ref[:, cols] + jnp.sum(p, axis=0, keepdims=True)
            acc_ref[:, cols] = alpha * acc_ref[:, cols] + jnp.dot(vt, p.astype(BF16),
                                                                  preferred_element_type=F32)
            m_ref[:, cols] = m_new

    scores(0, sa_ref)

    def pair(t, carry):
        scores(2 * t + 1, sb_ref)
        update(2 * t, sa_ref, False)
        scores(2 * t + 2, sa_ref)
        update(2 * t + 1, sb_ref, False)
        return carry

    lax.fori_loop(0, qi // 2, pair, 0)

    @pl.when(qi % 2 == 0)
    def _():
        update(qi, sa_ref, True)

    @pl.when(qi % 2 == 1)
    def _():
        scores(qi, sb_ref)
        update(qi - 1, sa_ref, False)
        update(qi, sb_ref, True)

    o_ref[...] = (acc_ref[...] / l_ref[...]).T.astype(o_ref.dtype)


def _flash(qt, k, vt, *, chunk=256):
    b, h, nb, _, blk = qt.shape
    s = nb * blk
    return pl.pallas_call(
        functools.partial(_flash_kernel, blk=blk, chunk=min(chunk, blk), scale=QK_HEAD ** -0.5),
        grid=(b, h, nb),
        in_specs=[
            pl.BlockSpec((None, None, None, QK_PAD, blk), lambda bi, hi, qi: (bi, hi, qi, 0, 0)),
            pl.BlockSpec((None, None, s, QK_PAD), lambda bi, hi, qi: (bi, hi, 0, 0)),
            pl.BlockSpec((None, None, nb, V_HEAD, blk), lambda bi, hi, qi: (bi, hi, 0, 0, 0)),
        ],
        out_specs=pl.BlockSpec((None, blk, V_HEAD), lambda bi, hi, qi: (bi, qi, hi)),
        out_shape=jax.ShapeDtypeStruct((b, s, h * V_HEAD), BF16),
        scratch_shapes=[
            pltpu.VMEM((blk // min(chunk, blk), blk, min(chunk, blk)), F32),
            pltpu.VMEM((blk // min(chunk, blk), blk, min(chunk, blk)), F32),
            pltpu.VMEM((1, blk), F32),
            pltpu.VMEM((1, blk), F32),
            pltpu.VMEM((V_HEAD, blk), F32),
        ],
        compiler_params=_params("parallel", "parallel", "arbitrary"),
        name="mla_flash",
    )(qt, k, vt)


def _merge_kernel(a_ref, b_ref, wa_ref, wb_ref, gp_ref, gm_ref, o_ref):
    pa = jnp.dot(a_ref[...], wa_ref[...], preferred_element_type=F32)
    pb = jnp.dot(b_ref[...], wb_ref[...], preferred_element_type=F32)
    o_ref[...] = (jax.nn.sigmoid(gp_ref[...]) * pa + jax.nn.sigmoid(gm_ref[...]) * pb).astype(BF16)


def _merge(a, b, wa, wb, z_main, gate_off, *, bm=1024, bn=512):
    t, pw = a.shape
    mw = b.shape[1]
    d = wa.shape[1]
    bm, bn = _tile(t, bm), _tile(math.gcd(d, gate_off), bn)
    gp0, gm0 = gate_off // bn, (gate_off + d) // bn
    return pl.pallas_call(
        _merge_kernel,
        grid=(t // bm, d // bn),
        in_specs=[
            pl.BlockSpec((bm, pw), lambda i, j: (i, 0)),
            pl.BlockSpec((bm, mw), lambda i, j: (i, 0)),
            pl.BlockSpec((pw, bn), lambda i, j: (0, j)),
            pl.BlockSpec((mw, bn), lambda i, j: (0, j)),
            pl.BlockSpec((bm, bn), lambda i, j: (i, gp0 + j)),
            pl.BlockSpec((bm, bn), lambda i, j: (i, gm0 + j)),
        ],
        out_specs=pl.BlockSpec((bm, bn), lambda i, j: (i, j)),
        out_shape=jax.ShapeDtypeStruct((t, d), BF16),
        compiler_params=_params("parallel", "arbitrary"),
        name="branch_merge",
    )(a, b, wa, wb, z_main, z_main)


def _proj_residual_kernel(a_ref, w_ref, h_ref, o_ref):
    o_ref[...] = h_ref[...] + jnp.dot(a_ref[...], w_ref[...], preferred_element_type=F32)


def _proj_residual(a, w, h, *, bm=1024, bn=1024):
    t, kdim = a.shape
    d = w.shape[1]
    bm, bn = _tile(t, bm), _tile(d, bn)
    return pl.pallas_call(
        _proj_residual_kernel,
        grid=(t // bm, d // bn),
        in_specs=[
            pl.BlockSpec((bm, kdim), lambda i, j: (i, 0)),
            pl.BlockSpec((kdim, bn), lambda i, j: (0, j)),
            pl.BlockSpec((bm, bn), lambda i, j: (i, j)),
        ],
        out_specs=pl.BlockSpec((bm, bn), lambda i, j: (i, j)),
        out_shape=jax.ShapeDtypeStruct((t, d), F32),
        compiler_params=_params("parallel", "arbitrary"),
        name="out_proj",
    )(a, w, h)


def _mem_kv_kernel(mem_ref, gain_ref, w_ref, gk_ref, k_ref, v_ref):
    mn = _rms(mem_ref[...], gain_ref[...]).astype(BF16)
    kv = jnp.dot(mn, w_ref[...], preferred_element_type=F32)
    xw = X_HEADS * X_HEAD_DIM
    for hh in range(X_HEADS):
        cols = slice(hh * X_HEAD_DIM, (hh + 1) * X_HEAD_DIM)
        k_ref[:, cols] = _rms(kv[:, cols], gk_ref[...]).astype(BF16)
    v_ref[...] = kv[:, xw:].astype(BF16)


def _mem_kv(mem, gain, w_xkv, gk):
    b, m, d = mem.shape
    xw = X_HEADS * X_HEAD_DIM
    return pl.pallas_call(
        _mem_kv_kernel,
        grid=(b,),
        in_specs=[
            pl.BlockSpec((None, m, d), lambda bi: (bi, 0, 0)),
            pl.BlockSpec((1, d), lambda bi: (0, 0)),
            pl.BlockSpec((d, 2 * xw), lambda bi: (0, 0)),
            pl.BlockSpec((1, X_HEAD_DIM), lambda bi: (0, 0)),
        ],
        out_specs=[
            pl.BlockSpec((None, m, xw), lambda bi: (bi, 0, 0)),
            pl.BlockSpec((None, m, xw), lambda bi: (bi, 0, 0)),
        ],
        out_shape=[
            jax.ShapeDtypeStruct((b, m, xw), BF16),
            jax.ShapeDtypeStruct((b, m, xw), BF16),
        ],
        compiler_params=_params("parallel"),
        name="mem_kv",
    )(mem, gain.reshape(1, d), w_xkv, gk)


def _cross_kernel(h_ref, gain_ref, wq_ref, gq_ref, k_ref, v_ref, wo_ref, o_ref, xo_ref):
    h = h_ref[...]
    uq = _rms(h, gain_ref[...]).astype(BF16)
    xq = jnp.dot(uq, wq_ref[...], preferred_element_type=F32)
    scale = X_HEAD_DIM ** -0.5
    for hh in range(X_HEADS):
        cols = slice(hh * X_HEAD_DIM, (hh + 1) * X_HEAD_DIM)
        qh = _rms(xq[:, cols], gq_ref[...]).astype(BF16)
        s = lax.dot_general(qh, k_ref[:, cols], (((1,), (1,)), ((), ())),
                            preferred_element_type=F32) * scale
        p = jnp.exp(s - jnp.max(s, axis=-1, keepdims=True))
        p = (p / jnp.sum(p, axis=-1, keepdims=True)).astype(BF16)
        xo_ref[:, cols] = jnp.dot(p, v_ref[:, cols], preferred_element_type=F32).astype(BF16)
    o_ref[...] = h + jnp.dot(xo_ref[...], wo_ref[...], preferred_element_type=F32)


def _cross(h, gain, wq, gq, xk, xv, wo, seq, *, bm=256):
    t, d = h.shape
    m = xk.shape[1]
    xw = X_HEADS * X_HEAD_DIM
    bm = _tile(seq, bm)
    nb = seq // bm
    return pl.pallas_call(
        _cross_kernel,
        grid=(t // bm,),
        in_specs=[
            pl.BlockSpec((bm, d), lambda i: (i, 0)),
            pl.BlockSpec((1, d), lambda i: (0, 0)),
            pl.BlockSpec((d, xw), lambda i: (0, 0)),
            pl.BlockSpec((1, X_HEAD_DIM), lambda i: (0, 0)),
            pl.BlockSpec((None, m, xw), lambda i: (i // nb, 0, 0)),
            pl.BlockSpec((None, m, xw), lambda i: (i // nb, 0, 0)),
            pl.BlockSpec((xw, d), lambda i: (0, 0)),
        ],
        out_specs=pl.BlockSpec((bm, d), lambda i: (i, 0)),
        out_shape=jax.ShapeDtypeStruct((t, d), F32),
        scratch_shapes=[pltpu.VMEM((bm, xw), BF16)],
        compiler_params=_params("parallel"),
        name="cross_attn",
    )(h, gain.reshape(1, d), wq, gq, xk, xv, wo)


def _pad_lanes(v, width):
    return jnp.pad(v, (0, width - v.shape[0])).reshape(1, width)


def kernel(x, mem, positions, ffn1_norm, ffn1_w_gu, ffn1_w_down, mix_norm, w_in, w_pool, pool_scale,
           q_latent_norm, kv_latent_norm, w_uq, w_ukv, q_nope_norm, k_nope_norm, q_rope_norm,
           k_rope_norm, w_branch_pool, w_branch_mla, w_out, x_norm, mem_norm, w_xq, w_xkv, xq_norm,
           xk_norm, w_xo, ffn2_norm, ffn2_w_gu, ffn2_w_down):
    batch, seq, d = x.shape
    t = batch * seq
    depth = w_in.shape[0]
    pw = w_branch_pool.shape[1]
    ql = q_latent_norm.shape[1]
    kvl = kv_latent_norm.shape[1]
    heads = w_uq.shape[2] // QK_HEAD

    half = QK_ROPE // 2
    inv = 1.0 / (ROPE_THETA ** (jnp.arange(half, dtype=F32) * (2.0 / QK_ROPE)))
    inv = _pad_lanes(jnp.concatenate([inv, inv]), LANES)
    pos = positions.reshape(t, 1)

    h = x.reshape(t, d)
    for l in range(depth):
        h = _ffn(h, ffn1_norm[l], _tile_gate_up(ffn1_w_gu[l]), ffn1_w_down[l].astype(BF16))

        o_q, o_kv, o_kr, o_gp = pw, pw + ql, pw + ql + kvl, pw + ql + kvl + QK_ROPE
        wi = w_in[l]
        w_main = jnp.concatenate([wi[:, :o_kv], wi[:, o_gp:]], axis=1).astype(BF16)
        w_tail = jnp.pad(wi[:, o_kv:o_gp], ((0, 0), (0, LANES - QK_ROPE))).astype(BF16)
        z_main = _norm_matmul(h, mix_norm[l], w_main, name="in_proj_main")
        z_tail = _norm_matmul(h, mix_norm[l], w_tail, name="in_proj_tail")

        a_out = _pool(z_main, w_pool[l].astype(BF16), pool_scale[l], batch, seq)

        wuq = jnp.pad(w_uq[l].reshape(ql, heads, QK_HEAD), ((0, 0), (0, 0), (0, QK_PAD - QK_HEAD)))
        wuq = wuq.transpose(1, 0, 2).astype(BF16)
        wukv = w_ukv[l].reshape(kvl, heads, QK_NOPE + V_HEAD).transpose(1, 0, 2).astype(BF16)
        qt, k, vt = _qkv(z_main, z_tail, pos, inv,
                       q_latent_norm[l].reshape(1, ql), kv_latent_norm[l].reshape(1, kvl),
                       q_nope_norm[l].reshape(1, QK_NOPE), k_nope_norm[l].reshape(1, QK_NOPE),
                       _pad_lanes(q_rope_norm[l], LANES), _pad_lanes(k_rope_norm[l], LANES),
                       wuq, wukv, batch, seq, pw)
        b_out = _flash(qt, k, vt).reshape(t, heads * V_HEAD)

        merged = _merge(a_out, b_out, w_branch_pool[l].astype(BF16), w_branch_mla[l].astype(BF16),
                        z_main, pw + ql)
        h = _proj_residual(merged, w_out[l].astype(BF16), h)

        xk, xv = _mem_kv(mem, mem_norm[l], w_xkv[l].astype(BF16), xk_norm[l].reshape(1, X_HEAD_DIM))
        h = _cross(h, x_norm[l], w_xq[l].astype(BF16), xq_norm[l].reshape(1, X_HEAD_DIM), xk, xv,
                   w_xo[l].astype(BF16), seq)

        h = _ffn(h, ffn2_norm[l], _tile_gate_up(ffn2_w_gu[l]), ffn2_w_down[l].astype(BF16))
    return h.reshape(batch, seq, d)
```

```python
import functools
import math

import jax
import jax.numpy as jnp
from jax import lax
from jax.experimental import pallas as pl
from jax.experimental.pallas import tpu as pltpu

F32 = jnp.float32
BF16 = jnp.bfloat16

POOL_WINDOWS = (2, 4, 8, 16)
POOL_HALO = 16
QK_NOPE = 128
QK_ROPE = 64
QK_HEAD = QK_NOPE + QK_ROPE
QK_PAD = 256
V_HEAD = 128
V_ROWS = V_HEAD + 16
X_HEADS = 4
X_HEAD_DIM = 128
ROPE_THETA = 10000.0
EPS = 1e-6
LANES = 128
VMEM_LIMIT = 60 * 1024 * 1024


def _params(*sem):
    return pltpu.CompilerParams(dimension_semantics=sem, vmem_limit_bytes=VMEM_LIMIT)


def _tile(n, pref):
    if n <= pref:
        return n
    t = (pref // LANES) * LANES
    while t >= LANES:
        if n % t == 0:
            return t
        t -= LANES
    raise ValueError(f"no lane-aligned tile of {n} below {pref}")


def _rms(x, gain):
    ms = jnp.mean(x * x, axis=-1, keepdims=True)
    return x * lax.rsqrt(ms + EPS) * gain


def _ffn_kernel(x_ref, gain_ref, wg_ref, wu_ref, wd_ref, o_ref, xn_ref):
    @pl.when(pl.program_id(1) == 0)
    def _():
        x = x_ref[...]
        xn_ref[...] = _rms(x, gain_ref[...]).astype(BF16)
        o_ref[...] = x

    xn = xn_ref[...]
    g = jnp.dot(xn, wg_ref[...], preferred_element_type=F32)
    u = jnp.dot(xn, wu_ref[...], preferred_element_type=F32)
    hid = (g * jax.nn.sigmoid(g) * u * 0.5).astype(BF16)
    o_ref[...] += jnp.dot(hid, wd_ref[...], preferred_element_type=F32)


def _ffn(x, gain, w_gu, w_down, *, bm=512, bf=256):
    t, d = x.shape
    f = w_down.shape[0]
    bm, bf = _tile(t, bm), _tile(f, bf)
    nf = f // bf
    return pl.pallas_call(
        _ffn_kernel,
        grid=(t // bm, nf),
        in_specs=[
            pl.BlockSpec((bm, d), lambda i, j: (i, 0)),
            pl.BlockSpec((1, d), lambda i, j: (0, 0)),
            pl.BlockSpec((d, bf), lambda i, j: (0, j)),
            pl.BlockSpec((d, bf), lambda i, j: (0, j + nf)),
            pl.BlockSpec((bf, d), lambda i, j: (j, 0)),
        ],
        out_specs=pl.BlockSpec((bm, d), lambda i, j: (i, 0)),
        out_shape=jax.ShapeDtypeStruct((t, d), F32),
        scratch_shapes=[pltpu.VMEM((bm, d), BF16)],
        compiler_params=_params("parallel", "arbitrary"),
        name="ffn",
    )(x, gain.reshape(1, d), w_gu, w_gu, w_down)


def _ffn_shifted_kernel(x_ref, gain_ref, wg_ref, wu_ref, wd_ref, o_ref, xn_ref, hid0_ref, hid1_ref, *, nf):
    j = pl.program_id(1)
    hid_refs = (hid0_ref, hid1_ref)

    def hidden():
        xn = xn_ref[...]
        g = jnp.dot(xn, wg_ref[...], preferred_element_type=F32)
        u = jnp.dot(xn, wu_ref[...], preferred_element_type=F32)
        return g, u

    def act(g, u):
        return (g * jax.nn.sigmoid(g) * u * 0.5).astype(BF16)

    def down(hid_ref):
        o_ref[...] += jnp.dot(hid_ref[...], wd_ref[...], preferred_element_type=F32)

    @pl.when(j == 0)
    def _():
        x = x_ref[...]
        xn_ref[...] = _rms(x, gain_ref[...]).astype(BF16)
        o_ref[...] = x
        hid0_ref[...] = act(*hidden())

    for parity in (0, 1):
        @pl.when((j > 0) & (j < nf) & (j % 2 == parity))
        def _():
            g, u = hidden()
            down(hid_refs[1 - parity])
            hid_refs[parity][...] = act(g, u)

    @pl.when(j == nf)
    def _():
        down(hid_refs[(nf - 1) % 2])


def _ffn_shifted(x, gain, w_gu, w_down, *, bm=512, bf=256):
    t, d = x.shape
    f = w_down.shape[0]
    bm, bf = _tile(t, bm), _tile(f, bf)
    nf = f // bf
    return pl.pallas_call(
        functools.partial(_ffn_shifted_kernel, nf=nf),
        grid=(t // bm, nf + 1),
        in_specs=[
            pl.BlockSpec((bm, d), lambda i, j: (i, 0)),
            pl.BlockSpec((1, d), lambda i, j: (0, 0)),
            pl.BlockSpec((d, bf), lambda i, j: (0, jnp.minimum(j, nf - 1))),
            pl.BlockSpec((d, bf), lambda i, j: (0, jnp.minimum(j, nf - 1) + nf)),
            pl.BlockSpec((bf, d), lambda i, j: (jnp.maximum(j - 1, 0), 0)),
        ],
        out_specs=pl.BlockSpec((bm, d), lambda i, j: (i, 0)),
        out_shape=jax.ShapeDtypeStruct((t, d), F32),
        scratch_shapes=[pltpu.VMEM((bm, d), BF16), pltpu.VMEM((bm, bf), BF16), pltpu.VMEM((bm, bf), BF16)],
        compiler_params=_params("parallel", "arbitrary"),
        name="ffn_shifted",
    )(x, gain.reshape(1, d), w_gu, w_gu, w_down)


NORM_ROWS = 256
IN_TILE = 1024


def _norm_matmul_kernel(x_ref, gain_ref, w_ref, o_ref, xn_ref):
    @pl.when(pl.program_id(1) == 0)
    def _():
        for r in range(0, x_ref.shape[0], NORM_ROWS):
            rows = slice(r, min(r + NORM_ROWS, x_ref.shape[0]))
            xn_ref[rows, :] = _rms(x_ref[rows, :], gain_ref[...]).astype(BF16)

    o_ref[...] = jnp.dot(xn_ref[...], w_ref[...], preferred_element_type=F32)


def _norm_matmul(x, gain, w, *, bm=1024, bn=1024, name):
    t, d = x.shape
    n = w.shape[1]
    bm, bn = _tile(t, bm), _tile(n, bn)
    return pl.pallas_call(
        _norm_matmul_kernel,
        grid=(t // bm, n // bn),
        in_specs=[
            pl.BlockSpec((bm, d), lambda i, j: (i, 0), pipeline_mode=pl.Buffered(1)),
            pl.BlockSpec((1, d), lambda i, j: (0, 0)),
            pl.BlockSpec((d, bn), lambda i, j: (0, j)),
        ],
        out_specs=pl.BlockSpec((bm, bn), lambda i, j: (i, j)),
        out_shape=jax.ShapeDtypeStruct((t, n), F32),
        scratch_shapes=[pltpu.VMEM((bm, d), BF16)],
        compiler_params=_params("parallel", "arbitrary"),
        name=name,
    )(x, gain.reshape(1, d), w)


def _pool_kernel(zc_ref, zp_ref, wp_ref, sc_ref, o_ref, ext_ref, *, bm, c):
    i = pl.program_id(1)
    ext_ref[0:POOL_HALO, :] = jnp.where(i > 0, zp_ref[...], 0.0)
    ext_ref[POOL_HALO:POOL_HALO + bm, :] = zc_ref[...]
    t = i * bm + lax.broadcasted_iota(jnp.int32, (bm, 1), 0)
    for g, w in enumerate(POOL_WINDOWS):
        cols = slice(g * c, (g + 1) * c)
        x = zc_ref[:, cols]
        acc = x
        for k in range(1, w):
            acc = acc + ext_ref[POOL_HALO - k:POOL_HALO - k + bm, cols]
        cnt = jnp.minimum(t + 1, w).astype(F32)
        mixed = (acc / cnt - x).astype(BF16)
        y = jnp.dot(mixed, wp_ref[g], preferred_element_type=F32) * sc_ref[:, cols]
        o_ref[:, cols] = y.astype(BF16)


def _pool(z_main, w_pool, pool_scale, batch, seq, *, bm=512):
    g, c, _ = w_pool.shape
    pw = g * c
    bm = _tile(seq, bm)
    nb = seq // bm
    halo_blocks = bm // POOL_HALO
    return pl.pallas_call(
        functools.partial(_pool_kernel, bm=bm, c=c),
        grid=(batch, nb),
        in_specs=[
            pl.BlockSpec((bm, pw), lambda b, i: (b * nb + i, 0)),
            pl.BlockSpec((POOL_HALO, pw),
                         lambda b, i: (jnp.maximum((b * nb + i) * halo_blocks - 1, 0), 0)),
            pl.BlockSpec((g, c, c), lambda b, i: (0, 0, 0)),
            pl.BlockSpec((1, pw), lambda b, i: (0, 0)),
        ],
        out_specs=pl.BlockSpec((bm, pw), lambda b, i: (b * nb + i, 0)),
        out_shape=jax.ShapeDtypeStruct((batch * seq, pw), BF16),
        scratch_shapes=[pltpu.VMEM((POOL_HALO + bm, pw), F32)],
        compiler_params=_params("parallel", "parallel"),
        name="pool",
    )(z_main, z_main, w_pool, pool_scale.reshape(1, pw))


def _rope_tables(pos, inv):
    ang = pos.astype(F32) * inv
    cos, sin = jnp.cos(ang), jnp.sin(ang)
    lane = lax.broadcasted_iota(jnp.int32, ang.shape, 1)
    half = QK_ROPE // 2
    c = jnp.where(lane < QK_ROPE, cos, 0.0)
    sa = jnp.where(lane < half, -sin, 0.0)
    sb = jnp.where((lane >= half) & (lane < QK_ROPE), sin, 0.0)
    return c, sa, sb


def _rope_norm(r, gain, c, sa, sb):
    ms = jnp.sum(r * r, axis=-1, keepdims=True) * (1.0 / QK_ROPE)
    r = r * lax.rsqrt(ms + EPS) * gain
    half = QK_ROPE // 2
    return r * c + pltpu.roll(r, LANES - half, 1) * sa + pltpu.roll(r, half, 1) * sb


def _qkv_kernel(zq_ref, zkv_ref, zkr_ref, pos_ref, post_ref, inv_ref, invt_ref, gql_ref, gkvl_ref, gqt_ref,
                gkn_ref, gkr_ref, wuqt_ref, wuk_ref, wuvt_ref, qt_ref, k_ref, vt_ref,
                cqt_ref, ckv_ref, ckvt_ref, kr_ref, cost_ref, sint_ref):
    half = QK_ROPE // 2
    r1, r2, r3 = QK_NOPE, QK_NOPE + half, QK_HEAD

    @pl.when(pl.program_id(1) == 0)
    def _():
        cqt_ref[...] = _rms(zq_ref[...], gql_ref[...]).T.astype(BF16)
        ckv = _rms(zkv_ref[...], gkvl_ref[...])
        ckv_ref[...] = ckv.astype(BF16)
        ckvt_ref[...] = ckv.T.astype(BF16)
        c, sa, sb = _rope_tables(pos_ref[...], inv_ref[...])
        kr_ref[...] = _rope_norm(zkr_ref[...], gkr_ref[...], c, sa, sb).astype(BF16)
        ang = post_ref[...].astype(F32) * invt_ref[...]
        cost_ref[...] = jnp.cos(ang)
        sint_ref[...] = jnp.sin(ang)

    q = jnp.dot(wuqt_ref[...], cqt_ref[...], preferred_element_type=F32)
    nope = q[:r1]
    rs = lax.rsqrt(jnp.mean(nope * nope, axis=0, keepdims=True) + EPS)
    qt_ref[:r1, :] = (nope * rs * gqt_ref[:r1, :]).astype(BF16)
    x1, x2 = q[r1:r2], q[r2:r3]
    ms = (jnp.sum(x1 * x1, axis=0, keepdims=True) + jnp.sum(x2 * x2, axis=0, keepdims=True)) * (1.0 / QK_ROPE)
    rs = lax.rsqrt(ms + EPS)
    y1, y2 = x1 * rs * gqt_ref[r1:r2, :], x2 * rs * gqt_ref[r2:r3, :]
    cos, sin = cost_ref[...], sint_ref[...]
    qt_ref[r1:r2, :] = (y1 * cos - y2 * sin).astype(BF16)
    qt_ref[r2:r3, :] = (y2 * cos + y1 * sin).astype(BF16)
    qt_ref[r3:, :] = jnp.zeros((QK_PAD - r3, qt_ref.shape[1]), BF16)

    k = jnp.dot(ckv_ref[...], wuk_ref[...], preferred_element_type=F32)
    k_ref[:, :QK_NOPE] = _rms(k, gkn_ref[...]).astype(BF16)
    k_ref[:, QK_NOPE:] = kr_ref[...]
    vt = jnp.dot(wuvt_ref[...], ckvt_ref[...], preferred_element_type=F32)
    vt_ref[:V_HEAD, :] = vt.astype(BF16)
    vt_ref[V_HEAD:, :] = jnp.ones((V_ROWS - V_HEAD, vt_ref.shape[1]), BF16)


def _qkv(z, pos, inv, gql, gkvl, gqt, gkn, gkr, wuqt, wuk, wuvt, batch, seq, off_q, off_kv, *, bm=512):
    h, _, ql = wuqt.shape
    kvl = wuk.shape[1]
    bm = _tile(seq, bm)
    nb = seq // bm
    t = batch * seq
    half = QK_ROPE // 2
    assert off_q % ql == 0 and off_kv % kvl == 0 and (off_kv + kvl) % LANES == 0
    const = lambda i, hh: (0, 0)
    head = lambda i, hh: (hh, 0, 0)
    out_map = lambda i, hh: (i // nb, hh, i % nb, 0)
    out_map_t = lambda i, hh: (i // nb, hh, i % nb, 0, 0)
    invt = jnp.broadcast_to(inv[0, :half].reshape(half, 1), (half, bm))
    gqt = jnp.broadcast_to(gqt.reshape(QK_PAD, 1), (QK_PAD, bm))
    return pl.pallas_call(
        _qkv_kernel,
        grid=(t // bm, h),
        in_specs=[
            pl.BlockSpec((bm, ql), lambda i, hh: (i, off_q // ql)),
            pl.BlockSpec((bm, kvl), lambda i, hh: (i, off_kv // kvl)),
            pl.BlockSpec((bm, LANES), lambda i, hh: (i, (off_kv + kvl) // LANES)),
            pl.BlockSpec((bm, 1), lambda i, hh: (i, 0)),
            pl.BlockSpec((1, bm), lambda i, hh: (0, i)),
            pl.BlockSpec((1, LANES), const),
            pl.BlockSpec((half, bm), const),
            pl.BlockSpec((1, ql), const),
            pl.BlockSpec((1, kvl), const),
            pl.BlockSpec((QK_PAD, bm), const),
            pl.BlockSpec((1, QK_NOPE), const),
            pl.BlockSpec((1, LANES), const),
            pl.BlockSpec((None, QK_PAD, ql), head),
            pl.BlockSpec((None, kvl, QK_NOPE), head),
            pl.BlockSpec((None, V_HEAD, kvl), head),
        ],
        out_specs=[
            pl.BlockSpec((None, None, None, QK_PAD, bm), out_map_t),
            pl.BlockSpec((None, None, bm, QK_PAD), out_map),
            pl.BlockSpec((None, None, None, V_ROWS, bm), out_map_t),
        ],
        out_shape=[
            jax.ShapeDtypeStruct((batch, h, nb, QK_PAD, bm), BF16),
            jax.ShapeDtypeStruct((batch, h, seq, QK_PAD), BF16),
            jax.ShapeDtypeStruct((batch, h, nb, V_ROWS, bm), BF16),
        ],
        scratch_shapes=[
            pltpu.VMEM((ql, bm), BF16),
            pltpu.VMEM((bm, kvl), BF16),
            pltpu.VMEM((kvl, bm), BF16),
            pltpu.VMEM((bm, LANES), BF16),
            pltpu.VMEM((half, bm), F32),
            pltpu.VMEM((half, bm), F32),
        ],
        compiler_params=_params("parallel", "arbitrary"),
        name="qkv_prep",
    )(z, z, z, pos.reshape(t, 1), pos.reshape(1, t), inv, invt, gql, gkvl, gqt, gkn, gkr, wuqt, wuk, wuvt)


def _flash_kernel(qt_ref, k_ref, vt_ref, o_ref, sa_ref, sb_ref, m_ref, acc_ref, *, blk, chunk):
    qi = pl.program_id(2)
    nch = blk // chunk
    m_ref[...] = jnp.full(m_ref.shape, -jnp.inf, F32)
    acc_ref[...] = jnp.zeros(acc_ref.shape, F32)

    def scores(kj, s_ref):
        start = pl.multiple_of(kj * blk, blk)
        k = k_ref[pl.ds(start, blk), :]
        for c in range(nch):
            s_ref[c] = jnp.dot(k, qt_ref[:, c * chunk:(c + 1) * chunk], preferred_element_type=F32)

    def update(kj, s_ref, masked):
        vt = vt_ref[kj]
        for c in range(nch):
            cols = slice(c * chunk, (c + 1) * chunk)
            st = s_ref[c]
            if masked:
                key = lax.broadcasted_iota(jnp.int32, st.shape, 0)
                qry = lax.broadcasted_iota(jnp.int32, st.shape, 1) + c * chunk
                st = jnp.where(key <= qry, st, -jnp.inf)
            m_old = m_ref[:, cols]
            m_new = jnp.maximum(m_old, jnp.max(st, axis=0, keepdims=True))
            p = jnp.exp2(st - m_new).astype(BF16)
            alpha = jnp.exp2(m_old - m_new)
            acc_ref[:, cols] = alpha * acc_ref[:, cols] + jnp.dot(vt, p, preferred_element_type=F32)
            m_ref[:, cols] = m_new

    scores(0, sa_ref)

    def pair(t, carry):
        scores(2 * t + 1, sb_ref)
        update(2 * t, sa_ref, False)
        scores(2 * t + 2, sa_ref)
        update(2 * t + 1, sb_ref, False)
        return carry

    lax.fori_loop(0, qi // 2, pair, 0)

    @pl.when(qi % 2 == 0)
    def _():
        update(qi, sa_ref, True)

    @pl.when(qi % 2 == 1)
    def _():
        scores(qi, sb_ref)
        update(qi - 1, sa_ref, False)
        update(qi, sb_ref, True)

    o_ref[...] = (acc_ref[:V_HEAD, :] / acc_ref[V_HEAD:V_HEAD + 1, :]).T.astype(o_ref.dtype)


def _flash(qt, k, vt, *, chunk=256):
    b, h, nb, _, blk = qt.shape
    s = nb * blk
    chunk = min(chunk, blk)
    return pl.pallas_call(
        functools.partial(_flash_kernel, blk=blk, chunk=chunk),
        grid=(b, h, nb),
        in_specs=[
            pl.BlockSpec((None, None, None, QK_PAD, blk), lambda bi, hi, qi: (bi, hi, qi, 0, 0)),
            pl.BlockSpec((None, None, s, QK_PAD), lambda bi, hi, qi: (bi, hi, 0, 0)),
            pl.BlockSpec((None, None, nb, V_ROWS, blk), lambda bi, hi, qi: (bi, hi, 0, 0, 0)),
        ],
        out_specs=pl.BlockSpec((None, blk, V_HEAD), lambda bi, hi, qi: (bi, qi, hi)),
        out_shape=jax.ShapeDtypeStruct((b, s, h * V_HEAD), BF16),
        scratch_shapes=[
            pltpu.VMEM((blk // chunk, blk, chunk), F32),
            pltpu.VMEM((blk // chunk, blk, chunk), F32),
            pltpu.VMEM((1, blk), F32),
            pltpu.VMEM((V_ROWS, blk), F32),
        ],
        compiler_params=_params("parallel", "parallel", "arbitrary"),
        name="mla_flash",
    )(qt, k, vt)


def _merge_kernel(a_ref, b_ref, wa_ref, wb_ref, gp_ref, gm_ref, o_ref):
    pa = jnp.dot(a_ref[...], wa_ref[...], preferred_element_type=F32)
    pb = jnp.dot(b_ref[...], wb_ref[...], preferred_element_type=F32)
    o_ref[...] = (jax.nn.sigmoid(gp_ref[...]) * pa + jax.nn.sigmoid(gm_ref[...]) * pb).astype(BF16)


def _merge(a, b, wa, wb, z_main, gate_off, *, bm=1024, bn=512):
    t, pw = a.shape
    mw = b.shape[1]
    d = wa.shape[1]
    bm, bn = _tile(t, bm), _tile(math.gcd(d, gate_off), bn)
    gp0, gm0 = gate_off // bn, (gate_off + d) // bn
    return pl.pallas_call(
        _merge_kernel,
        grid=(t // bm, d // bn),
        in_specs=[
            pl.BlockSpec((bm, pw), lambda i, j: (i, 0)),
            pl.BlockSpec((bm, mw), lambda i, j: (i, 0)),
            pl.BlockSpec((pw, bn), lambda i, j: (0, j)),
            pl.BlockSpec((mw, bn), lambda i, j: (0, j)),
            pl.BlockSpec((bm, bn), lambda i, j: (i, gp0 + j)),
            pl.BlockSpec((bm, bn), lambda i, j: (i, gm0 + j)),
        ],
        out_specs=pl.BlockSpec((bm, bn), lambda i, j: (i, j)),
        out_shape=jax.ShapeDtypeStruct((t, d), BF16),
        compiler_params=_params("parallel", "arbitrary"),
        name="branch_merge",
    )(a, b, wa, wb, z_main, z_main)


def _proj_residual_kernel(a_ref, w_ref, h_ref, o_ref):
    o_ref[...] = h_ref[...] + jnp.dot(a_ref[...], w_ref[...], preferred_element_type=F32)


def _proj_residual(a, w, h, *, bm=1024, bn=1024):
    t, kdim = a.shape
    d = w.shape[1]
    bm, bn = _tile(t, bm), _tile(d, bn)
    return pl.pallas_call(
        _proj_residual_kernel,
        grid=(t // bm, d // bn),
        in_specs=[
            pl.BlockSpec((bm, kdim), lambda i, j: (i, 0)),
            pl.BlockSpec((kdim, bn), lambda i, j: (0, j)),
            pl.BlockSpec((bm, bn), lambda i, j: (i, j)),
        ],
        out_specs=pl.BlockSpec((bm, bn), lambda i, j: (i, j)),
        out_shape=jax.ShapeDtypeStruct((t, d), F32),
        compiler_params=_params("parallel", "arbitrary"),
        name="out_proj",
    )(a, w, h)


def _mem_kv_kernel(mem_ref, gain_ref, w_ref, gk_ref, k_ref, v_ref):
    mn = _rms(mem_ref[...], gain_ref[...]).astype(BF16)
    kv = jnp.dot(mn, w_ref[...], preferred_element_type=F32)
    xw = X_HEADS * X_HEAD_DIM
    for hh in range(X_HEADS):
        cols = slice(hh * X_HEAD_DIM, (hh + 1) * X_HEAD_DIM)
        k_ref[:, cols] = _rms(kv[:, cols], gk_ref[...]).astype(BF16)
    v_ref[...] = kv[:, xw:].astype(BF16)


def _mem_kv(mem, gain, w_xkv, gk):
    b, m, d = mem.shape
    xw = X_HEADS * X_HEAD_DIM
    return pl.pallas_call(
        _mem_kv_kernel,
        grid=(b,),
        in_specs=[
            pl.BlockSpec((None, m, d), lambda bi: (bi, 0, 0)),
            pl.BlockSpec((1, d), lambda bi: (0, 0)),
            pl.BlockSpec((d, 2 * xw), lambda bi: (0, 0)),
            pl.BlockSpec((1, X_HEAD_DIM), lambda bi: (0, 0)),
        ],
        out_specs=[
            pl.BlockSpec((None, m, xw), lambda bi: (bi, 0, 0)),
            pl.BlockSpec((None, m, xw), lambda bi: (bi, 0, 0)),
        ],
        out_shape=[
            jax.ShapeDtypeStruct((b, m, xw), BF16),
            jax.ShapeDtypeStruct((b, m, xw), BF16),
        ],
        compiler_params=_params("parallel"),
        name="mem_kv",
    )(mem, gain.reshape(1, d), w_xkv, gk)


def _cross_kernel(h_ref, gain_ref, wq_ref, gq_ref, k_ref, v_ref, wo_ref, o_ref, xo_ref):
    h = h_ref[...]
    uq = _rms(h, gain_ref[...]).astype(BF16)
    xq = jnp.dot(uq, wq_ref[...], preferred_element_type=F32)
    scale = X_HEAD_DIM ** -0.5
    for hh in range(X_HEADS):
        cols = slice(hh * X_HEAD_DIM, (hh + 1) * X_HEAD_DIM)
        qh = _rms(xq[:, cols], gq_ref[...]).astype(BF16)
        s = lax.dot_general(qh, k_ref[:, cols], (((1,), (1,)), ((), ())),
                            preferred_element_type=F32) * scale
        p = jnp.exp(s - jnp.max(s, axis=-1, keepdims=True))
        p = (p / jnp.sum(p, axis=-1, keepdims=True)).astype(BF16)
        xo_ref[:, cols] = jnp.dot(p, v_ref[:, cols], preferred_element_type=F32).astype(BF16)
    o_ref[...] = h + jnp.dot(xo_ref[...], wo_ref[...], preferred_element_type=F32)


def _cross(h, gain, wq, gq, xk, xv, wo, seq, *, bm=256):
    t, d = h.shape
    m = xk.shape[1]
    xw = X_HEADS * X_HEAD_DIM
    bm = _tile(seq, bm)
    nb = seq // bm
    return pl.pallas_call(
        _cross_kernel,
        grid=(t // bm,),
        in_specs=[
            pl.BlockSpec((bm, d), lambda i: (i, 0)),
            pl.BlockSpec((1, d), lambda i: (0, 0)),
            pl.BlockSpec((d, xw), lambda i: (0, 0)),
            pl.BlockSpec((1, X_HEAD_DIM), lambda i: (0, 0)),
            pl.BlockSpec((None, m, xw), lambda i: (i // nb, 0, 0)),
            pl.BlockSpec((None, m, xw), lambda i: (i // nb, 0, 0)),
            pl.BlockSpec((xw, d), lambda i: (0, 0)),
        ],
        out_specs=pl.BlockSpec((bm, d), lambda i: (i, 0)),
        out_shape=jax.ShapeDtypeStruct((t, d), F32),
        scratch_shapes=[pltpu.VMEM((bm, xw), BF16)],
        compiler_params=_params("parallel"),
        name="cross_attn",
    )(h, gain.reshape(1, d), wq, gq, xk, xv, wo)


def _pad_lanes(v, width):
    return jnp.pad(v, (0, width - v.shape[0])).reshape(1, width)


def kernel(x, mem, positions, ffn1_norm, ffn1_w_gu, ffn1_w_down, mix_norm, w_in, w_pool, pool_scale,
           q_latent_norm, kv_latent_norm, w_uq, w_ukv, q_nope_norm, k_nope_norm, q_rope_norm,
           k_rope_norm, w_branch_pool, w_branch_mla, w_out, x_norm, mem_norm, w_xq, w_xkv, xq_norm,
           xk_norm, w_xo, ffn2_norm, ffn2_w_gu, ffn2_w_down):
    batch, seq, d = x.shape
    t = batch * seq
    depth = w_in.shape[0]
    pw = w_branch_pool.shape[1]
    ql = q_latent_norm.shape[1]
    kvl = kv_latent_norm.shape[1]
    heads = w_uq.shape[2] // QK_HEAD

    half = QK_ROPE // 2
    inv = 1.0 / (ROPE_THETA ** (jnp.arange(half, dtype=F32) * (2.0 / QK_ROPE)))
    inv = _pad_lanes(jnp.concatenate([inv, inv]), LANES)
    c2 = QK_HEAD ** -0.5 * math.log2(math.e)

    h = x.reshape(t, d)
    for l in range(depth):
        h = _ffn(h, ffn1_norm[l], ffn1_w_gu[l].astype(BF16), ffn1_w_down[l].astype(BF16))

        o_kv, o_gp = pw + ql, pw + ql + kvl + QK_ROPE
        off_kv = o_kv + 2 * d
        wi = w_in[l].astype(BF16)
        n_in = -(-wi.shape[1] // IN_TILE) * IN_TILE
        w_all = jnp.concatenate([wi[:, :o_kv], wi[:, o_gp:], wi[:, o_kv:o_gp],
                                 jnp.zeros((d, n_in - wi.shape[1]), BF16)], axis=1)
        z = _norm_matmul(h, mix_norm[l], w_all, bn=IN_TILE, name="in_proj")

        a_out = _pool(z, w_pool[l].astype(BF16), pool_scale[l], batch, seq)

        wuq = jnp.pad(w_uq[l].reshape(ql, heads, QK_HEAD), ((0, 0), (0, 0), (0, QK_PAD - QK_HEAD)))
        wuqt = wuq.transpose(1, 2, 0).astype(BF16)
        wukv = w_ukv[l].reshape(kvl, heads, QK_NOPE + V_HEAD)
        wuk = wukv[:, :, :QK_NOPE].transpose(1, 0, 2).astype(BF16)
        wuvt = wukv[:, :, QK_NOPE:].transpose(1, 2, 0).astype(BF16)
        gqt = jnp.concatenate([q_nope_norm[l], q_rope_norm[l], jnp.zeros((QK_PAD - QK_HEAD,), F32)]) * c2
        qt, k, vt = _qkv(z, positions, inv,
                         q_latent_norm[l].reshape(1, ql), kv_latent_norm[l].reshape(1, kvl), gqt,
                         k_nope_norm[l].reshape(1, QK_NOPE), _pad_lanes(k_rope_norm[l], LANES),
                         wuqt, wuk, wuvt, batch, seq, pw, off_kv)
        b_out = _flash(qt, k, vt).reshape(t, heads * V_HEAD)

        merged = _merge(a_out, b_out, w_branch_pool[l].astype(BF16), w_branch_mla[l].astype(BF16),
                        z, pw + ql)
        h = _proj_residual(merged, w_out[l].astype(BF16), h)

        xk, xv = _mem_kv(mem, mem_norm[l], w_xkv[l].astype(BF16), xk_norm[l].reshape(1, X_HEAD_DIM))
        h = _cross(h, x_norm[l], w_xq[l].astype(BF16), xq_norm[l].reshape(1, X_HEAD_DIM), xk, xv,
                   w_xo[l].astype(BF16), seq)

        h = _ffn_shifted(h, ffn2_norm[l], ffn2_w_gu[l].astype(BF16), ffn2_w_down[l].astype(BF16))
    return h.reshape(batch, seq, d)
```

```python
import functools
import math

import jax
import jax.numpy as jnp
from jax import lax
from jax.experimental import pallas as pl
from jax.experimental.pallas import tpu as pltpu

F32 = jnp.float32
BF16 = jnp.bfloat16

POOL_WINDOWS = (2, 4, 8, 16)
POOL_HALO = 16
QK_NOPE = 128
QK_ROPE = 64
QK_HEAD = QK_NOPE + QK_ROPE
QK_PAD = 256
V_HEAD = 128
V_ROWS = V_HEAD + 16
X_HEADS = 4
X_HEAD_DIM = 128
ROPE_THETA = 10000.0
EPS = 1e-6
LANES = 128
VMEM_LIMIT = 62 * 1024 * 1024
NORM_ROWS = 256
IN_TILE = 1024


def _params(*sem):
    return pltpu.CompilerParams(dimension_semantics=sem, vmem_limit_bytes=VMEM_LIMIT)


def _tile(n, pref):
    if n <= pref:
        return n
    t = (pref // LANES) * LANES
    while t >= LANES:
        if n % t == 0:
            return t
        t -= LANES
    raise ValueError(f"no lane-aligned tile of {n} below {pref}")


def _rms(x, gain):
    ms = jnp.mean(x * x, axis=-1, keepdims=True)
    return x * lax.rsqrt(ms + EPS) * gain


def _ffn_kernel(x_ref, gain_ref, wg_ref, wu_ref, wd_ref, o_ref, xn_ref):
    @pl.when(pl.program_id(1) == 0)
    def _():
        for r in range(0, x_ref.shape[0], NORM_ROWS):
            rows = slice(r, min(r + NORM_ROWS, x_ref.shape[0]))
            x = x_ref[rows, :]
            xn_ref[rows, :] = _rms(x, gain_ref[...]).astype(BF16)
            o_ref[rows, :] = x

    xn = xn_ref[...]
    g = jnp.dot(xn, wg_ref[...], preferred_element_type=F32)
    u = jnp.dot(xn, wu_ref[...], preferred_element_type=F32)
    hid = (g * jax.nn.sigmoid(g) * u * 0.5).astype(BF16)
    o_ref[...] += jnp.dot(hid, wd_ref[...], preferred_element_type=F32)


def _ffn(x, gain, w_gu, w_down, *, bm=512, bf, x_buffers, name):
    t, d = x.shape
    f = w_down.shape[0]
    bm = _tile(t, bm)
    nf = f // bf
    return pl.pallas_call(
        _ffn_kernel,
        grid=(t // bm, nf),
        in_specs=[
            pl.BlockSpec((bm, d), lambda i, j: (i, 0), pipeline_mode=pl.Buffered(x_buffers)),
            pl.BlockSpec((1, d), lambda i, j: (0, 0)),
            pl.BlockSpec((d, bf), lambda i, j: (0, j)),
            pl.BlockSpec((d, bf), lambda i, j: (0, j + nf)),
            pl.BlockSpec((bf, d), lambda i, j: (j, 0)),
        ],
        out_specs=pl.BlockSpec((bm, d), lambda i, j: (i, 0)),
        out_shape=jax.ShapeDtypeStruct((t, d), F32),
        scratch_shapes=[pltpu.VMEM((bm, d), BF16)],
        compiler_params=_params("parallel", "arbitrary"),
        name=name,
    )(x, gain.reshape(1, d), w_gu, w_gu, w_down)


def _pad_ffn_weights(w_gu, w_down, tile):
    d, f2 = w_gu.shape
    f = f2 // 2
    pad = -f % tile
    w_gu = jnp.pad(w_gu.reshape(d, 2, f), ((0, 0), (0, 0), (0, pad))).reshape(d, 2 * (f + pad))
    w_down = jnp.pad(w_down, ((0, pad), (0, 0)))
    return w_gu.astype(BF16), w_down.astype(BF16)


def _norm_matmul_kernel(x_ref, gain_ref, w_ref, o_ref, xn_ref):
    @pl.when(pl.program_id(1) == 0)
    def _():
        for r in range(0, x_ref.shape[0], NORM_ROWS):
            rows = slice(r, min(r + NORM_ROWS, x_ref.shape[0]))
            xn_ref[rows, :] = _rms(x_ref[rows, :], gain_ref[...]).astype(BF16)

    o_ref[...] = jnp.dot(xn_ref[...], w_ref[...], preferred_element_type=F32)


def _norm_matmul(x, gain, w, *, bm=1024, bn=1024, name):
    t, d = x.shape
    n = w.shape[1]
    bm, bn = _tile(t, bm), _tile(n, bn)
    return pl.pallas_call(
        _norm_matmul_kernel,
        grid=(t // bm, n // bn),
        in_specs=[
            pl.BlockSpec((bm, d), lambda i, j: (i, 0), pipeline_mode=pl.Buffered(1)),
            pl.BlockSpec((1, d), lambda i, j: (0, 0)),
            pl.BlockSpec((d, bn), lambda i, j: (0, j)),
        ],
        out_specs=pl.BlockSpec((bm, bn), lambda i, j: (i, j)),
        out_shape=jax.ShapeDtypeStruct((t, n), F32),
        scratch_shapes=[pltpu.VMEM((bm, d), BF16)],
        compiler_params=_params("parallel", "arbitrary"),
        name=name,
    )(x, gain.reshape(1, d), w)


def _pool_kernel(zc_ref, zp_ref, wp_ref, sc_ref, o_ref, ext_ref, *, bm, c):
    i = pl.program_id(1)
    ext_ref[0:POOL_HALO, :] = jnp.where(i > 0, zp_ref[...], 0.0)
    ext_ref[POOL_HALO:POOL_HALO + bm, :] = zc_ref[...]
    t = i * bm + lax.broadcasted_iota(jnp.int32, (bm, 1), 0)
    for g, w in enumerate(POOL_WINDOWS):
        cols = slice(g * c, (g + 1) * c)
        x = zc_ref[:, cols]
        acc = x
        for k in range(1, w):
            acc = acc + ext_ref[POOL_HALO - k:POOL_HALO - k + bm, cols]
        cnt = jnp.minimum(t + 1, w).astype(F32)
        mixed = (acc / cnt - x).astype(BF16)
        y = jnp.dot(mixed, wp_ref[g], preferred_element_type=F32) * sc_ref[:, cols]
        o_ref[:, cols] = y.astype(BF16)


def _pool(z_main, w_pool, pool_scale, batch, seq, *, bm=512):
    g, c, _ = w_pool.shape
    pw = g * c
    bm = _tile(seq, bm)
    nb = seq // bm
    halo_blocks = bm // POOL_HALO
    return pl.pallas_call(
        functools.partial(_pool_kernel, bm=bm, c=c),
        grid=(batch, nb),
        in_specs=[
            pl.BlockSpec((bm, pw), lambda b, i: (b * nb + i, 0)),
            pl.BlockSpec((POOL_HALO, pw),
                         lambda b, i: (jnp.maximum((b * nb + i) * halo_blocks - 1, 0), 0)),
            pl.BlockSpec((g, c, c), lambda b, i: (0, 0, 0)),
            pl.BlockSpec((1, pw), lambda b, i: (0, 0)),
        ],
        out_specs=pl.BlockSpec((bm, pw), lambda b, i: (b * nb + i, 0)),
        out_shape=jax.ShapeDtypeStruct((batch * seq, pw), BF16),
        scratch_shapes=[pltpu.VMEM((POOL_HALO + bm, pw), F32)],
        compiler_params=_params("parallel", "parallel"),
        name="pool",
    )(z_main, z_main, w_pool, pool_scale.reshape(1, pw))


def _rope_tables(pos, inv):
    ang = pos.astype(F32) * inv
    cos, sin = jnp.cos(ang), jnp.sin(ang)
    lane = lax.broadcasted_iota(jnp.int32, ang.shape, 1)
    half = QK_ROPE // 2
    c = jnp.where(lane < QK_ROPE, cos, 0.0)
    sa = jnp.where(lane < half, -sin, 0.0)
    sb = jnp.where((lane >= half) & (lane < QK_ROPE), sin, 0.0)
    return c, sa, sb


def _rope_norm(r, gain, c, sa, sb):
    ms = jnp.sum(r * r, axis=-1, keepdims=True) * (1.0 / QK_ROPE)
    r = r * lax.rsqrt(ms + EPS) * gain
    half = QK_ROPE // 2
    return r * c + pltpu.roll(r, LANES - half, 1) * sa + pltpu.roll(r, half, 1) * sb


def _qkv_kernel(zq_ref, zkv_ref, zkr_ref, pos_ref, post_ref, inv_ref, invt_ref, gql_ref, gkvl_ref, gqt_ref,
                gkn_ref, gkr_ref, wuqt_ref, wuk_ref, wuvt_ref, qt_ref, k_ref, vt_ref,
                cqt_ref, ckv_ref, ckvt_ref, kr_ref, cost_ref, sint_ref):
    half = QK_ROPE // 2
    r1, r2, r3 = QK_NOPE, QK_NOPE + half, QK_HEAD

    @pl.when(pl.program_id(1) == 0)
    def _():
        cqt_ref[...] = _rms(zq_ref[...], gql_ref[...]).T.astype(BF16)
        ckv = _rms(zkv_ref[...], gkvl_ref[...])
        ckv_ref[...] = ckv.astype(BF16)
        ckvt_ref[...] = ckv.T.astype(BF16)
        c, sa, sb = _rope_tables(pos_ref[...], inv_ref[...])
        lane = lax.broadcasted_iota(jnp.int32, zkr_ref.shape, 1)
        zkr = jnp.where(lane < QK_ROPE, zkr_ref[...], 0.0)
        kr_ref[...] = _rope_norm(zkr, gkr_ref[...], c, sa, sb).astype(BF16)
        ang = post_ref[...].astype(F32) * invt_ref[...]
        cost_ref[...] = jnp.cos(ang)
        sint_ref[...] = jnp.sin(ang)

    q = jnp.dot(wuqt_ref[...], cqt_ref[...], preferred_element_type=F32)
    nope = q[:r1]
    rs = lax.rsqrt(jnp.mean(nope * nope, axis=0, keepdims=True) + EPS)
    qt_ref[:r1, :] = (nope * rs * gqt_ref[:r1, :]).astype(BF16)
    x1, x2 = q[r1:r2], q[r2:r3]
    ms = (jnp.sum(x1 * x1, axis=0, keepdims=True) + jnp.sum(x2 * x2, axis=0, keepdims=True)) * (1.0 / QK_ROPE)
    rs = lax.rsqrt(ms + EPS)
    y1, y2 = x1 * rs * gqt_ref[r1:r2, :], x2 * rs * gqt_ref[r2:r3, :]
    cos, sin = cost_ref[...], sint_ref[...]
    qt_ref[r1:r2, :] = (y1 * cos - y2 * sin).astype(BF16)
    qt_ref[r2:r3, :] = (y2 * cos + y1 * sin).astype(BF16)
    qt_ref[r3:, :] = jnp.zeros((QK_PAD - r3, qt_ref.shape[1]), BF16)

    k = jnp.dot(ckv_ref[...], wuk_ref[...], preferred_element_type=F32)
    k_ref[:, :QK_NOPE] = _rms(k, gkn_ref[...]).astype(BF16)
    k_ref[:, QK_NOPE:] = kr_ref[...]
    vt = jnp.dot(wuvt_ref[...], ckvt_ref[...], preferred_element_type=F32)
    vt_ref[:V_HEAD, :] = vt.astype(BF16)
    vt_ref[V_HEAD:, :] = jnp.ones((V_ROWS - V_HEAD, vt_ref.shape[1]), BF16)


def _qkv(z, pos, inv, gql, gkvl, gqt, gkn, gkr, wuqt, wuk, wuvt, batch, seq, off_q, off_kv, *, bm=512):
    h, _, ql = wuqt.shape
    kvl = wuk.shape[1]
    bm = _tile(seq, bm)
    nb = seq // bm
    t = batch * seq
    half = QK_ROPE // 2
    assert off_q % ql == 0 and off_kv % kvl == 0 and (off_kv + kvl) % LANES == 0
    const = lambda i, hh: (0, 0)
    head = lambda i, hh: (hh, 0, 0)
    out_map = lambda i, hh: (i // nb, hh, i % nb, 0)
    out_map_t = lambda i, hh: (i // nb, hh, i % nb, 0, 0)
    invt = jnp.broadcast_to(inv[0, :half].reshape(half, 1), (half, bm))
    gqt = jnp.broadcast_to(gqt.reshape(QK_PAD, 1), (QK_PAD, bm))
    return pl.pallas_call(
        _qkv_kernel,
        grid=(t // bm, h),
        in_specs=[
            pl.BlockSpec((bm, ql), lambda i, hh: (i, off_q // ql)),
            pl.BlockSpec((bm, kvl), lambda i, hh: (i, off_kv // kvl)),
            pl.BlockSpec((bm, LANES), lambda i, hh: (i, (off_kv + kvl) // LANES)),
            pl.BlockSpec((bm, 1), lambda i, hh: (i, 0)),
            pl.BlockSpec((1, bm), lambda i, hh: (0, i)),
            pl.BlockSpec((1, LANES), const),
            pl.BlockSpec((half, bm), const),
            pl.BlockSpec((1, ql), const),
            pl.BlockSpec((1, kvl), const),
            pl.BlockSpec((QK_PAD, bm), const),
            pl.BlockSpec((1, QK_NOPE), const),
            pl.BlockSpec((1, LANES), const),
            pl.BlockSpec((None, QK_PAD, ql), head),
            pl.BlockSpec((None, kvl, QK_NOPE), head),
            pl.BlockSpec((None, V_HEAD, kvl), head),
        ],
        out_specs=[
            pl.BlockSpec((None, None, None, QK_PAD, bm), out_map_t),
            pl.BlockSpec((None, None, bm, QK_PAD), out_map),
            pl.BlockSpec((None, None, None, V_ROWS, bm), out_map_t),
        ],
        out_shape=[
            jax.ShapeDtypeStruct((batch, h, nb, QK_PAD, bm), BF16),
            jax.ShapeDtypeStruct((batch, h, seq, QK_PAD), BF16),
            jax.ShapeDtypeStruct((batch, h, nb, V_ROWS, bm), BF16),
        ],
        scratch_shapes=[
            pltpu.VMEM((ql, bm), BF16),
            pltpu.VMEM((bm, kvl), BF16),
            pltpu.VMEM((kvl, bm), BF16),
            pltpu.VMEM((bm, LANES), BF16),
            pltpu.VMEM((half, bm), F32),
            pltpu.VMEM((half, bm), F32),
        ],
        compiler_params=_params("parallel", "arbitrary"),
        name="qkv_prep",
    )(z, z, z, pos.reshape(t, 1), pos.reshape(1, t), inv, invt, gql, gkvl, gqt, gkn, gkr, wuqt, wuk, wuvt)


def _flash_kernel(qt_ref, k_ref, vt_ref, o_ref, sa_ref, sb_ref, m_ref, acc_ref, *, blk, chunk):
    qi = pl.program_id(2)
    nch = blk // chunk
    m_ref[...] = jnp.full(m_ref.shape, -jnp.inf, F32)
    acc_ref[...] = jnp.zeros(acc_ref.shape, F32)

    def scores(kj, s_ref):
        start = pl.multiple_of(kj * blk, blk)
        k = k_ref[pl.ds(start, blk), :]
        for c in range(nch):
            s_ref[c] = jnp.dot(k, qt_ref[:, c * chunk:(c + 1) * chunk], preferred_element_type=F32)

    def update(kj, s_ref, masked):
        vt = vt_ref[kj]
        for c in range(nch):
            cols = slice(c * chunk, (c + 1) * chunk)
            st = s_ref[c]
            if masked:
                key = lax.broadcasted_iota(jnp.int32, st.shape, 0)
                qry = lax.broadcasted_iota(jnp.int32, st.shape, 1) + c * chunk
                st = jnp.where(key <= qry, st, -jnp.inf)
            m_old = m_ref[:, cols]
            m_new = jnp.maximum(m_old, jnp.max(st, axis=0, keepdims=True))
            p = jnp.exp2(st - m_new).astype(BF16)
            alpha = jnp.exp2(m_old - m_new)
            acc_ref[:, cols] = alpha * acc_ref[:, cols] + jnp.dot(vt, p, preferred_element_type=F32)
            m_ref[:, cols] = m_new

    scores(0, sa_ref)

    def pair(t, carry):
        scores(2 * t + 1, sb_ref)
        update(2 * t, sa_ref, False)
        scores(2 * t + 2, sa_ref)
        update(2 * t + 1, sb_ref, False)
        return carry

    lax.fori_loop(0, qi // 2, pair, 0)

    @pl.when(qi % 2 == 0)
    def _():
        update(qi, sa_ref, True)

    @pl.when(qi % 2 == 1)
    def _():
        scores(qi, sb_ref)
        update(qi - 1, sa_ref, False)
        update(qi, sb_ref, True)

    o_ref[...] = (acc_ref[:V_HEAD, :] / acc_ref[V_HEAD:V_HEAD + 1, :]).T.astype(o_ref.dtype)


def _flash(qt, k, vt, *, chunk=256):
    b, h, nb, _, blk = qt.shape
    s = nb * blk
    chunk = min(chunk, blk)
    return pl.pallas_call(
        functools.partial(_flash_kernel, blk=blk, chunk=chunk),
        grid=(b, h, nb),
        in_specs=[
            pl.BlockSpec((None, None, None, QK_PAD, blk), lambda bi, hi, qi: (bi, hi, qi, 0, 0)),
            pl.BlockSpec((None, None, s, QK_PAD), lambda bi, hi, qi: (bi, hi, 0, 0)),
            pl.BlockSpec((None, None, nb, V_ROWS, blk), lambda bi, hi, qi: (bi, hi, 0, 0, 0)),
        ],
        out_specs=pl.BlockSpec((None, blk, V_HEAD), lambda bi, hi, qi: (bi, qi, hi)),
        out_shape=jax.ShapeDtypeStruct((b, s, h * V_HEAD), BF16),
        scratch_shapes=[
            pltpu.VMEM((blk // chunk, blk, chunk), F32),
            pltpu.VMEM((blk // chunk, blk, chunk), F32),
            pltpu.VMEM((1, blk), F32),
            pltpu.VMEM((V_ROWS, blk), F32),
        ],
        compiler_params=_params("parallel", "parallel", "arbitrary"),
        name="mla_flash",
    )(qt, k, vt)


def _merge_kernel(a_ref, b_ref, wa_ref, wb_ref, gp_ref, gpx_ref, gm_ref, gmx_ref, o_ref, *, shift):
    bn = o_ref.shape[1]

    def gate(lo_ref, hi_ref):
        g = jnp.concatenate([lo_ref[...], hi_ref[...]], axis=1)[:, shift:shift + bn]
        return jax.nn.sigmoid(g)

    pa = jnp.dot(a_ref[...], wa_ref[...], preferred_element_type=F32)
    pb = jnp.dot(b_ref[...], wb_ref[...], preferred_element_type=F32)
    o_ref[...] = (gate(gp_ref, gpx_ref) * pa + gate(gm_ref, gmx_ref) * pb).astype(BF16)


def _merge(a, b, wa, wb, z, gate_off, *, bm=1024, bn=512):
    t, pw = a.shape
    mw = b.shape[1]
    d = wa.shape[1]
    shift = gate_off % LANES
    base = gate_off - shift
    bm, bn = _tile(t, bm), _tile(math.gcd(d, base), bn)
    lanes_per_tile = bn // LANES
    lo = lambda off: (lambda i, j: (i, off // bn + j))
    hi = lambda off: (lambda i, j: (i, (off + bn) // LANES + j * lanes_per_tile))
    return pl.pallas_call(
        functools.partial(_merge_kernel, shift=shift),
        grid=(t // bm, d // bn),
        in_specs=[
            pl.BlockSpec((bm, pw), lambda i, j: (i, 0)),
            pl.BlockSpec((bm, mw), lambda i, j: (i, 0)),
            pl.BlockSpec((pw, bn), lambda i, j: (0, j)),
            pl.BlockSpec((mw, bn), lambda i, j: (0, j)),
            pl.BlockSpec((bm, bn), lo(base)),
            pl.BlockSpec((bm, LANES), hi(base)),
            pl.BlockSpec((bm, bn), lo(base + d)),
            pl.BlockSpec((bm, LANES), hi(base + d)),
        ],
        out_specs=pl.BlockSpec((bm, bn), lambda i, j: (i, j)),
        out_shape=jax.ShapeDtypeStruct((t, d), BF16),
        compiler_params=_params("parallel", "arbitrary"),
        name="branch_merge",
    )(a, b, wa, wb, z, z, z, z)


def _proj_residual_kernel(a_ref, w_ref, h_ref, o_ref):
    o_ref[...] = h_ref[...] + jnp.dot(a_ref[...], w_ref[...], preferred_element_type=F32)


def _proj_residual(a, w, h, *, bm=1024, bn=1024):
    t, kdim = a.shape
    d = w.shape[1]
    bm, bn = _tile(t, bm), _tile(d, bn)
    return pl.pallas_call(
        _proj_residual_kernel,
        grid=(t // bm, d // bn),
        in_specs=[
            pl.BlockSpec((bm, kdim), lambda i, j: (i, 0)),
            pl.BlockSpec((kdim, bn), lambda i, j: (0, j)),
            pl.BlockSpec((bm, bn), lambda i, j: (i, j)),
        ],
        out_specs=pl.BlockSpec((bm, bn), lambda i, j: (i, j)),
        out_shape=jax.ShapeDtypeStruct((t, d), F32),
        compiler_params=_params("parallel", "arbitrary"),
        name="out_proj",
    )(a, w, h)


def _mem_kv_kernel(mem_ref, gain_ref, w_ref, gk_ref, k_ref, v_ref):
    mn = _rms(mem_ref[...], gain_ref[...]).astype(BF16)
    kv = jnp.dot(mn, w_ref[...], preferred_element_type=F32)
    xw = X_HEADS * X_HEAD_DIM
    for hh in range(X_HEADS):
        cols = slice(hh * X_HEAD_DIM, (hh + 1) * X_HEAD_DIM)
        k_ref[:, cols] = _rms(kv[:, cols], gk_ref[...]).astype(BF16)
    v_ref[...] = kv[:, xw:].astype(BF16)


def _mem_kv(mem, gain, w_xkv, gk):
    b, m, d = mem.shape
    xw = X_HEADS * X_HEAD_DIM
    return pl.pallas_call(
        _mem_kv_kernel,
        grid=(b,),
        in_specs=[
            pl.BlockSpec((None, m, d), lambda bi: (bi, 0, 0)),
            pl.BlockSpec((1, d), lambda bi: (0, 0)),
            pl.BlockSpec((d, 2 * xw), lambda bi: (0, 0)),
            pl.BlockSpec((1, X_HEAD_DIM), lambda bi: (0, 0)),
        ],
        out_specs=[
            pl.BlockSpec((None, m, xw), lambda bi: (bi, 0, 0)),
            pl.BlockSpec((None, m, xw), lambda bi: (bi, 0, 0)),
        ],
        out_shape=[
            jax.ShapeDtypeStruct((b, m, xw), BF16),
            jax.ShapeDtypeStruct((b, m, xw), BF16),
        ],
        compiler_params=_params("parallel"),
        name="mem_kv",
    )(mem, gain.reshape(1, d), w_xkv, gk)


def _cross_kernel(h_ref, gain_ref, wq_ref, gq_ref, k_ref, v_ref, wo_ref, o_ref, xo_ref):
    h = h_ref[...]
    uq = _rms(h, gain_ref[...]).astype(BF16)
    xq = jnp.dot(uq, wq_ref[...], preferred_element_type=F32)
    scale = X_HEAD_DIM ** -0.5
    for hh in range(X_HEADS):
        cols = slice(hh * X_HEAD_DIM, (hh + 1) * X_HEAD_DIM)
        qh = _rms(xq[:, cols], gq_ref[...]).astype(BF16)
        s = lax.dot_general(qh, k_ref[:, cols], (((1,), (1,)), ((), ())),
                            preferred_element_type=F32) * scale
        p = jnp.exp(s - jnp.max(s, axis=-1, keepdims=True))
        p = (p / jnp.sum(p, axis=-1, keepdims=True)).astype(BF16)
        xo_ref[:, cols] = jnp.dot(p, v_ref[:, cols], preferred_element_type=F32).astype(BF16)
    o_ref[...] = h + jnp.dot(xo_ref[...], wo_ref[...], preferred_element_type=F32)


def _cross(h, gain, wq, gq, xk, xv, wo, seq, *, bm=256):
    t, d = h.shape
    m = xk.shape[1]
    xw = X_HEADS * X_HEAD_DIM
    bm = _tile(seq, bm)
    nb = seq // bm
    return pl.pallas_call(
        _cross_kernel,
        grid=(t // bm,),
        in_specs=[
            pl.BlockSpec((bm, d), lambda i: (i, 0)),
            pl.BlockSpec((1, d), lambda i: (0, 0)),
            pl.BlockSpec((d, xw), lambda i: (0, 0)),
            pl.BlockSpec((1, X_HEAD_DIM), lambda i: (0, 0)),
            pl.BlockSpec((None, m, xw), lambda i: (i // nb, 0, 0)),
            pl.BlockSpec((None, m, xw), lambda i: (i // nb, 0, 0)),
            pl.BlockSpec((xw, d), lambda i: (0, 0)),
        ],
        out_specs=pl.BlockSpec((bm, d), lambda i: (i, 0)),
        out_shape=jax.ShapeDtypeStruct((t, d), F32),
        scratch_shapes=[pltpu.VMEM((bm, xw), BF16)],
        compiler_params=_params("parallel"),
        name="cross_attn",
    )(h, gain.reshape(1, d), wq, gq, xk, xv, wo)


def _pad_lanes(v, width):
    return jnp.pad(v, (0, width - v.shape[0])).reshape(1, width)


def kernel(x, mem, positions, ffn1_norm, ffn1_w_gu, ffn1_w_down, mix_norm, w_in, w_pool, pool_scale,
           q_latent_norm, kv_latent_norm, w_uq, w_ukv, q_nope_norm, k_nope_norm, q_rope_norm,
           k_rope_norm, w_branch_pool, w_branch_mla, w_out, x_norm, mem_norm, w_xq, w_xkv, xq_norm,
           xk_norm, w_xo, ffn2_norm, ffn2_w_gu, ffn2_w_down):
    batch, seq, d = x.shape
    t = batch * seq
    depth = w_in.shape[0]
    pw = w_branch_pool.shape[1]
    ql = q_latent_norm.shape[1]
    kvl = kv_latent_norm.shape[1]
    heads = w_uq.shape[2] // QK_HEAD

    half = QK_ROPE // 2
    inv = 1.0 / (ROPE_THETA ** (jnp.arange(half, dtype=F32) * (2.0 / QK_ROPE)))
    inv = _pad_lanes(jnp.concatenate([inv, inv]), LANES)
    c2 = QK_HEAD ** -0.5 * math.log2(math.e)

    h = x.reshape(t, d)
    for l in range(depth):
        w_gu, w_down = _pad_ffn_weights(ffn1_w_gu[l], ffn1_w_down[l], 512)
        h = _ffn(h, ffn1_norm[l], w_gu, w_down, bf=512, x_buffers=1, name="ffn_wide")

        off_kv = pw + ql
        off_gate = off_kv + kvl + QK_ROPE
        n_in = w_in.shape[2]
        n_pad = -n_in % IN_TILE
        assert n_pad >= LANES
        w_all = jnp.pad(w_in[l], ((0, 0), (0, n_pad))).astype(BF16)
        z = _norm_matmul(h, mix_norm[l], w_all, bn=IN_TILE, name="in_proj")

        a_out = _pool(z, w_pool[l].astype(BF16), pool_scale[l], batch, seq)

        wuq = jnp.pad(w_uq[l].reshape(ql, heads, QK_HEAD), ((0, 0), (0, 0), (0, QK_PAD - QK_HEAD)))
        wuqt = wuq.transpose(1, 2, 0).astype(BF16)
        wukv = w_ukv[l].reshape(kvl, heads, QK_NOPE + V_HEAD)
        wuk = wukv[:, :, :QK_NOPE].transpose(1, 0, 2).astype(BF16)
        wuvt = wukv[:, :, QK_NOPE:].transpose(1, 2, 0).astype(BF16)
        gqt = jnp.concatenate([q_nope_norm[l], q_rope_norm[l], jnp.zeros((QK_PAD - QK_HEAD,), F32)]) * c2
        qt, k, vt = _qkv(z, positions, inv,
                         q_latent_norm[l].reshape(1, ql), kv_latent_norm[l].reshape(1, kvl), gqt,
                         k_nope_norm[l].reshape(1, QK_NOPE), _pad_lanes(k_rope_norm[l], LANES),
                         wuqt, wuk, wuvt, batch, seq, pw, off_kv)
        b_out = _flash(qt, k, vt).reshape(t, heads * V_HEAD)

        merged = _merge(a_out, b_out, w_branch_pool[l].astype(BF16), w_branch_mla[l].astype(BF16),
                        z, off_gate)
        h = _proj_residual(merged, w_out[l].astype(BF16), h)

        xk, xv = _mem_kv(mem, mem_norm[l], w_xkv[l].astype(BF16), xk_norm[l].reshape(1, X_HEAD_DIM))
        h = _cross(h, x_norm[l], w_xq[l].astype(BF16), xq_norm[l].reshape(1, X_HEAD_DIM), xk, xv,
                   w_xo[l].astype(BF16), seq)

        h = _ffn(h, ffn2_norm[l], ffn2_w_gu[l].astype(BF16), ffn2_w_down[l].astype(BF16),
                 bf=256, x_buffers=2, name="ffn")
    return h.reshape(batch, seq, d)
```

```python
import functools
import math

import jax
import jax.numpy as jnp
from jax import lax
from jax.experimental import pallas as pl
from jax.experimental.pallas import tpu as pltpu

F32 = jnp.float32
BF16 = jnp.bfloat16

POOL_WINDOWS = (2, 4, 8, 16)
POOL_HALO = 16
QK_NOPE = 128
QK_ROPE = 64
QK_HEAD = QK_NOPE + QK_ROPE
QK_PAD = 256
V_HEAD = 128
V_ROWS = V_HEAD + 16
X_HEADS = 4
X_HEAD_DIM = 128
ROPE_THETA = 10000.0
EPS = 1e-6
LANES = 128
VMEM_LIMIT = 62 * 1024 * 1024
NORM_ROWS = 256
IN_TILE = 1024


def _params(*sem):
    return pltpu.CompilerParams(dimension_semantics=sem, vmem_limit_bytes=VMEM_LIMIT)


def _tile(n, pref):
    if n <= pref:
        return n
    t = (pref // LANES) * LANES
    while t >= LANES:
        if n % t == 0:
            return t
        t -= LANES
    raise ValueError(f"no lane-aligned tile of {n} below {pref}")


def _rms(x, gain):
    ms = jnp.mean(x * x, axis=-1, keepdims=True)
    return x * lax.rsqrt(ms + EPS) * gain


def _ffn_kernel(x_ref, gain_ref, wg_ref, wu_ref, wd_ref, o_ref, xn_ref):
    @pl.when(pl.program_id(1) == 0)
    def _():
        for r in range(0, x_ref.shape[0], NORM_ROWS):
            rows = slice(r, min(r + NORM_ROWS, x_ref.shape[0]))
            x = x_ref[rows, :]
            xn_ref[rows, :] = _rms(x, gain_ref[...]).astype(BF16)
            o_ref[rows, :] = x

    xn = xn_ref[...]
    g = jnp.dot(xn, wg_ref[...], preferred_element_type=F32)
    u = jnp.dot(xn, wu_ref[...], preferred_element_type=F32)
    hid = (g * jax.nn.sigmoid(g) * u * 0.5).astype(BF16)
    o_ref[...] += jnp.dot(hid, wd_ref[...], preferred_element_type=F32)


def _ffn(x, gain, w_gu, w_down, *, bm=512, bf=256):
    t, d = x.shape
    f = w_down.shape[0]
    bm, bf = _tile(t, bm), _tile(f, bf)
    nf = f // bf
    return pl.pallas_call(
        _ffn_kernel,
        grid=(t // bm, nf),
        in_specs=[
            pl.BlockSpec((bm, d), lambda i, j: (i, 0)),
            pl.BlockSpec((1, d), lambda i, j: (0, 0)),
            pl.BlockSpec((d, bf), lambda i, j: (0, j)),
            pl.BlockSpec((d, bf), lambda i, j: (0, j + nf)),
            pl.BlockSpec((bf, d), lambda i, j: (j, 0)),
        ],
        out_specs=pl.BlockSpec((bm, d), lambda i, j: (i, 0)),
        out_shape=jax.ShapeDtypeStruct((t, d), F32),
        scratch_shapes=[pltpu.VMEM((bm, d), BF16)],
        compiler_params=_params("parallel", "arbitrary"),
        name="ffn",
    )(x, gain.reshape(1, d), w_gu, w_gu, w_down)


def _norm_matmul_kernel(x_ref, gain_ref, w_ref, o_ref, xn_ref):
    @pl.when(pl.program_id(1) == 0)
    def _():
        for r in range(0, x_ref.shape[0], NORM_ROWS):
            rows = slice(r, min(r + NORM_ROWS, x_ref.shape[0]))
            xn_ref[rows, :] = _rms(x_ref[rows, :], gain_ref[...]).astype(BF16)

    o_ref[...] = jnp.dot(xn_ref[...], w_ref[...], preferred_element_type=F32)


def _norm_matmul(x, gain, w, *, bm=1024, bn=1024, name):
    t, d = x.shape
    n = w.shape[1]
    bm, bn = _tile(t, bm), min(bn, n)
    return pl.pallas_call(
        _norm_matmul_kernel,
        grid=(t // bm, pl.cdiv(n, bn)),
        in_specs=[
            pl.BlockSpec((bm, d), lambda i, j: (i, 0), pipeline_mode=pl.Buffered(1)),
            pl.BlockSpec((1, d), lambda i, j: (0, 0)),
            pl.BlockSpec((d, bn), lambda i, j: (0, j)),
        ],
        out_specs=pl.BlockSpec((bm, bn), lambda i, j: (i, j)),
        out_shape=jax.ShapeDtypeStruct((t, n), F32),
        scratch_shapes=[pltpu.VMEM((bm, d), BF16)],
        compiler_params=_params("parallel", "arbitrary"),
        name=name,
    )(x, gain.reshape(1, d), w)


def _pool_kernel(zc_ref, zp_ref, wp_ref, sc_ref, o_ref, ext_ref, *, bm, c):
    i = pl.program_id(1)
    ext_ref[0:POOL_HALO, :] = jnp.where(i > 0, zp_ref[...], 0.0)
    ext_ref[POOL_HALO:POOL_HALO + bm, :] = zc_ref[...]
    t = i * bm + lax.broadcasted_iota(jnp.int32, (bm, 1), 0)
    for g, w in enumerate(POOL_WINDOWS):
        cols = slice(g * c, (g + 1) * c)
        x = zc_ref[:, cols]
        acc = x
        for k in range(1, w):
            acc = acc + ext_ref[POOL_HALO - k:POOL_HALO - k + bm, cols]
        cnt = jnp.minimum(t + 1, w).astype(F32)
        mixed = (acc / cnt - x).astype(BF16)
        y = jnp.dot(mixed, wp_ref[g], preferred_element_type=F32) * sc_ref[:, cols]
        o_ref[:, cols] = y.astype(BF16)


def _pool(z_main, w_pool, pool_scale, batch, seq, *, bm=512):
    g, c, _ = w_pool.shape
    pw = g * c
    bm = _tile(seq, bm)
    nb = seq // bm
    halo_blocks = bm // POOL_HALO
    return pl.pallas_call(
        functools.partial(_pool_kernel, bm=bm, c=c),
        grid=(batch, nb),
        in_specs=[
            pl.BlockSpec((bm, pw), lambda b, i: (b * nb + i, 0)),
            pl.BlockSpec((POOL_HALO, pw),
                         lambda b, i: (jnp.maximum((b * nb + i) * halo_blocks - 1, 0), 0)),
            pl.BlockSpec((g, c, c), lambda b, i: (0, 0, 0)),
            pl.BlockSpec((1, pw), lambda b, i: (0, 0)),
        ],
        out_specs=pl.BlockSpec((bm, pw), lambda b, i: (b * nb + i, 0)),
        out_shape=jax.ShapeDtypeStruct((batch * seq, pw), BF16),
        scratch_shapes=[pltpu.VMEM((POOL_HALO + bm, pw), F32)],
        compiler_params=_params("parallel", "parallel"),
        name="pool",
    )(z_main, z_main, w_pool, pool_scale.reshape(1, pw))


def _rope_tables(pos, inv):
    ang = pos.astype(F32) * inv
    cos, sin = jnp.cos(ang), jnp.sin(ang)
    lane = lax.broadcasted_iota(jnp.int32, ang.shape, 1)
    half = QK_ROPE // 2
    c = jnp.where(lane < QK_ROPE, cos, 0.0)
    sa = jnp.where(lane < half, -sin, 0.0)
    sb = jnp.where((lane >= half) & (lane < QK_ROPE), sin, 0.0)
    return c, sa, sb


def _rope_norm(r, gain, c, sa, sb):
    ms = jnp.sum(r * r, axis=-1, keepdims=True) * (1.0 / QK_ROPE)
    r = r * lax.rsqrt(ms + EPS) * gain
    half = QK_ROPE // 2
    return r * c + pltpu.roll(r, LANES - half, 1) * sa + pltpu.roll(r, half, 1) * sb


def _qkv_kernel(zq_ref, zkv_ref, zkr_ref, pos_ref, post_ref, inv_ref, invt_ref, gql_ref, gkvl_ref, gqt_ref,
                gkn_ref, gkr_ref, wuqt_ref, wuk_ref, wuvt_ref, qt_ref, k_ref, vt_ref,
                cqt_ref, ckv_ref, ckvt_ref, kr_ref, cost_ref, sint_ref):
    half = QK_ROPE // 2
    r1, r2, r3 = QK_NOPE, QK_NOPE + half, QK_HEAD

    @pl.when(pl.program_id(1) == 0)
    def _():
        cqt_ref[...] = _rms(zq_ref[...], gql_ref[...]).T.astype(BF16)
        ckv = _rms(zkv_ref[...], gkvl_ref[...])
        ckv_ref[...] = ckv.astype(BF16)
        ckvt_ref[...] = ckv.T.astype(BF16)
        c, sa, sb = _rope_tables(pos_ref[...], inv_ref[...])
        lane = lax.broadcasted_iota(jnp.int32, zkr_ref.shape, 1)
        zkr = jnp.where(lane < QK_ROPE, zkr_ref[...], 0.0)
        kr_ref[...] = _rope_norm(zkr, gkr_ref[...], c, sa, sb).astype(BF16)
        ang = post_ref[...].astype(F32) * invt_ref[...]
        cost_ref[...] = jnp.cos(ang)
        sint_ref[...] = jnp.sin(ang)

    bm = qt_ref.shape[-1]
    for s in range(qt_ref.shape[0]):
        q = jnp.dot(wuqt_ref[s], cqt_ref[...], preferred_element_type=F32)
        nope = q[:r1]
        rs = lax.rsqrt(jnp.mean(nope * nope, axis=0, keepdims=True) + EPS)
        qt_ref[s, :r1, :] = (nope * rs * gqt_ref[:r1, :]).astype(BF16)
        x1, x2 = q[r1:r2], q[r2:r3]
        ms = (jnp.sum(x1 * x1, axis=0, keepdims=True)
              + jnp.sum(x2 * x2, axis=0, keepdims=True)) * (1.0 / QK_ROPE)
        rs = lax.rsqrt(ms + EPS)
        y1, y2 = x1 * rs * gqt_ref[r1:r2, :], x2 * rs * gqt_ref[r2:r3, :]
        cos, sin = cost_ref[...], sint_ref[...]
        qt_ref[s, r1:r2, :] = (y1 * cos - y2 * sin).astype(BF16)
        qt_ref[s, r2:r3, :] = (y2 * cos + y1 * sin).astype(BF16)
        qt_ref[s, r3:, :] = jnp.zeros((QK_PAD - r3, bm), BF16)

        k = jnp.dot(ckv_ref[...], wuk_ref[s], preferred_element_type=F32)
        k_ref[s, :, :QK_NOPE] = _rms(k, gkn_ref[...]).astype(BF16)
        k_ref[s, :, QK_NOPE:] = kr_ref[...]
        vt = jnp.dot(wuvt_ref[s], ckvt_ref[...], preferred_element_type=F32)
        vt_ref[s, :V_HEAD, :] = vt.astype(BF16)
        vt_ref[s, V_HEAD:, :] = jnp.ones((V_ROWS - V_HEAD, bm), BF16)


def _qkv(z, pos, inv, gql, gkvl, gqt, gkn, gkr, wuqt, wuk, wuvt, batch, seq, off_q, off_kv, *, bm=512, hp=4):
    h, _, ql = wuqt.shape
    kvl = wuk.shape[1]
    bm = _tile(seq, bm)
    nb = seq // bm
    t = batch * seq
    half = QK_ROPE // 2
    assert off_q % ql == 0 and off_kv % kvl == 0 and (off_kv + kvl) % LANES == 0 and h % hp == 0
    const = lambda i, hh: (0, 0)
    head = lambda i, hh: (hh, 0, 0)
    out_map = lambda i, hh: (i // nb, hh, i % nb, 0)
    out_map_t = lambda i, hh: (i // nb, hh, i % nb, 0, 0)
    invt = jnp.broadcast_to(inv[0, :half].reshape(half, 1), (half, bm))
    gqt = jnp.broadcast_to(gqt.reshape(QK_PAD, 1), (QK_PAD, bm))
    return pl.pallas_call(
        _qkv_kernel,
        grid=(t // bm, h // hp),
        in_specs=[
            pl.BlockSpec((bm, ql), lambda i, hh: (i, off_q // ql)),
            pl.BlockSpec((bm, kvl), lambda i, hh: (i, off_kv // kvl)),
            pl.BlockSpec((bm, LANES), lambda i, hh: (i, (off_kv + kvl) // LANES)),
            pl.BlockSpec((bm, 1), lambda i, hh: (i, 0)),
            pl.BlockSpec((1, bm), lambda i, hh: (0, i)),
            pl.BlockSpec((1, LANES), const),
            pl.BlockSpec((half, bm), const),
            pl.BlockSpec((1, ql), const),
            pl.BlockSpec((1, kvl), const),
            pl.BlockSpec((QK_PAD, bm), const),
            pl.BlockSpec((1, QK_NOPE), const),
            pl.BlockSpec((1, LANES), const),
            pl.BlockSpec((hp, QK_PAD, ql), head),
            pl.BlockSpec((hp, kvl, QK_NOPE), head),
            pl.BlockSpec((hp, V_HEAD, kvl), head),
        ],
        out_specs=[
            pl.BlockSpec((None, hp, None, QK_PAD, bm), out_map_t),
            pl.BlockSpec((None, hp, bm, QK_PAD), out_map),
            pl.BlockSpec((None, hp, None, V_ROWS, bm), out_map_t),
        ],
        out_shape=[
            jax.ShapeDtypeStruct((batch, h, nb, QK_PAD, bm), BF16),
            jax.ShapeDtypeStruct((batch, h, seq, QK_PAD), BF16),
            jax.ShapeDtypeStruct((batch, h, nb, V_ROWS, bm), BF16),
        ],
        scratch_shapes=[
            pltpu.VMEM((ql, bm), BF16),
            pltpu.VMEM((bm, kvl), BF16),
            pltpu.VMEM((kvl, bm), BF16),
            pltpu.VMEM((bm, LANES), BF16),
            pltpu.VMEM((half, bm), F32),
            pltpu.VMEM((half, bm), F32),
        ],
        compiler_params=_params("parallel", "arbitrary"),
        name="qkv_prep",
    )(z, z, z, pos.reshape(t, 1), pos.reshape(1, t), inv, invt, gql, gkvl, gqt, gkn, gkr, wuqt, wuk, wuvt)


def _flash_kernel(qt_ref, k_ref, vt_ref, o_ref, sa_ref, sb_ref, m_ref, acc_ref, *, blk, chunk):
    qi = pl.program_id(2)
    nch = 2 * blk // chunk
    per_blk = blk // chunk
    m_ref[...] = jnp.full(m_ref.shape, -jnp.inf, F32)
    acc_ref[...] = jnp.zeros(acc_ref.shape, F32)

    def visibility(c, diag):
        if diag is None or c * chunk >= (diag + 1) * blk:
            return "all"
        return "none" if (c + 1) * chunk <= diag * blk else "some"

    def scores(kj, s_ref, diag=None):
        start = pl.multiple_of(kj * blk, blk)
        k = k_ref[pl.ds(start, blk), :]
        for c in range(nch):
            if visibility(c, diag) != "none":
                qt = qt_ref[c // per_blk, :, (c % per_blk) * chunk:(c % per_blk + 1) * chunk]
                s_ref[c] = jnp.dot(k, qt, preferred_element_type=F32)

    def update(kj, s_ref, diag=None):
        vt = vt_ref[kj]
        for c in range(nch):
            see = visibility(c, diag)
            if see == "none":
                continue
            cols = slice(c * chunk, (c + 1) * chunk)
            st = s_ref[c]
            if see == "some":
                key = lax.broadcasted_iota(jnp.int32, st.shape, 0) + diag * blk
                qry = lax.broadcasted_iota(jnp.int32, st.shape, 1) + c * chunk
                st = jnp.where(key <= qry, st, -jnp.inf)
            m_old = m_ref[:, cols]
            m_new = jnp.maximum(m_old, jnp.max(st, axis=0, keepdims=True))
            p = jnp.exp2(st - m_new).astype(BF16)
            alpha = jnp.exp2(m_old - m_new)
            acc_ref[:, cols] = alpha * acc_ref[:, cols] + jnp.dot(vt, p, preferred_element_type=F32)
            m_ref[:, cols] = m_new

    scores(0, sa_ref)

    def pair(t, carry):
        scores(2 * t + 1, sb_ref)
        update(2 * t, sa_ref)
        scores(2 * t + 2, sa_ref)
        update(2 * t + 1, sb_ref)
        return carry

    lax.fori_loop(0, qi, pair, 0)

    scores(2 * qi + 1, sb_ref, diag=1)
    update(2 * qi, sa_ref, diag=0)
    update(2 * qi + 1, sb_ref, diag=1)

    o_ref[...] = (acc_ref[:V_HEAD, :] / acc_ref[V_HEAD:V_HEAD + 1, :]).T.astype(o_ref.dtype)


def _flash(qt, k, vt, *, chunk=256):
    b, h, nb, _, blk = qt.shape
    s = nb * blk
    chunk = min(chunk, blk)
    assert nb % 2 == 0
    return pl.pallas_call(
        functools.partial(_flash_kernel, blk=blk, chunk=chunk),
        grid=(b, h, nb // 2),
        in_specs=[
            pl.BlockSpec((None, None, 2, QK_PAD, blk), lambda bi, hi, qi: (bi, hi, qi, 0, 0)),
            pl.BlockSpec((None, None, s, QK_PAD), lambda bi, hi, qi: (bi, hi, 0, 0)),
            pl.BlockSpec((None, None, nb, V_ROWS, blk), lambda bi, hi, qi: (bi, hi, 0, 0, 0)),
        ],
        out_specs=pl.BlockSpec((None, 2 * blk, V_HEAD), lambda bi, hi, qi: (bi, qi, hi)),
        out_shape=jax.ShapeDtypeStruct((b, s, h * V_HEAD), BF16),
        scratch_shapes=[
            pltpu.VMEM((2 * blk // chunk, blk, chunk), F32),
            pltpu.VMEM((2 * blk // chunk, blk, chunk), F32),
            pltpu.VMEM((1, 2 * blk), F32),
            pltpu.VMEM((V_ROWS, 2 * blk), F32),
        ],
        compiler_params=_params("parallel", "parallel", "arbitrary"),
        name="mla_flash",
    )(qt, k, vt)


def _merge_kernel(a_ref, b_ref, wa_ref, wb_ref, gp_ref, gpx_ref, gm_ref, gmx_ref, o_ref, *, shift):
    bn = o_ref.shape[1]

    def gate(lo_ref, hi_ref):
        g = jnp.concatenate([lo_ref[...], hi_ref[...]], axis=1)[:, shift:shift + bn]
        return jax.nn.sigmoid(g)

    pa = jnp.dot(a_ref[...], wa_ref[...], preferred_element_type=F32)
    pb = jnp.dot(b_ref[...], wb_ref[...], preferred_element_type=F32)
    o_ref[...] = (gate(gp_ref, gpx_ref) * pa + gate(gm_ref, gmx_ref) * pb).astype(BF16)


def _merge(a, b, wa, wb, z, gate_off, *, bm=1024, bn=512):
    t, pw = a.shape
    mw = b.shape[1]
    d = wa.shape[1]
    shift = gate_off % LANES
    base = gate_off - shift
    bm, bn = _tile(t, bm), _tile(math.gcd(d, base), bn)
    lanes_per_tile = bn // LANES
    lo = lambda off: (lambda i, j: (i, off // bn + j))
    hi = lambda off: (lambda i, j: (i, (off + bn) // LANES + j * lanes_per_tile))
    return pl.pallas_call(
        functools.partial(_merge_kernel, shift=shift),
        grid=(t // bm, d // bn),
        in_specs=[
            pl.BlockSpec((bm, pw), lambda i, j: (i, 0)),
            pl.BlockSpec((bm, mw), lambda i, j: (i, 0)),
            pl.BlockSpec((pw, bn), lambda i, j: (0, j)),
            pl.BlockSpec((mw, bn), lambda i, j: (0, j)),
            pl.BlockSpec((bm, bn), lo(base)),
            pl.BlockSpec((bm, LANES), hi(base)),
            pl.BlockSpec((bm, bn), lo(base + d)),
            pl.BlockSpec((bm, LANES), hi(base + d)),
        ],
        out_specs=pl.BlockSpec((bm, bn), lambda i, j: (i, j)),
        out_shape=jax.ShapeDtypeStruct((t, d), BF16),
        compiler_params=_params("parallel", "arbitrary"),
        name="branch_merge",
    )(a, b, wa, wb, z, z, z, z)


def _proj_residual_kernel(a_ref, w_ref, h_ref, o_ref):
    o_ref[...] = h_ref[...] + jnp.dot(a_ref[...], w_ref[...], preferred_element_type=F32)


def _proj_residual(a, w, h, *, bm=1024, bn=1024):
    t, kdim = a.shape
    d = w.shape[1]
    bm, bn = _tile(t, bm), _tile(d, bn)
    return pl.pallas_call(
        _proj_residual_kernel,
        grid=(t // bm, d // bn),
        in_specs=[
            pl.BlockSpec((bm, kdim), lambda i, j: (i, 0)),
            pl.BlockSpec((kdim, bn), lambda i, j: (0, j)),
            pl.BlockSpec((bm, bn), lambda i, j: (i, j)),
        ],
        out_specs=pl.BlockSpec((bm, bn), lambda i, j: (i, j)),
        out_shape=jax.ShapeDtypeStruct((t, d), F32),
        compiler_params=_params("parallel", "arbitrary"),
        name="out_proj",
    )(a, w, h)


def _mem_kv_kernel(mem_ref, gain_ref, w_ref, gk_ref, k_ref, v_ref):
    mn = _rms(mem_ref[...], gain_ref[...]).astype(BF16)
    kv = jnp.dot(mn, w_ref[...], preferred_element_type=F32)
    xw = X_HEADS * X_HEAD_DIM
    for hh in range(X_HEADS):
        cols = slice(hh * X_HEAD_DIM, (hh + 1) * X_HEAD_DIM)
        k_ref[:, cols] = _rms(kv[:, cols], gk_ref[...]).astype(BF16)
    v_ref[...] = kv[:, xw:].astype(BF16)


def _mem_kv(mem, gain, w_xkv, gk):
    b, m, d = mem.shape
    xw = X_HEADS * X_HEAD_DIM
    return pl.pallas_call(
        _mem_kv_kernel,
        grid=(b,),
        in_specs=[
            pl.BlockSpec((None, m, d), lambda bi: (bi, 0, 0)),
            pl.BlockSpec((1, d), lambda bi: (0, 0)),
            pl.BlockSpec((d, 2 * xw), lambda bi: (0, 0)),
            pl.BlockSpec((1, X_HEAD_DIM), lambda bi: (0, 0)),
        ],
        out_specs=[
            pl.BlockSpec((None, m, xw), lambda bi: (bi, 0, 0)),
            pl.BlockSpec((None, m, xw), lambda bi: (bi, 0, 0)),
        ],
        out_shape=[
            jax.ShapeDtypeStruct((b, m, xw), BF16),
            jax.ShapeDtypeStruct((b, m, xw), BF16),
        ],
        compiler_params=_params("parallel"),
        name="mem_kv",
    )(mem, gain.reshape(1, d), w_xkv, gk)


def _cross_kernel(h_ref, gain_ref, wq_ref, gq_ref, k_ref, v_ref, wo_ref, o_ref, xo_ref):
    h = h_ref[...]
    uq = _rms(h, gain_ref[...]).astype(BF16)
    xq = jnp.dot(uq, wq_ref[...], preferred_element_type=F32)
    scale = X_HEAD_DIM ** -0.5
    for hh in range(X_HEADS):
        cols = slice(hh * X_HEAD_DIM, (hh + 1) * X_HEAD_DIM)
        qh = _rms(xq[:, cols], gq_ref[...]).astype(BF16)
        s = lax.dot_general(qh, k_ref[:, cols], (((1,), (1,)), ((), ())),
                            preferred_element_type=F32) * scale
        p = jnp.exp(s - jnp.max(s, axis=-1, keepdims=True))
        p = (p / jnp.sum(p, axis=-1, keepdims=True)).astype(BF16)
        xo_ref[:, cols] = jnp.dot(p, v_ref[:, cols], preferred_element_type=F32).astype(BF16)
    o_ref[...] = h + jnp.dot(xo_ref[...], wo_ref[...], preferred_element_type=F32)


def _cross(h, gain, wq, gq, xk, xv, wo, seq, *, bm=256):
    t, d = h.shape
    m = xk.shape[1]
    xw = X_HEADS * X_HEAD_DIM
    bm = _tile(seq, bm)
    nb = seq // bm
    return pl.pallas_call(
        _cross_kernel,
        grid=(t // bm,),
        in_specs=[
            pl.BlockSpec((bm, d), lambda i: (i, 0)),
            pl.BlockSpec((1, d), lambda i: (0, 0)),
            pl.BlockSpec((d, xw), lambda i: (0, 0)),
            pl.BlockSpec((1, X_HEAD_DIM), lambda i: (0, 0)),
            pl.BlockSpec((None, m, xw), lambda i: (i // nb, 0, 0)),
            pl.BlockSpec((None, m, xw), lambda i: (i // nb, 0, 0)),
            pl.BlockSpec((xw, d), lambda i: (0, 0)),
        ],
        out_specs=pl.BlockSpec((bm, d), lambda i: (i, 0)),
        out_shape=jax.ShapeDtypeStruct((t, d), F32),
        scratch_shapes=[pltpu.VMEM((bm, xw), BF16)],
        compiler_params=_params("parallel"),
        name="cross_attn",
    )(h, gain.reshape(1, d), wq, gq, xk, xv, wo)


def _pad_lanes(v, width):
    return jnp.pad(v, (0, width - v.shape[0])).reshape(1, width)


def kernel(x, mem, positions, ffn1_norm, ffn1_w_gu, ffn1_w_down, mix_norm, w_in, w_pool, pool_scale,
           q_latent_norm, kv_latent_norm, w_uq, w_ukv, q_nope_norm, k_nope_norm, q_rope_norm,
           k_rope_norm, w_branch_pool, w_branch_mla, w_out, x_norm, mem_norm, w_xq, w_xkv, xq_norm,
           xk_norm, w_xo, ffn2_norm, ffn2_w_gu, ffn2_w_down):
    batch, seq, d = x.shape
    t = batch * seq
    depth = w_in.shape[0]
    pw = w_branch_pool.shape[1]
    ql = q_latent_norm.shape[1]
    kvl = kv_latent_norm.shape[1]
    heads = w_uq.shape[2] // QK_HEAD

    half = QK_ROPE // 2
    inv = 1.0 / (ROPE_THETA ** (jnp.arange(half, dtype=F32) * (2.0 / QK_ROPE)))
    inv = _pad_lanes(jnp.concatenate([inv, inv]), LANES)
    c2 = QK_HEAD ** -0.5 * math.log2(math.e)

    h = x.reshape(t, d)
    for l in range(depth):
        h = _ffn(h, ffn1_norm[l], ffn1_w_gu[l].astype(BF16), ffn1_w_down[l].astype(BF16))

        off_kv = pw + ql
        off_gate = off_kv + kvl + QK_ROPE
        z = _norm_matmul(h, mix_norm[l], w_in[l].astype(BF16), bn=IN_TILE, name="in_proj")

        a_out = _pool(z, w_pool[l].astype(BF16), pool_scale[l], batch, seq)

        wuq = jnp.pad(w_uq[l].reshape(ql, heads, QK_HEAD), ((0, 0), (0, 0), (0, QK_PAD - QK_HEAD)))
        wuqt = wuq.transpose(1, 2, 0).astype(BF16)
        wukv = w_ukv[l].reshape(kvl, heads, QK_NOPE + V_HEAD)
        wuk = wukv[:, :, :QK_NOPE].transpose(1, 0, 2).astype(BF16)
        wuvt = wukv[:, :, QK_NOPE:].transpose(1, 2, 0).astype(BF16)
        gqt = jnp.concatenate([q_nope_norm[l], q_rope_norm[l], jnp.zeros((QK_PAD - QK_HEAD,), F32)]) * c2
        qt, k, vt = _qkv(z, positions, inv,
                         q_latent_norm[l].reshape(1, ql), kv_latent_norm[l].reshape(1, kvl), gqt,
                         k_nope_norm[l].reshape(1, QK_NOPE), _pad_lanes(k_rope_norm[l], LANES),
                         wuqt, wuk, wuvt, batch, seq, pw, off_kv)
        b_out = _flash(qt, k, vt).reshape(t, heads * V_HEAD)

        merged = _merge(a_out, b_out, w_branch_pool[l].astype(BF16), w_branch_mla[l].astype(BF16),
                        z, off_gate)
        h = _proj_residual(merged, w_out[l].astype(BF16), h)

        xk, xv = _mem_kv(mem, mem_norm[l], w_xkv[l].astype(BF16), xk_norm[l].reshape(1, X_HEAD_DIM))
        h = _cross(h, x_norm[l], w_xq[l].astype(BF16), xq_norm[l].reshape(1, X_HEAD_DIM), xk, xv,
                   w_xo[l].astype(BF16), seq)

        h = _ffn(h, ffn2_norm[l], ffn2_w_gu[l].astype(BF16), ffn2_w_down[l].astype(BF16))
    return h.reshape(batch, seq, d)
```

```python
import functools
import math

import jax
import jax.numpy as jnp
from jax import lax
from jax.experimental import pallas as pl
from jax.experimental.pallas import tpu as pltpu

F32 = jnp.float32
BF16 = jnp.bfloat16

POOL_WINDOWS = (2, 4, 8, 16)
POOL_HALO = 16
QK_NOPE = 128
QK_ROPE = 64
QK_HEAD = QK_NOPE + QK_ROPE
QK_PAD = 256
V_HEAD = 128
V_ROWS = V_HEAD + 16
X_HEADS = 4
X_HEAD_DIM = 128
ROPE_THETA = 10000.0
EPS = 1e-6
LANES = 128
BF16_ROWS = 16
VMEM_LIMIT = 62 * 1024 * 1024
NORM_ROWS = 256
IN_TILE = 1024


def _params(*sem):
    return pltpu.CompilerParams(dimension_semantics=sem, vmem_limit_bytes=VMEM_LIMIT)


def _tile(n, pref):
    if n <= pref:
        return n
    t = (pref // LANES) * LANES
    while t >= LANES:
        if n % t == 0:
            return t
        t -= LANES
    raise ValueError(f"no lane-aligned tile of {n} below {pref}")


def _ceil_to(x, m):
    return -(-x // m) * m


def _rms(x, gain):
    ms = jnp.mean(x * x, axis=-1, keepdims=True)
    return x * lax.rsqrt(ms + EPS) * gain


def _ffn_kernel(x_ref, gain_ref, wg_ref, wu_ref, wd_ref, o_ref, xn_ref):
    @pl.when(pl.program_id(1) == 0)
    def _():
        for r in range(0, x_ref.shape[0], NORM_ROWS):
            rows = slice(r, min(r + NORM_ROWS, x_ref.shape[0]))
            x = x_ref[rows, :]
            xn_ref[rows, :] = _rms(x, gain_ref[...]).astype(BF16)
            o_ref[rows, :] = x

    xn = xn_ref[...]
    g = jnp.dot(xn, wg_ref[...], preferred_element_type=F32)
    u = jnp.dot(xn, wu_ref[...], preferred_element_type=F32)
    hid = (g * jax.nn.sigmoid(g) * u * 0.5).astype(BF16)
    o_ref[...] += jnp.dot(hid, wd_ref[...], preferred_element_type=F32)


def _cast_specs(ws, layer, gi, gj, rc):
    ins, outs = [], []
    for w in ws:
        _, r, c = w.shape
        br = _ceil_to(pl.cdiv(r, gi), BF16_ROWS)
        bc = _ceil_to(pl.cdiv(c, gj), LANES)
        last = pl.cdiv(c, bc) - 1
        assert (gi - 1) * br < r <= gi * br and last < gj

        def block(*g, last=last):
            i, j = rc(*g)
            return i, jnp.minimum(j, last)

        ins.append(pl.BlockSpec((None, br, bc), lambda *g, block=block: (layer, *block(*g))))
        outs.append(pl.BlockSpec((br, bc), block))
    return ins, outs


def _cast_block(cast_in, cast_out):
    for src, dst in zip(cast_in, cast_out):
        dst[...] = src[...].astype(BF16)


def _ffn(x, gain, w_gu, w_down, *, bm=512, bf=256):
    t, d = x.shape
    f = w_down.shape[0]
    bm, bf = _tile(t, bm), _tile(f, bf)
    nf = f // bf
    return pl.pallas_call(
        _ffn_kernel,
        grid=(t // bm, nf),
        in_specs=[
            pl.BlockSpec((bm, d), lambda i, j: (i, 0)),
            pl.BlockSpec((1, d), lambda i, j: (0, 0)),
            pl.BlockSpec((d, bf), lambda i, j: (0, j)),
            pl.BlockSpec((d, bf), lambda i, j: (0, j + nf)),
            pl.BlockSpec((bf, d), lambda i, j: (j, 0)),
        ],
        out_specs=pl.BlockSpec((bm, d), lambda i, j: (i, 0)),
        out_shape=jax.ShapeDtypeStruct((t, d), F32),
        scratch_shapes=[pltpu.VMEM((bm, d), BF16)],
        compiler_params=_params("parallel", "arbitrary"),
        name="ffn",
    )(x, gain.reshape(1, d), w_gu, w_gu, w_down)


def _norm_matmul_kernel(x_ref, gain_ref, w_ref, o_ref, xn_ref):
    @pl.when(pl.program_id(1) == 0)
    def _():
        for r in range(0, x_ref.shape[0], NORM_ROWS):
            rows = slice(r, min(r + NORM_ROWS, x_ref.shape[0]))
            xn_ref[rows, :] = _rms(x_ref[rows, :], gain_ref[...]).astype(BF16)

    o_ref[...] = jnp.dot(xn_ref[...], w_ref[...], preferred_element_type=F32)


def _norm_matmul(x, gain, w, *, bm=1024, bn=1024, name):
    t, d = x.shape
    n = w.shape[1]
    bm, bn = _tile(t, bm), min(bn, n)
    return pl.pallas_call(
        _norm_matmul_kernel,
        grid=(t // bm, pl.cdiv(n, bn)),
        in_specs=[
            pl.BlockSpec((bm, d), lambda i, j: (i, 0), pipeline_mode=pl.Buffered(1)),
            pl.BlockSpec((1, d), lambda i, j: (0, 0)),
            pl.BlockSpec((d, bn), lambda i, j: (0, j)),
        ],
        out_specs=pl.BlockSpec((bm, bn), lambda i, j: (i, j)),
        out_shape=jax.ShapeDtypeStruct((t, n), F32),
        scratch_shapes=[pltpu.VMEM((bm, d), BF16)],
        compiler_params=_params("parallel", "arbitrary"),
        name=name,
    )(x, gain.reshape(1, d), w)


def _pool_kernel(zc_ref, zp_ref, wp_ref, sc_ref, o_ref, ext_ref, *, bm, c):
    i = pl.program_id(1)
    ext_ref[0:POOL_HALO, :] = jnp.where(i > 0, zp_ref[...], 0.0)
    ext_ref[POOL_HALO:POOL_HALO + bm, :] = zc_ref[...]
    t = i * bm + lax.broadcasted_iota(jnp.int32, (bm, 1), 0)
    for g, w in enumerate(POOL_WINDOWS):
        cols = slice(g * c, (g + 1) * c)
        x = zc_ref[:, cols]
        acc = x
        for k in range(1, w):
            acc = acc + ext_ref[POOL_HALO - k:POOL_HALO - k + bm, cols]
        cnt = jnp.minimum(t + 1, w).astype(F32)
        mixed = (acc / cnt - x).astype(BF16)
        y = jnp.dot(mixed, wp_ref[g], preferred_element_type=F32) * sc_ref[:, cols]
        o_ref[:, cols] = y.astype(BF16)


def _pool(z_main, w_pool, pool_scale, batch, seq, *, bm=512):
    g, c, _ = w_pool.shape
    pw = g * c
    bm = _tile(seq, bm)
    nb = seq // bm
    halo_blocks = bm // POOL_HALO
    return pl.pallas_call(
        functools.partial(_pool_kernel, bm=bm, c=c),
        grid=(batch, nb),
        in_specs=[
            pl.BlockSpec((bm, pw), lambda b, i: (b * nb + i, 0)),
            pl.BlockSpec((POOL_HALO, pw),
                         lambda b, i: (jnp.maximum((b * nb + i) * halo_blocks - 1, 0), 0)),
            pl.BlockSpec((g, c, c), lambda b, i: (0, 0, 0)),
            pl.BlockSpec((1, pw), lambda b, i: (0, 0)),
        ],
        out_specs=pl.BlockSpec((bm, pw), lambda b, i: (b * nb + i, 0)),
        out_shape=jax.ShapeDtypeStruct((batch * seq, pw), BF16),
        scratch_shapes=[pltpu.VMEM((POOL_HALO + bm, pw), F32)],
        compiler_params=_params("parallel", "parallel"),
        name="pool",
    )(z_main, z_main, w_pool, pool_scale.reshape(1, pw))


def _rope_tables(pos, inv):
    ang = pos.astype(F32) * inv
    cos, sin = jnp.cos(ang), jnp.sin(ang)
    lane = lax.broadcasted_iota(jnp.int32, ang.shape, 1)
    half = QK_ROPE // 2
    c = jnp.where(lane < QK_ROPE, cos, 0.0)
    sa = jnp.where(lane < half, -sin, 0.0)
    sb = jnp.where((lane >= half) & (lane < QK_ROPE), sin, 0.0)
    return c, sa, sb


def _rope_norm(r, gain, c, sa, sb):
    ms = jnp.sum(r * r, axis=-1, keepdims=True) * (1.0 / QK_ROPE)
    r = r * lax.rsqrt(ms + EPS) * gain
    half = QK_ROPE // 2
    return r * c + pltpu.roll(r, LANES - half, 1) * sa + pltpu.roll(r, half, 1) * sb


def _qkv_kernel(zq_ref, zkv_ref, zkr_ref, pos_ref, post_ref, inv_ref, invt_ref, gql_ref, gkvl_ref, gqt_ref,
                gkn_ref, gkr_ref, wuqt_ref, wuk_ref, wuvt_ref, qt_ref, k_ref, vt_ref,
                cqt_ref, ckv_ref, ckvt_ref, kr_ref, cost_ref, sint_ref):
    half = QK_ROPE // 2
    r1, r2, r3 = QK_NOPE, QK_NOPE + half, QK_HEAD

    @pl.when(pl.program_id(1) == 0)
    def _():
        cqt_ref[...] = _rms(zq_ref[...], gql_ref[...]).T.astype(BF16)
        ckv = _rms(zkv_ref[...], gkvl_ref[...])
        ckv_ref[...] = ckv.astype(BF16)
        ckvt_ref[...] = ckv.T.astype(BF16)
        c, sa, sb = _rope_tables(pos_ref[...], inv_ref[...])
        lane = lax.broadcasted_iota(jnp.int32, zkr_ref.shape, 1)
        zkr = jnp.where(lane < QK_ROPE, zkr_ref[...], 0.0)
        kr_ref[...] = _rope_norm(zkr, gkr_ref[...], c, sa, sb).astype(BF16)
        ang = post_ref[...].astype(F32) * invt_ref[...]
        cost_ref[...] = jnp.cos(ang)
        sint_ref[...] = jnp.sin(ang)

    bm = qt_ref.shape[-1]
    for s in range(qt_ref.shape[0]):
        q = jnp.dot(wuqt_ref[s], cqt_ref[...], preferred_element_type=F32)
        nope = q[:r1]
        rs = lax.rsqrt(jnp.mean(nope * nope, axis=0, keepdims=True) + EPS)
        qt_ref[s, :r1, :] = (nope * rs * gqt_ref[:r1, :]).astype(BF16)
        x1, x2 = q[r1:r2], q[r2:r3]
        ms = (jnp.sum(x1 * x1, axis=0, keepdims=True)
              + jnp.sum(x2 * x2, axis=0, keepdims=True)) * (1.0 / QK_ROPE)
        rs = lax.rsqrt(ms + EPS)
        y1, y2 = x1 * rs * gqt_ref[r1:r2, :], x2 * rs * gqt_ref[r2:r3, :]
        cos, sin = cost_ref[...], sint_ref[...]
        qt_ref[s, r1:r2, :] = (y1 * cos - y2 * sin).astype(BF16)
        qt_ref[s, r2:r3, :] = (y2 * cos + y1 * sin).astype(BF16)
        qt_ref[s, r3:, :] = jnp.zeros((QK_PAD - r3, bm), BF16)

        k = jnp.dot(ckv_ref[...], wuk_ref[s], preferred_element_type=F32)
        k_ref[s, :, :QK_NOPE] = _rms(k, gkn_ref[...]).astype(BF16)
        k_ref[s, :, QK_NOPE:] = kr_ref[...]
        vt = jnp.dot(wuvt_ref[s], ckvt_ref[...], preferred_element_type=F32)
        vt_ref[s, :V_HEAD, :] = vt.astype(BF16)
        vt_ref[s, V_HEAD:, :] = jnp.ones((V_ROWS - V_HEAD, bm), BF16)


def _qkv(z, pos, inv, gql, gkvl, gqt, gkn, gkr, wuqt, wuk, wuvt, batch, seq, off_q, off_kv, *, bm=512, hp=4):
    h, _, ql = wuqt.shape
    kvl = wuk.shape[1]
    bm = _tile(seq, bm)
    nb = seq // bm
    t = batch * seq
    half = QK_ROPE // 2
    assert off_q % ql == 0 and off_kv % kvl == 0 and (off_kv + kvl) % LANES == 0 and h % hp == 0
    const = lambda i, hh: (0, 0)
    head = lambda i, hh: (hh, 0, 0)
    out_map = lambda i, hh: (i // nb, hh, i % nb, 0)
    out_map_t = lambda i, hh: (i // nb, hh, i % nb, 0, 0)
    invt = jnp.broadcast_to(inv[0, :half].reshape(half, 1), (half, bm))
    gqt = jnp.broadcast_to(gqt.reshape(QK_PAD, 1), (QK_PAD, bm))
    return pl.pallas_call(
        _qkv_kernel,
        grid=(t // bm, h // hp),
        in_specs=[
            pl.BlockSpec((bm, ql), lambda i, hh: (i, off_q // ql)),
            pl.BlockSpec((bm, kvl), lambda i, hh: (i, off_kv // kvl)),
            pl.BlockSpec((bm, LANES), lambda i, hh: (i, (off_kv + kvl) // LANES)),
            pl.BlockSpec((bm, 1), lambda i, hh: (i, 0)),
            pl.BlockSpec((1, bm), lambda i, hh: (0, i)),
            pl.BlockSpec((1, LANES), const),
            pl.BlockSpec((half, bm), const),
            pl.BlockSpec((1, ql), const),
            pl.BlockSpec((1, kvl), const),
            pl.BlockSpec((QK_PAD, bm), const),
            pl.BlockSpec((1, QK_NOPE), const),
            pl.BlockSpec((1, LANES), const),
            pl.BlockSpec((hp, QK_PAD, ql), head),
            pl.BlockSpec((hp, kvl, QK_NOPE), head),
            pl.BlockSpec((hp, V_HEAD, kvl), head),
        ],
        out_specs=[
            pl.BlockSpec((None, hp, None, QK_PAD, bm), out_map_t),
            pl.BlockSpec((None, hp, bm, QK_PAD), out_map),
            pl.BlockSpec((None, hp, None, V_ROWS, bm), out_map_t),
        ],
        out_shape=[
            jax.ShapeDtypeStruct((batch, h, nb, QK_PAD, bm), BF16),
            jax.ShapeDtypeStruct((batch, h, seq, QK_PAD), BF16),
            jax.ShapeDtypeStruct((batch, h, nb, V_ROWS, bm), BF16),
        ],
        scratch_shapes=[
            pltpu.VMEM((ql, bm), BF16),
            pltpu.VMEM((bm, kvl), BF16),
            pltpu.VMEM((kvl, bm), BF16),
            pltpu.VMEM((bm, LANES), BF16),
            pltpu.VMEM((half, bm), F32),
            pltpu.VMEM((half, bm), F32),
        ],
        compiler_params=_params("parallel", "arbitrary"),
        name="qkv_prep",
    )(z, z, z, pos.reshape(t, 1), pos.reshape(1, t), inv, invt, gql, gkvl, gqt, gkn, gkr, wuqt, wuk, wuvt)


def _flash_kernel(qt_ref, k_ref, vt_ref, *refs, blk, chunk, n_cast):
    cast_in, o_ref, cast_out = refs[:n_cast], refs[n_cast], refs[n_cast + 1:2 * n_cast + 1]
    sa_ref, sb_ref, m_ref, acc_ref = refs[2 * n_cast + 1:]
    _cast_block(cast_in, cast_out)

    qi = pl.program_id(2)
    nch = 2 * blk // chunk
    per_blk = blk // chunk
    m_ref[...] = jnp.full(m_ref.shape, -jnp.inf, F32)
    acc_ref[...] = jnp.zeros(acc_ref.shape, F32)

    def visibility(c, diag):
        if diag is None or c * chunk >= (diag + 1) * blk:
            return "all"
        return "none" if (c + 1) * chunk <= diag * blk else "some"

    def scores(kj, s_ref, diag=None):
        start = pl.multiple_of(kj * blk, blk)
        k = k_ref[pl.ds(start, blk), :]
        for c in range(nch):
            if visibility(c, diag) != "none":
                qt = qt_ref[c // per_blk, :, (c % per_blk) * chunk:(c % per_blk + 1) * chunk]
                s_ref[c] = jnp.dot(k, qt, preferred_element_type=F32)

    def update(kj, s_ref, diag=None):
        vt = vt_ref[kj]
        for c in range(nch):
            see = visibility(c, diag)
            if see == "none":
                continue
            cols = slice(c * chunk, (c + 1) * chunk)
            st = s_ref[c]
            if see == "some":
                key = lax.broadcasted_iota(jnp.int32, st.shape, 0) + diag * blk
                qry = lax.broadcasted_iota(jnp.int32, st.shape, 1) + c * chunk
                st = jnp.where(key <= qry, st, -jnp.inf)
            m_old = m_ref[:, cols]
            m_new = jnp.maximum(m_old, jnp.max(st, axis=0, keepdims=True))
            p = jnp.exp2(st - m_new).astype(BF16)
            alpha = jnp.exp2(m_old - m_new)
            acc_ref[:, cols] = alpha * acc_ref[:, cols] + jnp.dot(vt, p, preferred_element_type=F32)
            m_ref[:, cols] = m_new

    scores(0, sa_ref)

    def pair(t, carry):
        scores(2 * t + 1, sb_ref)
        update(2 * t, sa_ref)
        scores(2 * t + 2, sa_ref)
        update(2 * t + 1, sb_ref)
        return carry

    lax.fori_loop(0, qi, pair, 0)

    scores(2 * qi + 1, sb_ref, diag=1)
    update(2 * qi, sa_ref, diag=0)
    update(2 * qi + 1, sb_ref, diag=1)

    o_ref[...] = (acc_ref[:V_HEAD, :] / acc_ref[V_HEAD:V_HEAD + 1, :]).T.astype(o_ref.dtype)


def _flash(qt, k, vt, cast=(), layer=0, *, chunk=256):
    b, h, nb, _, blk = qt.shape
    s = nb * blk
    chunk = min(chunk, blk)
    assert nb % 2 == 0
    cast_in, cast_out = _cast_specs(cast, layer, b * h, nb // 2, lambda bi, hi, qi: (bi * h + hi, qi))
    return pl.pallas_call(
        functools.partial(_flash_kernel, blk=blk, chunk=chunk, n_cast=len(cast)),
        grid=(b, h, nb // 2),
        in_specs=[
            pl.BlockSpec((None, None, 2, QK_PAD, blk), lambda bi, hi, qi: (bi, hi, qi, 0, 0)),
            pl.BlockSpec((None, None, s, QK_PAD), lambda bi, hi, qi: (bi, hi, 0, 0)),
            pl.BlockSpec((None, None, nb, V_ROWS, blk), lambda bi, hi, qi: (bi, hi, 0, 0, 0)),
            *cast_in,
        ],
        out_specs=[pl.BlockSpec((None, 2 * blk, V_HEAD), lambda bi, hi, qi: (bi, qi, hi)), *cast_out],
        out_shape=[jax.ShapeDtypeStruct((b, s, h * V_HEAD), BF16)]
        + [jax.ShapeDtypeStruct(w.shape[1:], BF16) for w in cast],
        scratch_shapes=[
            pltpu.VMEM((2 * blk // chunk, blk, chunk), F32),
            pltpu.VMEM((2 * blk // chunk, blk, chunk), F32),
            pltpu.VMEM((1, 2 * blk), F32),
            pltpu.VMEM((V_ROWS, 2 * blk), F32),
        ],
        compiler_params=_params("parallel", "parallel", "arbitrary"),
        name="mla_flash",
    )(qt, k, vt, *cast)


def _merge_kernel(a_ref, b_ref, wa_ref, wb_ref, gp_ref, gpx_ref, gm_ref, gmx_ref, o_ref, *, shift):
    bn = o_ref.shape[1]

    def gate(lo_ref, hi_ref):
        g = jnp.concatenate([lo_ref[...], hi_ref[...]], axis=1)[:, shift:shift + bn]
        return jax.nn.sigmoid(g)

    pa = jnp.dot(a_ref[...], wa_ref[...], preferred_element_type=F32)
    pb = jnp.dot(b_ref[...], wb_ref[...], preferred_element_type=F32)
    o_ref[...] = (gate(gp_ref, gpx_ref) * pa + gate(gm_ref, gmx_ref) * pb).astype(BF16)


def _merge(a, b, wa, wb, z, gate_off, *, bm=1024, bn=512):
    t, pw = a.shape
    mw = b.shape[1]
    d = wa.shape[1]
    shift = gate_off % LANES
    base = gate_off - shift
    bm, bn = _tile(t, bm), _tile(math.gcd(d, base), bn)
    lanes_per_tile = bn // LANES
    lo = lambda off: (lambda i, j: (i, off // bn + j))
    hi = lambda off: (lambda i, j: (i, (off + bn) // LANES + j * lanes_per_tile))
    return pl.pallas_call(
        functools.partial(_merge_kernel, shift=shift),
        grid=(t // bm, d // bn),
        in_specs=[
            pl.BlockSpec((bm, pw), lambda i, j: (i, 0)),
            pl.BlockSpec((bm, mw), lambda i, j: (i, 0)),
            pl.BlockSpec((pw, bn), lambda i, j: (0, j)),
            pl.BlockSpec((mw, bn), lambda i, j: (0, j)),
            pl.BlockSpec((bm, bn), lo(base)),
            pl.BlockSpec((bm, LANES), hi(base)),
            pl.BlockSpec((bm, bn), lo(base + d)),
            pl.BlockSpec((bm, LANES), hi(base + d)),
        ],
        out_specs=pl.BlockSpec((bm, bn), lambda i, j: (i, j)),
        out_shape=jax.ShapeDtypeStruct((t, d), BF16),
        compiler_params=_params("parallel", "arbitrary"),
        name="branch_merge",
    )(a, b, wa, wb, z, z, z, z)


def _proj_residual_kernel(a_ref, w_ref, h_ref, o_ref):
    o_ref[...] = h_ref[...] + jnp.dot(a_ref[...], w_ref[...], preferred_element_type=F32)


def _proj_residual(a, w, h, *, bm=1024, bn=1024):
    t, kdim = a.shape
    d = w.shape[1]
    bm, bn = _tile(t, bm), _tile(d, bn)
    return pl.pallas_call(
        _proj_residual_kernel,
        grid=(t // bm, d // bn),
        in_specs=[
            pl.BlockSpec((bm, kdim), lambda i, j: (i, 0)),
            pl.BlockSpec((kdim, bn), lambda i, j: (0, j)),
            pl.BlockSpec((bm, bn), lambda i, j: (i, j)),
        ],
        out_specs=pl.BlockSpec((bm, bn), lambda i, j: (i, j)),
        out_shape=jax.ShapeDtypeStruct((t, d), F32),
        compiler_params=_params("parallel", "arbitrary"),
        name="out_proj",
    )(a, w, h)


def _mem_kv_kernel(mem_ref, gain_ref, w_ref, gk_ref, k_ref, v_ref):
    mn = _rms(mem_ref[...], gain_ref[...]).astype(BF16)
    kv = jnp.dot(mn, w_ref[...], preferred_element_type=F32)
    xw = X_HEADS * X_HEAD_DIM
    for hh in range(X_HEADS):
        cols = slice(hh * X_HEAD_DIM, (hh + 1) * X_HEAD_DIM)
        k_ref[:, cols] = _rms(kv[:, cols], gk_ref[...]).astype(BF16)
    v_ref[...] = kv[:, xw:].astype(BF16)


def _mem_kv(mem, gain, w_xkv, gk):
    b, m, d = mem.shape
    xw = X_HEADS * X_HEAD_DIM
    return pl.pallas_call(
        _mem_kv_kernel,
        grid=(b,),
        in_specs=[
            pl.BlockSpec((None, m, d), lambda bi: (bi, 0, 0)),
            pl.BlockSpec((1, d), lambda bi: (0, 0)),
            pl.BlockSpec((d, 2 * xw), lambda bi: (0, 0)),
            pl.BlockSpec((1, X_HEAD_DIM), lambda bi: (0, 0)),
        ],
        out_specs=[
            pl.BlockSpec((None, m, xw), lambda bi: (bi, 0, 0)),
            pl.BlockSpec((None, m, xw), lambda bi: (bi, 0, 0)),
        ],
        out_shape=[
            jax.ShapeDtypeStruct((b, m, xw), BF16),
            jax.ShapeDtypeStruct((b, m, xw), BF16),
        ],
        compiler_params=_params("parallel"),
        name="mem_kv",
    )(mem, gain.reshape(1, d), w_xkv, gk)


def _cross_kernel(h_ref, gain_ref, wq_ref, gq_ref, k_ref, v_ref, wo_ref, o_ref, xo_ref):
    h = h_ref[...]
    uq = _rms(h, gain_ref[...]).astype(BF16)
    xq = jnp.dot(uq, wq_ref[...], preferred_element_type=F32)
    scale = X_HEAD_DIM ** -0.5
    for hh in range(X_HEADS):
        cols = slice(hh * X_HEAD_DIM, (hh + 1) * X_HEAD_DIM)
        qh = _rms(xq[:, cols], gq_ref[...]).astype(BF16)
        s = lax.dot_general(qh, k_ref[:, cols], (((1,), (1,)), ((), ())),
                            preferred_element_type=F32) * scale
        p = jnp.exp(s - jnp.max(s, axis=-1, keepdims=True))
        p = (p / jnp.sum(p, axis=-1, keepdims=True)).astype(BF16)
        xo_ref[:, cols] = jnp.dot(p, v_ref[:, cols], preferred_element_type=F32).astype(BF16)
    o_ref[...] = h + jnp.dot(xo_ref[...], wo_ref[...], preferred_element_type=F32)


def _cross(h, gain, wq, gq, xk, xv, wo, seq, *, bm=256):
    t, d = h.shape
    m = xk.shape[1]
    xw = X_HEADS * X_HEAD_DIM
    bm = _tile(seq, bm)
    nb = seq // bm
    return pl.pallas_call(
        _cross_kernel,
        grid=(t // bm,),
        in_specs=[
            pl.BlockSpec((bm, d), lambda i: (i, 0)),
            pl.BlockSpec((1, d), lambda i: (0, 0)),
            pl.BlockSpec((d, xw), lambda i: (0, 0)),
            pl.BlockSpec((1, X_HEAD_DIM), lambda i: (0, 0)),
            pl.BlockSpec((None, m, xw), lambda i: (i // nb, 0, 0)),
            pl.BlockSpec((None, m, xw), lambda i: (i // nb, 0, 0)),
            pl.BlockSpec((xw, d), lambda i: (0, 0)),
        ],
        out_specs=pl.BlockSpec((bm, d), lambda i: (i, 0)),
        out_shape=jax.ShapeDtypeStruct((t, d), F32),
        scratch_shapes=[pltpu.VMEM((bm, xw), BF16)],
        compiler_params=_params("parallel"),
        name="cross_attn",
    )(h, gain.reshape(1, d), wq, gq, xk, xv, wo)


def _pad_lanes(v, width):
    return jnp.pad(v, (0, width - v.shape[0])).reshape(1, width)


def kernel(x, mem, positions, ffn1_norm, ffn1_w_gu, ffn1_w_down, mix_norm, w_in, w_pool, pool_scale,
           q_latent_norm, kv_latent_norm, w_uq, w_ukv, q_nope_norm, k_nope_norm, q_rope_norm,
           k_rope_norm, w_branch_pool, w_branch_mla, w_out, x_norm, mem_norm, w_xq, w_xkv, xq_norm,
           xk_norm, w_xo, ffn2_norm, ffn2_w_gu, ffn2_w_down):
    batch, seq, d = x.shape
    t = batch * seq
    depth = w_in.shape[0]
    pw = w_branch_pool.shape[1]
    ql = q_latent_norm.shape[1]
    kvl = kv_latent_norm.shape[1]
    heads = w_uq.shape[2] // QK_HEAD

    half = QK_ROPE // 2
    inv = 1.0 / (ROPE_THETA ** (jnp.arange(half, dtype=F32) * (2.0 / QK_ROPE)))
    inv = _pad_lanes(jnp.concatenate([inv, inv]), LANES)
    c2 = QK_HEAD ** -0.5 * math.log2(math.e)

    h = x.reshape(t, d)
    for l in range(depth):
        h = _ffn(h, ffn1_norm[l], ffn1_w_gu[l].astype(BF16), ffn1_w_down[l].astype(BF16))

        off_kv = pw + ql
        off_gate = off_kv + kvl + QK_ROPE
        z = _norm_matmul(h, mix_norm[l], w_in[l].astype(BF16), bn=IN_TILE, name="in_proj")

        a_out = _pool(z, w_pool[l].astype(BF16), pool_scale[l], batch, seq)

        wuq = jnp.pad(w_uq[l].reshape(ql, heads, QK_HEAD), ((0, 0), (0, 0), (0, QK_PAD - QK_HEAD)))
        wuqt = wuq.transpose(1, 2, 0).astype(BF16)
        wukv = w_ukv[l].reshape(kvl, heads, QK_NOPE + V_HEAD)
        wuk = wukv[:, :, :QK_NOPE].transpose(1, 0, 2).astype(BF16)
        wuvt = wukv[:, :, QK_NOPE:].transpose(1, 2, 0).astype(BF16)
        gqt = jnp.concatenate([q_nope_norm[l], q_rope_norm[l], jnp.zeros((QK_PAD - QK_HEAD,), F32)]) * c2
        qt, k, vt = _qkv(z, positions, inv,
                         q_latent_norm[l].reshape(1, ql), kv_latent_norm[l].reshape(1, kvl), gqt,
                         k_nope_norm[l].reshape(1, QK_NOPE), _pad_lanes(k_rope_norm[l], LANES),
                         wuqt, wuk, wuvt, batch, seq, pw, off_kv)
        later = (ffn2_w_gu, ffn2_w_down, w_out, w_branch_pool, w_branch_mla)
        b_out, w_gu2, w_down2, w_out_b, w_bp, w_bm = _flash(qt, k, vt, later, l)
        b_out = b_out.reshape(t, heads * V_HEAD)

        merged = _merge(a_out, b_out, w_bp, w_bm, z, off_gate)
        h = _proj_residual(merged, w_out_b, h)

        xk, xv = _mem_kv(mem, mem_norm[l], w_xkv[l].astype(BF16), xk_norm[l].reshape(1, X_HEAD_DIM))
        h = _cross(h, x_norm[l], w_xq[l].astype(BF16), xq_norm[l].reshape(1, X_HEAD_DIM), xk, xv,
                   w_xo[l].astype(BF16), seq)

        h = _ffn(h, ffn2_norm[l], w_gu2, w_down2)
    return h.reshape(batch, seq, d)
```

```python
import functools
import math

import jax
import jax.numpy as jnp
from jax import lax
from jax.experimental import pallas as pl
from jax.experimental.pallas import tpu as pltpu

F32 = jnp.float32
BF16 = jnp.bfloat16

POOL_WINDOWS = (2, 4, 8, 16)
POOL_HALO = 16
QK_NOPE = 128
QK_ROPE = 64
QK_HEAD = QK_NOPE + QK_ROPE
QK_PAD = 256
V_HEAD = 128
V_ROWS = V_HEAD + 16
X_HEADS = 4
X_HEAD_DIM = 128
ROPE_THETA = 10000.0
EPS = 1e-6
LANES = 128
BF16_ROWS = 16
VMEM_LIMIT = 62 * 1024 * 1024
NORM_ROWS = 256
IN_TILE = 1024


def _params(*sem):
    return pltpu.CompilerParams(dimension_semantics=sem, vmem_limit_bytes=VMEM_LIMIT)


def _tile(n, pref):
    if n <= pref:
        return n
    t = (pref // LANES) * LANES
    while t >= LANES:
        if n % t == 0:
            return t
        t -= LANES
    raise ValueError(f"no lane-aligned tile of {n} below {pref}")


def _ceil_to(x, m):
    return -(-x // m) * m


def _rms(x, gain):
    ms = jnp.mean(x * x, axis=-1, keepdims=True)
    return x * lax.rsqrt(ms + EPS) * gain


def _ffn_kernel(x_ref, gain_ref, wg_ref, wu_ref, wd_ref, o_ref, xn_ref):
    @pl.when(pl.program_id(1) == 0)
    def _():
        for r in range(0, x_ref.shape[0], NORM_ROWS):
            rows = slice(r, min(r + NORM_ROWS, x_ref.shape[0]))
            x = x_ref[rows, :]
            xn_ref[rows, :] = _rms(x, gain_ref[...]).astype(BF16)
            o_ref[rows, :] = x

    xn = xn_ref[...]
    g = jnp.dot(xn, wg_ref[...], preferred_element_type=F32)
    u = jnp.dot(xn, wu_ref[...], preferred_element_type=F32)
    hid = (g * jax.nn.sigmoid(g) * u * 0.5).astype(BF16)
    o_ref[...] += jnp.dot(hid, wd_ref[...], preferred_element_type=F32)


def _cast_specs(ws, layer, gi, gj, rc):
    ins, outs = [], []
    for w in ws:
        _, r, c = w.shape
        br = _ceil_to(pl.cdiv(r, gi), BF16_ROWS)
        bc = _ceil_to(pl.cdiv(c, gj), LANES)
        last = pl.cdiv(c, bc) - 1
        assert (gi - 1) * br < r <= gi * br and last < gj

        def block(*g, last=last):
            i, j = rc(*g)
            return i, jnp.minimum(j, last)

        ins.append(pl.BlockSpec((None, br, bc), lambda *g, block=block: (layer, *block(*g))))
        outs.append(pl.BlockSpec((br, bc), block))
    return ins, outs


def _cast_block(cast_in, cast_out):
    for src, dst in zip(cast_in, cast_out):
        dst[...] = src[...].astype(BF16)


def _ffn(x, gain, w_gu, w_down, *, bm=512, bf=256):
    t, d = x.shape
    f = w_down.shape[0]
    bm, bf = _tile(t, bm), _tile(f, bf)
    nf = f // bf
    return pl.pallas_call(
        _ffn_kernel,
        grid=(t // bm, nf),
        in_specs=[
            pl.BlockSpec((bm, d), lambda i, j: (i, 0)),
            pl.BlockSpec((1, d), lambda i, j: (0, 0)),
            pl.BlockSpec((d, bf), lambda i, j: (0, j)),
            pl.BlockSpec((d, bf), lambda i, j: (0, j + nf)),
            pl.BlockSpec((bf, d), lambda i, j: (j, 0)),
        ],
        out_specs=pl.BlockSpec((bm, d), lambda i, j: (i, 0)),
        out_shape=jax.ShapeDtypeStruct((t, d), F32),
        scratch_shapes=[pltpu.VMEM((bm, d), BF16)],
        compiler_params=_params("parallel", "arbitrary"),
        name="ffn",
    )(x, gain.reshape(1, d), w_gu, w_gu, w_down)


def _norm_matmul_kernel(x_ref, gain_ref, w_ref, o_ref, xn_ref):
    @pl.when(pl.program_id(1) == 0)
    def _():
        for r in range(0, x_ref.shape[0], NORM_ROWS):
            rows = slice(r, min(r + NORM_ROWS, x_ref.shape[0]))
            xn_ref[rows, :] = _rms(x_ref[rows, :], gain_ref[...]).astype(BF16)

    o_ref[...] = jnp.dot(xn_ref[...], w_ref[...], preferred_element_type=F32)


def _norm_matmul(x, gain, w, *, bm=1024, bn=1024, name):
    t, d = x.shape
    n = w.shape[1]
    bm, bn = _tile(t, bm), min(bn, n)
    return pl.pallas_call(
        _norm_matmul_kernel,
        grid=(t // bm, pl.cdiv(n, bn)),
        in_specs=[
            pl.BlockSpec((bm, d), lambda i, j: (i, 0), pipeline_mode=pl.Buffered(1)),
            pl.BlockSpec((1, d), lambda i, j: (0, 0)),
            pl.BlockSpec((d, bn), lambda i, j: (0, j)),
        ],
        out_specs=pl.BlockSpec((bm, bn), lambda i, j: (i, j)),
        out_shape=jax.ShapeDtypeStruct((t, n), F32),
        scratch_shapes=[pltpu.VMEM((bm, d), BF16)],
        compiler_params=_params("parallel", "arbitrary"),
        name=name,
    )(x, gain.reshape(1, d), w)


def _pool_kernel(zc_ref, zp_ref, wp_ref, sc_ref, o_ref, ext_ref, *, bm, c):
    i = pl.program_id(1)
    ext_ref[0:POOL_HALO, :] = jnp.where(i > 0, zp_ref[...], 0.0)
    ext_ref[POOL_HALO:POOL_HALO + bm, :] = zc_ref[...]
    t = i * bm + lax.broadcasted_iota(jnp.int32, (bm, 1), 0)
    for g, w in enumerate(POOL_WINDOWS):
        cols = slice(g * c, (g + 1) * c)
        x = zc_ref[:, cols]
        acc = x
        for k in range(1, w):
            acc = acc + ext_ref[POOL_HALO - k:POOL_HALO - k + bm, cols]
        cnt = jnp.minimum(t + 1, w).astype(F32)
        mixed = (acc / cnt - x).astype(BF16)
        y = jnp.dot(mixed, wp_ref[g], preferred_element_type=F32) * sc_ref[:, cols]
        o_ref[:, cols] = y.astype(BF16)


def _pool(z_main, w_pool, pool_scale, batch, seq, *, bm=512):
    g, c, _ = w_pool.shape
    pw = g * c
    bm = _tile(seq, bm)
    nb = seq // bm
    halo_blocks = bm // POOL_HALO
    return pl.pallas_call(
        functools.partial(_pool_kernel, bm=bm, c=c),
        grid=(batch, nb),
        in_specs=[
            pl.BlockSpec((bm, pw), lambda b, i: (b * nb + i, 0)),
            pl.BlockSpec((POOL_HALO, pw),
                         lambda b, i: (jnp.maximum((b * nb + i) * halo_blocks - 1, 0), 0)),
            pl.BlockSpec((g, c, c), lambda b, i: (0, 0, 0)),
            pl.BlockSpec((1, pw), lambda b, i: (0, 0)),
        ],
        out_specs=pl.BlockSpec((bm, pw), lambda b, i: (b * nb + i, 0)),
        out_shape=jax.ShapeDtypeStruct((batch * seq, pw), BF16),
        scratch_shapes=[pltpu.VMEM((POOL_HALO + bm, pw), F32)],
        compiler_params=_params("parallel", "parallel"),
        name="pool",
    )(z_main, z_main, w_pool, pool_scale.reshape(1, pw))


def _rope_tables(pos, inv):
    ang = pos.astype(F32) * inv
    cos, sin = jnp.cos(ang), jnp.sin(ang)
    lane = lax.broadcasted_iota(jnp.int32, ang.shape, 1)
    half = QK_ROPE // 2
    c = jnp.where(lane < QK_ROPE, cos, 0.0)
    sa = jnp.where(lane < half, -sin, 0.0)
    sb = jnp.where((lane >= half) & (lane < QK_ROPE), sin, 0.0)
    return c, sa, sb


def _rope_norm(r, gain, c, sa, sb):
    ms = jnp.sum(r * r, axis=-1, keepdims=True) * (1.0 / QK_ROPE)
    r = r * lax.rsqrt(ms + EPS) * gain
    half = QK_ROPE // 2
    return r * c + pltpu.roll(r, LANES - half, 1) * sa + pltpu.roll(r, half, 1) * sb


def _qkv_kernel(zq_ref, zkv_ref, zkr_ref, pos_ref, post_ref, inv_ref, invt_ref, gql_ref, gkvl_ref, gqt_ref,
                gkn_ref, gkr_ref, wuqt_ref, wuk_ref, wuvt_ref, qt_ref, k_ref, vt_ref,
                cqt_ref, ckv_ref, ckvt_ref, kr_ref, cost_ref, sint_ref):
    half = QK_ROPE // 2
    r1, r2, r3 = QK_NOPE, QK_NOPE + half, QK_HEAD

    @pl.when(pl.program_id(1) == 0)
    def _():
        cqt_ref[...] = _rms(zq_ref[...], gql_ref[...]).T.astype(BF16)
        ckv = _rms(zkv_ref[...], gkvl_ref[...])
        ckv_ref[...] = ckv.astype(BF16)
        ckvt_ref[...] = ckv.T.astype(BF16)
        c, sa, sb = _rope_tables(pos_ref[...], inv_ref[...])
        lane = lax.broadcasted_iota(jnp.int32, zkr_ref.shape, 1)
        zkr = jnp.where(lane < QK_ROPE, zkr_ref[...], 0.0)
        kr_ref[...] = _rope_norm(zkr, gkr_ref[...], c, sa, sb).astype(BF16)
        ang = post_ref[...].astype(F32) * invt_ref[...]
        cost_ref[...] = jnp.cos(ang)
        sint_ref[...] = jnp.sin(ang)

    bm = qt_ref.shape[-1]
    for s in range(qt_ref.shape[0]):
        q = jnp.dot(wuqt_ref[s], cqt_ref[...], preferred_element_type=F32)
        nope = q[:r1]
        rs = lax.rsqrt(jnp.mean(nope * nope, axis=0, keepdims=True) + EPS)
        qt_ref[s, :r1, :] = (nope * rs * gqt_ref[:r1, :]).astype(BF16)
        x1, x2 = q[r1:r2], q[r2:r3]
        ms = (jnp.sum(x1 * x1, axis=0, keepdims=True)
              + jnp.sum(x2 * x2, axis=0, keepdims=True)) * (1.0 / QK_ROPE)
        rs = lax.rsqrt(ms + EPS)
        y1, y2 = x1 * rs * gqt_ref[r1:r2, :], x2 * rs * gqt_ref[r2:r3, :]
        cos, sin = cost_ref[...], sint_ref[...]
        qt_ref[s, r1:r2, :] = (y1 * cos - y2 * sin).astype(BF16)
        qt_ref[s, r2:r3, :] = (y2 * cos + y1 * sin).astype(BF16)
        qt_ref[s, r3:, :] = jnp.zeros((QK_PAD - r3, bm), BF16)

        k = jnp.dot(ckv_ref[...], wuk_ref[s], preferred_element_type=F32)
        k_ref[s, :, :QK_NOPE] = _rms(k, gkn_ref[...]).astype(BF16)
        k_ref[s, :, QK_NOPE:] = kr_ref[...]
        vt = jnp.dot(wuvt_ref[s], ckvt_ref[...], preferred_element_type=F32)
        vt_ref[s, :V_HEAD, :] = vt.astype(BF16)
        vt_ref[s, V_HEAD:, :] = jnp.ones((V_ROWS - V_HEAD, bm), BF16)


def _qkv(z, pos, inv, gql, gkvl, gqt, gkn, gkr, wuqt, wuk, wuvt, batch, seq, off_q, off_kv, *, bm=512, hp=8):
    h, _, ql = wuqt.shape
    hp = min(hp, h)
    kvl = wuk.shape[1]
    bm = _tile(seq, bm)
    nb = seq // bm
    t = batch * seq
    half = QK_ROPE // 2
    assert off_q % ql == 0 and off_kv % kvl == 0 and (off_kv + kvl) % LANES == 0 and h % hp == 0
    const = lambda i, hh: (0, 0)
    head = lambda i, hh: (hh, 0, 0)
    out_map = lambda i, hh: (i // nb, hh, i % nb, 0)
    out_map_t = lambda i, hh: (i // nb, hh, i % nb, 0, 0)
    invt = jnp.broadcast_to(inv[0, :half].reshape(half, 1), (half, bm))
    gqt = jnp.broadcast_to(gqt.reshape(QK_PAD, 1), (QK_PAD, bm))
    return pl.pallas_call(
        _qkv_kernel,
        grid=(t // bm, h // hp),
        in_specs=[
            pl.BlockSpec((bm, ql), lambda i, hh: (i, off_q // ql)),
            pl.BlockSpec((bm, kvl), lambda i, hh: (i, off_kv // kvl)),
            pl.BlockSpec((bm, LANES), lambda i, hh: (i, (off_kv + kvl) // LANES)),
            pl.BlockSpec((bm, 1), lambda i, hh: (i, 0)),
            pl.BlockSpec((1, bm), lambda i, hh: (0, i)),
            pl.BlockSpec((1, LANES), const),
            pl.BlockSpec((half, bm), const),
            pl.BlockSpec((1, ql), const),
            pl.BlockSpec((1, kvl), const),
            pl.BlockSpec((QK_PAD, bm), const),
            pl.BlockSpec((1, QK_NOPE), const),
            pl.BlockSpec((1, LANES), const),
            pl.BlockSpec((hp, QK_PAD, ql), head),
            pl.BlockSpec((hp, kvl, QK_NOPE), head),
            pl.BlockSpec((hp, V_HEAD, kvl), head),
        ],
        out_specs=[
            pl.BlockSpec((None, hp, None, QK_PAD, bm), out_map_t),
            pl.BlockSpec((None, hp, bm, QK_PAD), out_map),
            pl.BlockSpec((None, hp, None, V_ROWS, bm), out_map_t),
        ],
        out_shape=[
            jax.ShapeDtypeStruct((batch, h, nb, QK_PAD, bm), BF16),
            jax.ShapeDtypeStruct((batch, h, seq, QK_PAD), BF16),
            jax.ShapeDtypeStruct((batch, h, nb, V_ROWS, bm), BF16),
        ],
        scratch_shapes=[
            pltpu.VMEM((ql, bm), BF16),
            pltpu.VMEM((bm, kvl), BF16),
            pltpu.VMEM((kvl, bm), BF16),
            pltpu.VMEM((bm, LANES), BF16),
            pltpu.VMEM((half, bm), F32),
            pltpu.VMEM((half, bm), F32),
        ],
        compiler_params=_params("parallel", "arbitrary"),
        name="qkv_prep",
    )(z, z, z, pos.reshape(t, 1), pos.reshape(1, t), inv, invt, gql, gkvl, gqt, gkn, gkr, wuqt, wuk, wuvt)


def _flash_kernel(qt_ref, k_ref, vt_ref, *refs, blk, chunk, n_cast):
    cast_in, o_ref, cast_out = refs[:n_cast], refs[n_cast], refs[n_cast + 1:2 * n_cast + 1]
    sa_ref, sb_ref, m_ref, acc_ref = refs[2 * n_cast + 1:]
    _cast_block(cast_in, cast_out)

    qi = pl.program_id(2)
    nch = 2 * blk // chunk
    per_blk = blk // chunk
    m_ref[...] = jnp.full(m_ref.shape, -jnp.inf, F32)
    acc_ref[...] = jnp.zeros(acc_ref.shape, F32)

    def visibility(c, diag):
        if diag is None or c * chunk >= (diag + 1) * blk:
            return "all"
        return "none" if (c + 1) * chunk <= diag * blk else "some"

    def scores(kj, s_ref, diag=None):
        start = pl.multiple_of(kj * blk, blk)
        k = k_ref[pl.ds(start, blk), :]
        for c in range(nch):
            if visibility(c, diag) != "none":
                qt = qt_ref[c // per_blk, :, (c % per_blk) * chunk:(c % per_blk + 1) * chunk]
                s_ref[c] = jnp.dot(k, qt, preferred_element_type=F32)

    def update(kj, s_ref, diag=None):
        vt = vt_ref[kj]
        for c in range(nch):
            see = visibility(c, diag)
            if see == "none":
                continue
            cols = slice(c * chunk, (c + 1) * chunk)
            st = s_ref[c]
            if see == "some":
                key = lax.broadcasted_iota(jnp.int32, st.shape, 0) + diag * blk
                qry = lax.broadcasted_iota(jnp.int32, st.shape, 1) + c * chunk
                st = jnp.where(key <= qry, st, -jnp.inf)
            m_old = m_ref[:, cols]
            m_new = jnp.maximum(m_old, jnp.max(st, axis=0, keepdims=True))
            p = jnp.exp2(st - m_new).astype(BF16)
            alpha = jnp.exp2(m_old - m_new)
            acc_ref[:, cols] = alpha * acc_ref[:, cols] + jnp.dot(vt, p, preferred_element_type=F32)
            m_ref[:, cols] = m_new

    scores(0, sa_ref)

    def pair(t):
        scores(2 * t + 1, sb_ref)
        update(2 * t, sa_ref)
        scores(2 * t + 2, sa_ref)
        update(2 * t + 1, sb_ref)

    def two_pairs(u, carry):
        pair(2 * u)
        pair(2 * u + 1)
        return carry

    lax.fori_loop(0, qi // 2, two_pairs, 0)

    @pl.when(qi % 2 == 1)
    def _():
        pair(qi - 1)

    scores(2 * qi + 1, sb_ref, diag=1)
    update(2 * qi, sa_ref, diag=0)
    update(2 * qi + 1, sb_ref, diag=1)

    o_ref[...] = (acc_ref[:V_HEAD, :] / acc_ref[V_HEAD:V_HEAD + 1, :]).T.astype(o_ref.dtype)


def _flash(qt, k, vt, cast=(), layer=0, *, chunk=256):
    b, h, nb, _, blk = qt.shape
    s = nb * blk
    chunk = min(chunk, blk)
    assert nb % 2 == 0
    cast_in, cast_out = _cast_specs(cast, layer, b * h, nb // 2, lambda bi, hi, qi: (bi * h + hi, qi))
    return pl.pallas_call(
        functools.partial(_flash_kernel, blk=blk, chunk=chunk, n_cast=len(cast)),
        grid=(b, h, nb // 2),
        in_specs=[
            pl.BlockSpec((None, None, 2, QK_PAD, blk), lambda bi, hi, qi: (bi, hi, qi, 0, 0)),
            pl.BlockSpec((None, None, s, QK_PAD), lambda bi, hi, qi: (bi, hi, 0, 0)),
            pl.BlockSpec((None, None, nb, V_ROWS, blk), lambda bi, hi, qi: (bi, hi, 0, 0, 0)),
            *cast_in,
        ],
        out_specs=[pl.BlockSpec((None, 2 * blk, V_HEAD), lambda bi, hi, qi: (bi, qi, hi)), *cast_out],
        out_shape=[jax.ShapeDtypeStruct((b, s, h * V_HEAD), BF16)]
        + [jax.ShapeDtypeStruct(w.shape[1:], BF16) for w in cast],
        scratch_shapes=[
            pltpu.VMEM((2 * blk // chunk, blk, chunk), F32),
            pltpu.VMEM((2 * blk // chunk, blk, chunk), F32),
            pltpu.VMEM((1, 2 * blk), F32),
            pltpu.VMEM((V_ROWS, 2 * blk), F32),
        ],
        compiler_params=_params("parallel", "parallel", "arbitrary"),
        name="mla_flash",
    )(qt, k, vt, *cast)


def _merge_kernel(a_ref, b_ref, wa_ref, wb_ref, gp_ref, gpx_ref, gm_ref, gmx_ref, o_ref, *, shift):
    bn = o_ref.shape[1]

    def gate(lo_ref, hi_ref):
        g = jnp.concatenate([lo_ref[...], hi_ref[...]], axis=1)[:, shift:shift + bn]
        return jax.nn.sigmoid(g)

    pa = jnp.dot(a_ref[...], wa_ref[...], preferred_element_type=F32)
    pb = jnp.dot(b_ref[...], wb_ref[...], preferred_element_type=F32)
    o_ref[...] = (gate(gp_ref, gpx_ref) * pa + gate(gm_ref, gmx_ref) * pb).astype(BF16)


def _merge(a, b, wa, wb, z, gate_off, *, bm=1024, bn=512):
    t, pw = a.shape
    mw = b.shape[1]
    d = wa.shape[1]
    shift = gate_off % LANES
    base = gate_off - shift
    bm, bn = _tile(t, bm), _tile(math.gcd(d, base), bn)
    lanes_per_tile = bn // LANES
    lo = lambda off: (lambda i, j: (i, off // bn + j))
    hi = lambda off: (lambda i, j: (i, (off + bn) // LANES + j * lanes_per_tile))
    return pl.pallas_call(
        functools.partial(_merge_kernel, shift=shift),
        grid=(t // bm, d // bn),
        in_specs=[
            pl.BlockSpec((bm, pw), lambda i, j: (i, 0)),
            pl.BlockSpec((bm, mw), lambda i, j: (i, 0)),
            pl.BlockSpec((pw, bn), lambda i, j: (0, j)),
            pl.BlockSpec((mw, bn), lambda i, j: (0, j)),
            pl.BlockSpec((bm, bn), lo(base)),
            pl.BlockSpec((bm, LANES), hi(base)),
            pl.BlockSpec((bm, bn), lo(base + d)),
            pl.BlockSpec((bm, LANES), hi(base + d)),
        ],
        out_specs=pl.BlockSpec((bm, bn), lambda i, j: (i, j)),
        out_shape=jax.ShapeDtypeStruct((t, d), BF16),
        compiler_params=_params("parallel", "arbitrary"),
        name="branch_merge",
    )(a, b, wa, wb, z, z, z, z)


def _proj_residual_kernel(a_ref, w_ref, h_ref, o_ref):
    o_ref[...] = h_ref[...] + jnp.dot(a_ref[...], w_ref[...], preferred_element_type=F32)


def _proj_residual(a, w, h, *, bm=1024, bn=1024):
    t, kdim = a.shape
    d = w.shape[1]
    bm, bn = _tile(t, bm), _tile(d, bn)
    return pl.pallas_call(
        _proj_residual_kernel,
        grid=(t // bm, d // bn),
        in_specs=[
            pl.BlockSpec((bm, kdim), lambda i, j: (i, 0)),
            pl.BlockSpec((kdim, bn), lambda i, j: (0, j)),
            pl.BlockSpec((bm, bn), lambda i, j: (i, j)),
        ],
        out_specs=pl.BlockSpec((bm, bn), lambda i, j: (i, j)),
        out_shape=jax.ShapeDtypeStruct((t, d), F32),
        compiler_params=_params("parallel", "arbitrary"),
        name="out_proj",
    )(a, w, h)


def _mem_kv_kernel(mem_ref, gain_ref, w_ref, gk_ref, k_ref, v_ref):
    mn = _rms(mem_ref[...], gain_ref[...]).astype(BF16)
    kv = jnp.dot(mn, w_ref[...], preferred_element_type=F32)
    xw = X_HEADS * X_HEAD_DIM
    for hh in range(X_HEADS):
        cols = slice(hh * X_HEAD_DIM, (hh + 1) * X_HEAD_DIM)
        k_ref[:, cols] = _rms(kv[:, cols], gk_ref[...]).astype(BF16)
    v_ref[...] = kv[:, xw:].astype(BF16)


def _mem_kv(mem, gain, w_xkv, gk):
    b, m, d = mem.shape
    xw = X_HEADS * X_HEAD_DIM
    return pl.pallas_call(
        _mem_kv_kernel,
        grid=(b,),
        in_specs=[
            pl.BlockSpec((None, m, d), lambda bi: (bi, 0, 0)),
            pl.BlockSpec((1, d), lambda bi: (0, 0)),
            pl.BlockSpec((d, 2 * xw), lambda bi: (0, 0)),
            pl.BlockSpec((1, X_HEAD_DIM), lambda bi: (0, 0)),
        ],
        out_specs=[
            pl.BlockSpec((None, m, xw), lambda bi: (bi, 0, 0)),
            pl.BlockSpec((None, m, xw), lambda bi: (bi, 0, 0)),
        ],
        out_shape=[
            jax.ShapeDtypeStruct((b, m, xw), BF16),
            jax.ShapeDtypeStruct((b, m, xw), BF16),
        ],
        compiler_params=_params("parallel"),
        name="mem_kv",
    )(mem, gain.reshape(1, d), w_xkv, gk)


def _cross_kernel(h_ref, gain_ref, wq_ref, gq_ref, k_ref, v_ref, wo_ref, o_ref, xo_ref):
    h = h_ref[...]
    uq = _rms(h, gain_ref[...]).astype(BF16)
    xq = jnp.dot(uq, wq_ref[...], preferred_element_type=F32)
    scale = X_HEAD_DIM ** -0.5
    for hh in range(X_HEADS):
        cols = slice(hh * X_HEAD_DIM, (hh + 1) * X_HEAD_DIM)
        qh = _rms(xq[:, cols], gq_ref[...]).astype(BF16)
        s = lax.dot_general(qh, k_ref[:, cols], (((1,), (1,)), ((), ())),
                            preferred_element_type=F32) * scale
        p = jnp.exp(s - jnp.max(s, axis=-1, keepdims=True))
        p = (p / jnp.sum(p, axis=-1, keepdims=True)).astype(BF16)
        xo_ref[:, cols] = jnp.dot(p, v_ref[:, cols], preferred_element_type=F32).astype(BF16)
    o_ref[...] = h + jnp.dot(xo_ref[...], wo_ref[...], preferred_element_type=F32)


def _cross(h, gain, wq, gq, xk, xv, wo, seq, *, bm=256):
    t, d = h.shape
    m = xk.shape[1]
    xw = X_HEADS * X_HEAD_DIM
    bm = _tile(seq, bm)
    nb = seq // bm
    return pl.pallas_call(
        _cross_kernel,
        grid=(t // bm,),
        in_specs=[
            pl.BlockSpec((bm, d), lambda i: (i, 0)),
            pl.BlockSpec((1, d), lambda i: (0, 0)),
            pl.BlockSpec((d, xw), lambda i: (0, 0)),
            pl.BlockSpec((1, X_HEAD_DIM), lambda i: (0, 0)),
            pl.BlockSpec((None, m, xw), lambda i: (i // nb, 0, 0)),
            pl.BlockSpec((None, m, xw), lambda i: (i // nb, 0, 0)),
            pl.BlockSpec((xw, d), lambda i: (0, 0)),
        ],
        out_specs=pl.BlockSpec((bm, d), lambda i: (i, 0)),
        out_shape=jax.ShapeDtypeStruct((t, d), F32),
        scratch_shapes=[pltpu.VMEM((bm, xw), BF16)],
        compiler_params=_params("parallel"),
        name="cross_attn",
    )(h, gain.reshape(1, d), wq, gq, xk, xv, wo)


def _pad_lanes(v, width):
    return jnp.pad(v, (0, width - v.shape[0])).reshape(1, width)


def kernel(x, mem, positions, ffn1_norm, ffn1_w_gu, ffn1_w_down, mix_norm, w_in, w_pool, pool_scale,
           q_latent_norm, kv_latent_norm, w_uq, w_ukv, q_nope_norm, k_nope_norm, q_rope_norm,
           k_rope_norm, w_branch_pool, w_branch_mla, w_out, x_norm, mem_norm, w_xq, w_xkv, xq_norm,
           xk_norm, w_xo, ffn2_norm, ffn2_w_gu, ffn2_w_down):
    batch, seq, d = x.shape
    t = batch * seq
    depth = w_in.shape[0]
    pw = w_branch_pool.shape[1]
    ql = q_latent_norm.shape[1]
    kvl = kv_latent_norm.shape[1]
    heads = w_uq.shape[2] // QK_HEAD

    half = QK_ROPE // 2
    inv = 1.0 / (ROPE_THETA ** (jnp.arange(half, dtype=F32) * (2.0 / QK_ROPE)))
    inv = _pad_lanes(jnp.concatenate([inv, inv]), LANES)
    c2 = QK_HEAD ** -0.5 * math.log2(math.e)

    h = x.reshape(t, d)
    for l in range(depth):
        h = _ffn(h, ffn1_norm[l], ffn1_w_gu[l].astype(BF16), ffn1_w_down[l].astype(BF16))

        off_kv = pw + ql
        off_gate = off_kv + kvl + QK_ROPE
        z = _norm_matmul(h, mix_norm[l], w_in[l].astype(BF16), bn=IN_TILE, name="in_proj")

        a_out = _pool(z, w_pool[l].astype(BF16), pool_scale[l], batch, seq)

        wuq = jnp.pad(w_uq[l].reshape(ql, heads, QK_HEAD), ((0, 0), (0, 0), (0, QK_PAD - QK_HEAD)))
        wuqt = wuq.transpose(1, 2, 0).astype(BF16)
        wukv = w_ukv[l].reshape(kvl, heads, QK_NOPE + V_HEAD)
        wuk = wukv[:, :, :QK_NOPE].transpose(1, 0, 2).astype(BF16)
        wuvt = wukv[:, :, QK_NOPE:].transpose(1, 2, 0).astype(BF16)
        gqt = jnp.concatenate([q_nope_norm[l], q_rope_norm[l], jnp.zeros((QK_PAD - QK_HEAD,), F32)]) * c2
        qt, k, vt = _qkv(z, positions, inv,
                         q_latent_norm[l].reshape(1, ql), kv_latent_norm[l].reshape(1, kvl), gqt,
                         k_nope_norm[l].reshape(1, QK_NOPE), _pad_lanes(k_rope_norm[l], LANES),
                         wuqt, wuk, wuvt, batch, seq, pw, off_kv)
        later = (ffn2_w_gu, ffn2_w_down, w_out, w_branch_pool, w_branch_mla)
        b_out, w_gu2, w_down2, w_out_b, w_bp, w_bm = _flash(qt, k, vt, later, l)
        b_out = b_out.reshape(t, heads * V_HEAD)

        merged = _merge(a_out, b_out, w_bp, w_bm, z, off_gate)
        h = _proj_residual(merged, w_out_b, h)

        xk, xv = _mem_kv(mem, mem_norm[l], w_xkv[l].astype(BF16), xk_norm[l].reshape(1, X_HEAD_DIM))
        h = _cross(h, x_norm[l], w_xq[l].astype(BF16), xq_norm[l].reshape(1, X_HEAD_DIM), xk, xv,
                   w_xo[l].astype(BF16), seq)

        h = _ffn(h, ffn2_norm[l], w_gu2, w_down2)
    return h.reshape(batch, seq, d)
```

```python
import functools
import math

import jax
import jax.numpy as jnp
from jax import lax
from jax.experimental import pallas as pl
from jax.experimental.pallas import tpu as pltpu

F32 = jnp.float32
BF16 = jnp.bfloat16

POOL_WINDOWS = (2, 4, 8, 16)
POOL_HALO = 16
QK_NOPE = 128
QK_ROPE = 64
QK_HEAD = QK_NOPE + QK_ROPE
QK_PAD = 256
V_HEAD = 128
V_ROWS = V_HEAD + 16
X_HEADS = 4
X_HEAD_DIM = 128
ROPE_THETA = 10000.0
EPS = 1e-6
LANES = 128
BF16_ROWS = 16
VMEM_LIMIT = 62 * 1024 * 1024
NORM_ROWS = 256
PAIRS_PER_TRIP = 4
IN_TILE = 1024


def _params(*sem):
    return pltpu.CompilerParams(dimension_semantics=sem, vmem_limit_bytes=VMEM_LIMIT)


def _tile(n, pref):
    if n <= pref:
        return n
    t = (pref // LANES) * LANES
    while t >= LANES:
        if n % t == 0:
            return t
        t -= LANES
    raise ValueError(f"no lane-aligned tile of {n} below {pref}")


def _ceil_to(x, m):
    return -(-x // m) * m


def _rms(x, gain):
    ms = jnp.mean(x * x, axis=-1, keepdims=True)
    return x * lax.rsqrt(ms + EPS) * gain


def _ffn_kernel(x_ref, gain_ref, wg_ref, wu_ref, wd_ref, o_ref, xn_ref):
    @pl.when(pl.program_id(1) == 0)
    def _():
        for r in range(0, x_ref.shape[0], NORM_ROWS):
            rows = slice(r, min(r + NORM_ROWS, x_ref.shape[0]))
            x = x_ref[rows, :]
            xn_ref[rows, :] = _rms(x, gain_ref[...]).astype(BF16)
            o_ref[rows, :] = x

    xn = xn_ref[...]
    g = jnp.dot(xn, wg_ref[...], preferred_element_type=F32)
    u = jnp.dot(xn, wu_ref[...], preferred_element_type=F32)
    hid = (g * jax.nn.sigmoid(g) * u * 0.5).astype(BF16)
    o_ref[...] += jnp.dot(hid, wd_ref[...], preferred_element_type=F32)


def _cast_specs(ws, layer, gi, gj, rc):
    ins, outs = [], []
    for w in ws:
        _, r, c = w.shape
        br = _ceil_to(pl.cdiv(r, gi), BF16_ROWS)
        bc = _ceil_to(pl.cdiv(c, gj), LANES)
        last = pl.cdiv(c, bc) - 1
        assert (gi - 1) * br < r <= gi * br and last < gj

        def block(*g, last=last):
            i, j = rc(*g)
            return i, jnp.minimum(j, last)

        ins.append(pl.BlockSpec((None, br, bc), lambda *g, block=block: (layer, *block(*g))))
        outs.append(pl.BlockSpec((br, bc), block))
    return ins, outs


def _cast_block(cast_in, cast_out):
    for src, dst in zip(cast_in, cast_out):
        dst[...] = src[...].astype(BF16)


def _ffn(x, gain, w_gu, w_down, *, bm=512, bf=256):
    t, d = x.shape
    f = w_down.shape[0]
    bm, bf = _tile(t, bm), _tile(f, bf)
    nf = f // bf
    return pl.pallas_call(
        _ffn_kernel,
        grid=(t // bm, nf),
        in_specs=[
            pl.BlockSpec((bm, d), lambda i, j: (i, 0)),
            pl.BlockSpec((1, d), lambda i, j: (0, 0)),
            pl.BlockSpec((d, bf), lambda i, j: (0, j)),
            pl.BlockSpec((d, bf), lambda i, j: (0, j + nf)),
            pl.BlockSpec((bf, d), lambda i, j: (j, 0)),
        ],
        out_specs=pl.BlockSpec((bm, d), lambda i, j: (i, 0)),
        out_shape=jax.ShapeDtypeStruct((t, d), F32),
        scratch_shapes=[pltpu.VMEM((bm, d), BF16)],
        compiler_params=_params("parallel", "arbitrary"),
        name="ffn",
    )(x, gain.reshape(1, d), w_gu, w_gu, w_down)


def _norm_matmul_kernel(x_ref, gain_ref, w_ref, o_ref, xn_ref):
    @pl.when(pl.program_id(1) == 0)
    def _():
        for r in range(0, x_ref.shape[0], NORM_ROWS):
            rows = slice(r, min(r + NORM_ROWS, x_ref.shape[0]))
            xn_ref[rows, :] = _rms(x_ref[rows, :], gain_ref[...]).astype(BF16)

    o_ref[...] = jnp.dot(xn_ref[...], w_ref[...], preferred_element_type=F32)


def _norm_matmul(x, gain, w, *, bm=1024, bn=1024, name):
    t, d = x.shape
    n = w.shape[1]
    bm, bn = _tile(t, bm), min(bn, n)
    return pl.pallas_call(
        _norm_matmul_kernel,
        grid=(t // bm, pl.cdiv(n, bn)),
        in_specs=[
            pl.BlockSpec((bm, d), lambda i, j: (i, 0), pipeline_mode=pl.Buffered(1)),
            pl.BlockSpec((1, d), lambda i, j: (0, 0)),
            pl.BlockSpec((d, bn), lambda i, j: (0, j)),
        ],
        out_specs=pl.BlockSpec((bm, bn), lambda i, j: (i, j)),
        out_shape=jax.ShapeDtypeStruct((t, n), F32),
        scratch_shapes=[pltpu.VMEM((bm, d), BF16)],
        compiler_params=_params("parallel", "arbitrary"),
        name=name,
    )(x, gain.reshape(1, d), w)


def _pool_kernel(zc_ref, zp_ref, wp_ref, sc_ref, o_ref, ext_ref, *, bm, c):
    i = pl.program_id(1)
    ext_ref[0:POOL_HALO, :] = jnp.where(i > 0, zp_ref[...], 0.0)
    ext_ref[POOL_HALO:POOL_HALO + bm, :] = zc_ref[...]
    t = i * bm + lax.broadcasted_iota(jnp.int32, (bm, 1), 0)
    for g, w in enumerate(POOL_WINDOWS):
        cols = slice(g * c, (g + 1) * c)
        x = zc_ref[:, cols]
        acc = x
        for k in range(1, w):
            acc = acc + ext_ref[POOL_HALO - k:POOL_HALO - k + bm, cols]
        cnt = jnp.minimum(t + 1, w).astype(F32)
        mixed = (acc / cnt - x).astype(BF16)
        y = jnp.dot(mixed, wp_ref[g], preferred_element_type=F32) * sc_ref[:, cols]
        o_ref[:, cols] = y.astype(BF16)


def _pool(z_main, w_pool, pool_scale, batch, seq, *, bm=512):
    g, c, _ = w_pool.shape
    pw = g * c
    bm = _tile(seq, bm)
    nb = seq // bm
    halo_blocks = bm // POOL_HALO
    return pl.pallas_call(
        functools.partial(_pool_kernel, bm=bm, c=c),
        grid=(batch, nb),
        in_specs=[
            pl.BlockSpec((bm, pw), lambda b, i: (b * nb + i, 0)),
            pl.BlockSpec((POOL_HALO, pw),
                         lambda b, i: (jnp.maximum((b * nb + i) * halo_blocks - 1, 0), 0)),
            pl.BlockSpec((g, c, c), lambda b, i: (0, 0, 0)),
            pl.BlockSpec((1, pw), lambda b, i: (0, 0)),
        ],
        out_specs=pl.BlockSpec((bm, pw), lambda b, i: (b * nb + i, 0)),
        out_shape=jax.ShapeDtypeStruct((batch * seq, pw), BF16),
        scratch_shapes=[pltpu.VMEM((POOL_HALO + bm, pw), F32)],
        compiler_params=_params("parallel", "parallel"),
        name="pool",
    )(z_main, z_main, w_pool, pool_scale.reshape(1, pw))


def _rope_tables(pos, inv):
    ang = pos.astype(F32) * inv
    cos, sin = jnp.cos(ang), jnp.sin(ang)
    lane = lax.broadcasted_iota(jnp.int32, ang.shape, 1)
    half = QK_ROPE // 2
    c = jnp.where(lane < QK_ROPE, cos, 0.0)
    sa = jnp.where(lane < half, -sin, 0.0)
    sb = jnp.where((lane >= half) & (lane < QK_ROPE), sin, 0.0)
    return c, sa, sb


def _rope_norm(r, gain, c, sa, sb):
    ms = jnp.sum(r * r, axis=-1, keepdims=True) * (1.0 / QK_ROPE)
    r = r * lax.rsqrt(ms + EPS) * gain
    half = QK_ROPE // 2
    return r * c + pltpu.roll(r, LANES - half, 1) * sa + pltpu.roll(r, half, 1) * sb


def _qkv_kernel(zq_ref, zkv_ref, zkr_ref, pos_ref, post_ref, inv_ref, invt_ref, gql_ref, gkvl_ref, gqt_ref,
                gkn_ref, gkr_ref, wuqt_ref, wuk_ref, wuvt_ref, qt_ref, k_ref, vt_ref,
                cqt_ref, ckv_ref, ckvt_ref, kr_ref, cost_ref, sint_ref):
    half = QK_ROPE // 2
    r1, r2, r3 = QK_NOPE, QK_NOPE + half, QK_HEAD

    @pl.when(pl.program_id(1) == 0)
    def _():
        cqt_ref[...] = _rms(zq_ref[...], gql_ref[...]).T.astype(BF16)
        ckv = _rms(zkv_ref[...], gkvl_ref[...])
        ckv_ref[...] = ckv.astype(BF16)
        ckvt_ref[...] = ckv.T.astype(BF16)
        c, sa, sb = _rope_tables(pos_ref[...], inv_ref[...])
        lane = lax.broadcasted_iota(jnp.int32, zkr_ref.shape, 1)
        zkr = jnp.where(lane < QK_ROPE, zkr_ref[...], 0.0)
        kr_ref[...] = _rope_norm(zkr, gkr_ref[...], c, sa, sb).astype(BF16)
        ang = post_ref[...].astype(F32) * invt_ref[...]
        cost_ref[...] = jnp.cos(ang)
        sint_ref[...] = jnp.sin(ang)

    bm = qt_ref.shape[-1]
    for s in range(qt_ref.shape[0]):
        q = jnp.dot(wuqt_ref[s], cqt_ref[...], preferred_element_type=F32)
        nope = q[:r1]
        rs = lax.rsqrt(jnp.mean(nope * nope, axis=0, keepdims=True) + EPS)
        qt_ref[s, :r1, :] = (nope * rs * gqt_ref[:r1, :]).astype(BF16)
        x1, x2 = q[r1:r2], q[r2:r3]
        ms = (jnp.sum(x1 * x1, axis=0, keepdims=True)
              + jnp.sum(x2 * x2, axis=0, keepdims=True)) * (1.0 / QK_ROPE)
        rs = lax.rsqrt(ms + EPS)
        y1, y2 = x1 * rs * gqt_ref[r1:r2, :], x2 * rs * gqt_ref[r2:r3, :]
        cos, sin = cost_ref[...], sint_ref[...]
        qt_ref[s, r1:r2, :] = (y1 * cos - y2 * sin).astype(BF16)
        qt_ref[s, r2:r3, :] = (y2 * cos + y1 * sin).astype(BF16)
        qt_ref[s, r3:, :] = jnp.zeros((QK_PAD - r3, bm), BF16)

        k = jnp.dot(ckv_ref[...], wuk_ref[s], preferred_element_type=F32)
        k_ref[s, :, :QK_NOPE] = _rms(k, gkn_ref[...]).astype(BF16)
        k_ref[s, :, QK_NOPE:] = kr_ref[...]
        vt = jnp.dot(wuvt_ref[s], ckvt_ref[...], preferred_element_type=F32)
        vt_ref[s, :V_HEAD, :] = vt.astype(BF16)
        vt_ref[s, V_HEAD:, :] = jnp.ones((V_ROWS - V_HEAD, bm), BF16)


def _qkv(z, pos, inv, gql, gkvl, gqt, gkn, gkr, wuqt, wuk, wuvt, batch, seq, off_q, off_kv, *, bm=512, hp=8):
    h, _, ql = wuqt.shape
    hp = min(hp, h)
    kvl = wuk.shape[1]
    bm = _tile(seq, bm)
    nb = seq // bm
    t = batch * seq
    half = QK_ROPE // 2
    assert off_q % ql == 0 and off_kv % kvl == 0 and (off_kv + kvl) % LANES == 0 and h % hp == 0
    const = lambda i, hh: (0, 0)
    head = lambda i, hh: (hh, 0, 0)
    out_map = lambda i, hh: (i // nb, hh, i % nb, 0)
    out_map_t = lambda i, hh: (i // nb, hh, i % nb, 0, 0)
    invt = jnp.broadcast_to(inv[0, :half].reshape(half, 1), (half, bm))
    gqt = jnp.broadcast_to(gqt.reshape(QK_PAD, 1), (QK_PAD, bm))
    return pl.pallas_call(
        _qkv_kernel,
        grid=(t // bm, h // hp),
        in_specs=[
            pl.BlockSpec((bm, ql), lambda i, hh: (i, off_q // ql)),
            pl.BlockSpec((bm, kvl), lambda i, hh: (i, off_kv // kvl)),
            pl.BlockSpec((bm, LANES), lambda i, hh: (i, (off_kv + kvl) // LANES)),
            pl.BlockSpec((bm, 1), lambda i, hh: (i, 0)),
            pl.BlockSpec((1, bm), lambda i, hh: (0, i)),
            pl.BlockSpec((1, LANES), const),
            pl.BlockSpec((half, bm), const),
            pl.BlockSpec((1, ql), const),
            pl.BlockSpec((1, kvl), const),
            pl.BlockSpec((QK_PAD, bm), const),
            pl.BlockSpec((1, QK_NOPE), const),
            pl.BlockSpec((1, LANES), const),
            pl.BlockSpec((hp, QK_PAD, ql), head),
            pl.BlockSpec((hp, kvl, QK_NOPE), head),
            pl.BlockSpec((hp, V_HEAD, kvl), head),
        ],
        out_specs=[
            pl.BlockSpec((None, hp, None, QK_PAD, bm), out_map_t),
            pl.BlockSpec((None, hp, bm, QK_PAD), out_map),
            pl.BlockSpec((None, hp, None, V_ROWS, bm), out_map_t),
        ],
        out_shape=[
            jax.ShapeDtypeStruct((batch, h, nb, QK_PAD, bm), BF16),
            jax.ShapeDtypeStruct((batch, h, seq, QK_PAD), BF16),
            jax.ShapeDtypeStruct((batch, h, nb, V_ROWS, bm), BF16),
        ],
        scratch_shapes=[
            pltpu.VMEM((ql, bm), BF16),
            pltpu.VMEM((bm, kvl), BF16),
            pltpu.VMEM((kvl, bm), BF16),
            pltpu.VMEM((bm, LANES), BF16),
            pltpu.VMEM((half, bm), F32),
            pltpu.VMEM((half, bm), F32),
        ],
        compiler_params=_params("parallel", "arbitrary"),
        name="qkv_prep",
    )(z, z, z, pos.reshape(t, 1), pos.reshape(1, t), inv, invt, gql, gkvl, gqt, gkn, gkr, wuqt, wuk, wuvt)


def _flash_kernel(qt_ref, k_ref, vt_ref, *refs, blk, chunk, n_cast):
    cast_in, o_ref, cast_out = refs[:n_cast], refs[n_cast], refs[n_cast + 1:2 * n_cast + 1]
    sa_ref, sb_ref, m_ref, acc_ref = refs[2 * n_cast + 1:]
    _cast_block(cast_in, cast_out)

    qi = pl.program_id(2)
    nch = 2 * blk // chunk
    per_blk = blk // chunk
    m_ref[...] = jnp.full(m_ref.shape, -jnp.inf, F32)
    acc_ref[...] = jnp.zeros(acc_ref.shape, F32)

    def visibility(c, diag):
        if diag is None or c * chunk >= (diag + 1) * blk:
            return "all"
        return "none" if (c + 1) * chunk <= diag * blk else "some"

    def scores(kj, s_ref, diag=None):
        start = pl.multiple_of(kj * blk, blk)
        k = k_ref[pl.ds(start, blk), :]
        for c in range(nch):
            if visibility(c, diag) != "none":
                qt = qt_ref[c // per_blk, :, (c % per_blk) * chunk:(c % per_blk + 1) * chunk]
                s_ref[c] = jnp.dot(k, qt, preferred_element_type=F32)

    def update(kj, s_ref, diag=None):
        vt = vt_ref[kj]
        for c in range(nch):
            see = visibility(c, diag)
            if see == "none":
                continue
            cols = slice(c * chunk, (c + 1) * chunk)
            st = s_ref[c]
            if see == "some":
                key = lax.broadcasted_iota(jnp.int32, st.shape, 0) + diag * blk
                qry = lax.broadcasted_iota(jnp.int32, st.shape, 1) + c * chunk
                st = jnp.where(key <= qry, st, -jnp.inf)
            m_old = m_ref[:, cols]
            m_new = jnp.maximum(m_old, jnp.max(st, axis=0, keepdims=True))
            p = jnp.exp2(st - m_new).astype(BF16)
            alpha = jnp.exp2(m_old - m_new)
            acc_ref[:, cols] = alpha * acc_ref[:, cols] + jnp.dot(vt, p, preferred_element_type=F32)
            m_ref[:, cols] = m_new

    scores(0, sa_ref)

    def pair(t):
        scores(2 * t + 1, sb_ref)
        update(2 * t, sa_ref)
        scores(2 * t + 2, sa_ref)
        update(2 * t + 1, sb_ref)

    def trip(u, carry):
        for t in range(PAIRS_PER_TRIP):
            pair(PAIRS_PER_TRIP * u + t)
        return carry

    lax.fori_loop(0, qi // PAIRS_PER_TRIP, trip, 0)
    done = qi // PAIRS_PER_TRIP * PAIRS_PER_TRIP
    for rest in range(PAIRS_PER_TRIP):
        @pl.when(qi - done == rest)
        def _():
            for t in range(rest):
                pair(done + t)
            scores(2 * qi + 1, sb_ref, diag=1)
            update(2 * qi, sa_ref, diag=0)
            update(2 * qi + 1, sb_ref, diag=1)
            o_ref[...] = (acc_ref[:V_HEAD, :] / acc_ref[V_HEAD:V_HEAD + 1, :]).T.astype(o_ref.dtype)


def _flash(qt, k, vt, cast=(), layer=0, *, chunk=256):
    b, h, nb, _, blk = qt.shape
    s = nb * blk
    chunk = min(chunk, blk)
    assert nb % 2 == 0
    cast_in, cast_out = _cast_specs(cast, layer, b * h, nb // 2, lambda bi, hi, qi: (bi * h + hi, qi))
    return pl.pallas_call(
        functools.partial(_flash_kernel, blk=blk, chunk=chunk, n_cast=len(cast)),
        grid=(b, h, nb // 2),
        in_specs=[
            pl.BlockSpec((None, None, 2, QK_PAD, blk), lambda bi, hi, qi: (bi, hi, qi, 0, 0)),
            pl.BlockSpec((None, None, s, QK_PAD), lambda bi, hi, qi: (bi, hi, 0, 0)),
            pl.BlockSpec((None, None, nb, V_ROWS, blk), lambda bi, hi, qi: (bi, hi, 0, 0, 0)),
            *cast_in,
        ],
        out_specs=[pl.BlockSpec((None, 2 * blk, V_HEAD), lambda bi, hi, qi: (bi, qi, hi)), *cast_out],
        out_shape=[jax.ShapeDtypeStruct((b, s, h * V_HEAD), BF16)]
        + [jax.ShapeDtypeStruct(w.shape[1:], BF16) for w in cast],
        scratch_shapes=[
            pltpu.VMEM((2 * blk // chunk, blk, chunk), F32),
            pltpu.VMEM((2 * blk // chunk, blk, chunk), F32),
            pltpu.VMEM((1, 2 * blk), F32),
            pltpu.VMEM((V_ROWS, 2 * blk), F32),
        ],
        compiler_params=_params("parallel", "parallel", "arbitrary"),
        name="mla_flash",
    )(qt, k, vt, *cast)


def _merge_kernel(a_ref, b_ref, wa_ref, wb_ref, gp_ref, gpx_ref, gm_ref, gmx_ref, o_ref, *, shift):
    bn = o_ref.shape[1]

    def gate(lo_ref, hi_ref):
        g = jnp.concatenate([lo_ref[...], hi_ref[...]], axis=1)[:, shift:shift + bn]
        return jax.nn.sigmoid(g)

    pa = jnp.dot(a_ref[...], wa_ref[...], preferred_element_type=F32)
    pb = jnp.dot(b_ref[...], wb_ref[...], preferred_element_type=F32)
    o_ref[...] = (gate(gp_ref, gpx_ref) * pa + gate(gm_ref, gmx_ref) * pb).astype(BF16)


def _merge(a, b, wa, wb, z, gate_off, *, bm=1024, bn=512):
    t, pw = a.shape
    mw = b.shape[1]
    d = wa.shape[1]
    shift = gate_off % LANES
    base = gate_off - shift
    bm, bn = _tile(t, bm), _tile(math.gcd(d, base), bn)
    lanes_per_tile = bn // LANES
    lo = lambda off: (lambda i, j: (i, off // bn + j))
    hi = lambda off: (lambda i, j: (i, (off + bn) // LANES + j * lanes_per_tile))
    return pl.pallas_call(
        functools.partial(_merge_kernel, shift=shift),
        grid=(t // bm, d // bn),
        in_specs=[
            pl.BlockSpec((bm, pw), lambda i, j: (i, 0)),
            pl.BlockSpec((bm, mw), lambda i, j: (i, 0)),
            pl.BlockSpec((pw, bn), lambda i, j: (0, j)),
            pl.BlockSpec((mw, bn), lambda i, j: (0, j)),
            pl.BlockSpec((bm, bn), lo(base)),
            pl.BlockSpec((bm, LANES), hi(base)),
            pl.BlockSpec((bm, bn), lo(base + d)),
            pl.BlockSpec((bm, LANES), hi(base + d)),
        ],
        out_specs=pl.BlockSpec((bm, bn), lambda i, j: (i, j)),
        out_shape=jax.ShapeDtypeStruct((t, d), BF16),
        compiler_params=_params("parallel", "arbitrary"),
        name="branch_merge",
    )(a, b, wa, wb, z, z, z, z)


def _proj_residual_kernel(a_ref, w_ref, h_ref, o_ref):
    o_ref[...] = h_ref[...] + jnp.dot(a_ref[...], w_ref[...], preferred_element_type=F32)


def _proj_residual(a, w, h, *, bm=1024, bn=1024):
    t, kdim = a.shape
    d = w.shape[1]
    bm, bn = _tile(t, bm), _tile(d, bn)
    return pl.pallas_call(
        _proj_residual_kernel,
        grid=(t // bm, d // bn),
        in_specs=[
            pl.BlockSpec((bm, kdim), lambda i, j: (i, 0)),
            pl.BlockSpec((kdim, bn), lambda i, j: (0, j)),
            pl.BlockSpec((bm, bn), lambda i, j: (i, j)),
        ],
        out_specs=pl.BlockSpec((bm, bn), lambda i, j: (i, j)),
        out_shape=jax.ShapeDtypeStruct((t, d), F32),
        compiler_params=_params("parallel", "arbitrary"),
        name="out_proj",
    )(a, w, h)


def _mem_kv_kernel(mem_ref, gain_ref, w_ref, gk_ref, k_ref, v_ref):
    mn = _rms(mem_ref[...], gain_ref[...]).astype(BF16)
    kv = jnp.dot(mn, w_ref[...], preferred_element_type=F32)
    xw = X_HEADS * X_HEAD_DIM
    for hh in range(X_HEADS):
        cols = slice(hh * X_HEAD_DIM, (hh + 1) * X_HEAD_DIM)
        k_ref[:, cols] = _rms(kv[:, cols], gk_ref[...]).astype(BF16)
    v_ref[...] = kv[:, xw:].astype(BF16)


def _mem_kv(mem, gain, w_xkv, gk):
    b, m, d = mem.shape
    xw = X_HEADS * X_HEAD_DIM
    return pl.pallas_call(
        _mem_kv_kernel,
        grid=(b,),
        in_specs=[
            pl.BlockSpec((None, m, d), lambda bi: (bi, 0, 0)),
            pl.BlockSpec((1, d), lambda bi: (0, 0)),
            pl.BlockSpec((d, 2 * xw), lambda bi: (0, 0)),
            pl.BlockSpec((1, X_HEAD_DIM), lambda bi: (0, 0)),
        ],
        out_specs=[
            pl.BlockSpec((None, m, xw), lambda bi: (bi, 0, 0)),
            pl.BlockSpec((None, m, xw), lambda bi: (bi, 0, 0)),
        ],
        out_shape=[
            jax.ShapeDtypeStruct((b, m, xw), BF16),
            jax.ShapeDtypeStruct((b, m, xw), BF16),
        ],
        compiler_params=_params("parallel"),
        name="mem_kv",
    )(mem, gain.reshape(1, d), w_xkv, gk)


def _cross_kernel(h_ref, gain_ref, wq_ref, gq_ref, k_ref, v_ref, wo_ref, o_ref, xo_ref):
    h = h_ref[...]
    uq = _rms(h, gain_ref[...]).astype(BF16)
    xq = jnp.dot(uq, wq_ref[...], preferred_element_type=F32)
    scale = X_HEAD_DIM ** -0.5
    for hh in range(X_HEADS):
        cols = slice(hh * X_HEAD_DIM, (hh + 1) * X_HEAD_DIM)
        qh = _rms(xq[:, cols], gq_ref[...]).astype(BF16)
        s = lax.dot_general(qh, k_ref[:, cols], (((1,), (1,)), ((), ())),
                            preferred_element_type=F32) * scale
        p = jnp.exp(s - jnp.max(s, axis=-1, keepdims=True))
        p = (p / jnp.sum(p, axis=-1, keepdims=True)).astype(BF16)
        xo_ref[:, cols] = jnp.dot(p, v_ref[:, cols], preferred_element_type=F32).astype(BF16)
    o_ref[...] = h + jnp.dot(xo_ref[...], wo_ref[...], preferred_element_type=F32)


def _cross(h, gain, wq, gq, xk, xv, wo, seq, *, bm=256):
    t, d = h.shape
    m = xk.shape[1]
    xw = X_HEADS * X_HEAD_DIM
    bm = _tile(seq, bm)
    nb = seq // bm
    return pl.pallas_call(
        _cross_kernel,
        grid=(t // bm,),
        in_specs=[
            pl.BlockSpec((bm, d), lambda i: (i, 0)),
            pl.BlockSpec((1, d), lambda i: (0, 0)),
            pl.BlockSpec((d, xw), lambda i: (0, 0)),
            pl.BlockSpec((1, X_HEAD_DIM), lambda i: (0, 0)),
            pl.BlockSpec((None, m, xw), lambda i: (i // nb, 0, 0)),
            pl.BlockSpec((None, m, xw), lambda i: (i // nb, 0, 0)),
            pl.BlockSpec((xw, d), lambda i: (0, 0)),
        ],
        out_specs=pl.BlockSpec((bm, d), lambda i: (i, 0)),
        out_shape=jax.ShapeDtypeStruct((t, d), F32),
        scratch_shapes=[pltpu.VMEM((bm, xw), BF16)],
        compiler_params=_params("parallel"),
        name="cross_attn",
    )(h, gain.reshape(1, d), wq, gq, xk, xv, wo)


def _pad_lanes(v, width):
    return jnp.pad(v, (0, width - v.shape[0])).reshape(1, width)


def kernel(x, mem, positions, ffn1_norm, ffn1_w_gu, ffn1_w_down, mix_norm, w_in, w_pool, pool_scale,
           q_latent_norm, kv_latent_norm, w_uq, w_ukv, q_nope_norm, k_nope_norm, q_rope_norm,
           k_rope_norm, w_branch_pool, w_branch_mla, w_out, x_norm, mem_norm, w_xq, w_xkv, xq_norm,
           xk_norm, w_xo, ffn2_norm, ffn2_w_gu, ffn2_w_down):
    batch, seq, d = x.shape
    t = batch * seq
    depth = w_in.shape[0]
    pw = w_branch_pool.shape[1]
    ql = q_latent_norm.shape[1]
    kvl = kv_latent_norm.shape[1]
    heads = w_uq.shape[2] // QK_HEAD

    half = QK_ROPE // 2
    inv = 1.0 / (ROPE_THETA ** (jnp.arange(half, dtype=F32) * (2.0 / QK_ROPE)))
    inv = _pad_lanes(jnp.concatenate([inv, inv]), LANES)
    c2 = QK_HEAD ** -0.5 * math.log2(math.e)

    h = x.reshape(t, d)
    for l in range(depth):
        h = _ffn(h, ffn1_norm[l], ffn1_w_gu[l].astype(BF16), ffn1_w_down[l].astype(BF16))

        off_kv = pw + ql
        off_gate = off_kv + kvl + QK_ROPE
        z = _norm_matmul(h, mix_norm[l], w_in[l].astype(BF16), bn=IN_TILE, name="in_proj")

        a_out = _pool(z, w_pool[l].astype(BF16), pool_scale[l], batch, seq)

        wuq = jnp.pad(w_uq[l].reshape(ql, heads, QK_HEAD), ((0, 0), (0, 0), (0, QK_PAD - QK_HEAD)))
        wuqt = wuq.transpose(1, 2, 0).astype(BF16)
        wukv = w_ukv[l].reshape(kvl, heads, QK_NOPE + V_HEAD)
        wuk = wukv[:, :, :QK_NOPE].transpose(1, 0, 2).astype(BF16)
        wuvt = wukv[:, :, QK_NOPE:].transpose(1, 2, 0).astype(BF16)
        gqt = jnp.concatenate([q_nope_norm[l], q_rope_norm[l], jnp.zeros((QK_PAD - QK_HEAD,), F32)]) * c2
        qt, k, vt = _qkv(z, positions, inv,
                         q_latent_norm[l].reshape(1, ql), kv_latent_norm[l].reshape(1, kvl), gqt,
                         k_nope_norm[l].reshape(1, QK_NOPE), _pad_lanes(k_rope_norm[l], LANES),
                         wuqt, wuk, wuvt, batch, seq, pw, off_kv)
        later = (ffn2_w_gu, ffn2_w_down, w_out, w_branch_pool, w_branch_mla)
        b_out, w_gu2, w_down2, w_out_b, w_bp, w_bm = _flash(qt, k, vt, later, l)
        b_out = b_out.reshape(t, heads * V_HEAD)

        merged = _merge(a_out, b_out, w_bp, w_bm, z, off_gate)
        h = _proj_residual(merged, w_out_b, h)

        xk, xv = _mem_kv(mem, mem_norm[l], w_xkv[l].astype(BF16), xk_norm[l].reshape(1, X_HEAD_DIM))
        h = _cross(h, x_norm[l], w_xq[l].astype(BF16), xq_norm[l].reshape(1, X_HEAD_DIM), xk, xv,
                   w_xo[l].astype(BF16), seq)

        h = _ffn(h, ffn2_norm[l], w_gu2, w_down2)
    return h.reshape(batch, seq, d)
```

```python
import functools
import math

import jax
import jax.numpy as jnp
from jax import lax
from jax.experimental import pallas as pl
from jax.experimental.pallas import tpu as pltpu

F32 = jnp.float32
BF16 = jnp.bfloat16

POOL_WINDOWS = (2, 4, 8, 16)
POOL_HALO = 16
QK_NOPE = 128
QK_ROPE = 64
QK_HEAD = QK_NOPE + QK_ROPE
QK_PAD = 256
V_HEAD = 128
V_ROWS = V_HEAD + 16
X_HEADS = 4
X_HEAD_DIM = 128
ROPE_THETA = 10000.0
EPS = 1e-6
LANES = 128
BF16_ROWS = 16
VMEM_LIMIT = 62 * 1024 * 1024
NORM_ROWS = 256
PAIRS_PER_TRIP = 4
IN_TILE = 1024


def _params(*sem):
    return pltpu.CompilerParams(dimension_semantics=sem, vmem_limit_bytes=VMEM_LIMIT)


def _tile(n, pref):
    if n <= pref:
        return n
    t = (pref // LANES) * LANES
    while t >= LANES:
        if n % t == 0:
            return t
        t -= LANES
    raise ValueError(f"no lane-aligned tile of {n} below {pref}")


def _ceil_to(x, m):
    return -(-x // m) * m


def _rms(x, gain):
    ms = jnp.mean(x * x, axis=-1, keepdims=True)
    return x * lax.rsqrt(ms + EPS) * gain


def _ffn_kernel(x_ref, gain_ref, wg_ref, wu_ref, wd_ref, o_ref, xn_ref):
    def add_tile(base, xn):
        g = jnp.dot(xn, wg_ref[...], preferred_element_type=F32)
        u = jnp.dot(xn, wu_ref[...], preferred_element_type=F32)
        hid = (g * jax.nn.sigmoid(g) * u * 0.5).astype(BF16)
        return base + jnp.dot(hid, wd_ref[...], preferred_element_type=F32)

    @pl.when(pl.program_id(1) == 0)
    def _():
        for r in range(0, x_ref.shape[0], NORM_ROWS):
            rows = slice(r, min(r + NORM_ROWS, x_ref.shape[0]))
            x = x_ref[rows, :]
            xn = _rms(x, gain_ref[...]).astype(BF16)
            xn_ref[rows, :] = xn
            o_ref[rows, :] = add_tile(x, xn)

    @pl.when(pl.program_id(1) > 0)
    def _():
        o_ref[...] = add_tile(o_ref[...], xn_ref[...])


def _cast_specs(ws, layer, gi, gj, rc):
    ins, outs = [], []
    for w in ws:
        _, r, c = w.shape
        br = _ceil_to(pl.cdiv(r, gi), BF16_ROWS)
        bc = _ceil_to(pl.cdiv(c, gj), LANES)
        last = pl.cdiv(c, bc) - 1
        assert (gi - 1) * br < r <= gi * br and last < gj

        def block(*g, last=last):
            i, j = rc(*g)
            return i, jnp.minimum(j, last)

        ins.append(pl.BlockSpec((None, br, bc), lambda *g, block=block: (layer, *block(*g))))
        outs.append(pl.BlockSpec((br, bc), block))
    return ins, outs


def _cast_block(cast_in, cast_out):
    for src, dst in zip(cast_in, cast_out):
        dst[...] = src[...].astype(BF16)


def _ffn(x, gain, w_gu, w_down, *, bm=512, bf=256):
    t, d = x.shape
    f = w_down.shape[0]
    bm, bf = _tile(t, bm), _tile(f, bf)
    nf = f // bf
    return pl.pallas_call(
        _ffn_kernel,
        grid=(t // bm, nf),
        in_specs=[
            pl.BlockSpec((bm, d), lambda i, j: (i, 0)),
            pl.BlockSpec((1, d), lambda i, j: (0, 0)),
            pl.BlockSpec((d, bf), lambda i, j: (0, j)),
            pl.BlockSpec((d, bf), lambda i, j: (0, j + nf)),
            pl.BlockSpec((bf, d), lambda i, j: (j, 0)),
        ],
        out_specs=pl.BlockSpec((bm, d), lambda i, j: (i, 0)),
        out_shape=jax.ShapeDtypeStruct((t, d), F32),
        scratch_shapes=[pltpu.VMEM((bm, d), BF16)],
        compiler_params=_params("parallel", "arbitrary"),
        name="ffn",
    )(x, gain.reshape(1, d), w_gu, w_gu, w_down)


def _norm_matmul_kernel(x_ref, gain_ref, w_ref, o_ref, xn_ref):
    @pl.when(pl.program_id(1) == 0)
    def _():
        for r in range(0, x_ref.shape[0], NORM_ROWS):
            rows = slice(r, min(r + NORM_ROWS, x_ref.shape[0]))
            xn = _rms(x_ref[rows, :], gain_ref[...]).astype(BF16)
            xn_ref[rows, :] = xn
            o_ref[rows, :] = jnp.dot(xn, w_ref[...], preferred_element_type=F32)

    @pl.when(pl.program_id(1) > 0)
    def _():
        o_ref[...] = jnp.dot(xn_ref[...], w_ref[...], preferred_element_type=F32)


def _norm_matmul(x, gain, w, *, bm=1024, bn=1024, name):
    t, d = x.shape
    n = w.shape[1]
    bm, bn = _tile(t, bm), min(bn, n)
    return pl.pallas_call(
        _norm_matmul_kernel,
        grid=(t // bm, pl.cdiv(n, bn)),
        in_specs=[
            pl.BlockSpec((bm, d), lambda i, j: (i, 0), pipeline_mode=pl.Buffered(1)),
            pl.BlockSpec((1, d), lambda i, j: (0, 0)),
            pl.BlockSpec((d, bn), lambda i, j: (0, j)),
        ],
        out_specs=pl.BlockSpec((bm, bn), lambda i, j: (i, j)),
        out_shape=jax.ShapeDtypeStruct((t, n), F32),
        scratch_shapes=[pltpu.VMEM((bm, d), BF16)],
        compiler_params=_params("parallel", "arbitrary"),
        name=name,
    )(x, gain.reshape(1, d), w)


def _pool_kernel(zc_ref, zp_ref, wp_ref, sc_ref, o_ref, ext_ref, *, bm, c):
    i = pl.program_id(1)
    ext_ref[0:POOL_HALO, :] = jnp.where(i > 0, zp_ref[...], 0.0)
    ext_ref[POOL_HALO:POOL_HALO + bm, :] = zc_ref[...]
    t = i * bm + lax.broadcasted_iota(jnp.int32, (bm, 1), 0)
    for g, w in enumerate(POOL_WINDOWS):
        cols = slice(g * c, (g + 1) * c)
        x = zc_ref[:, cols]
        acc = x
        for k in range(1, w):
            acc = acc + ext_ref[POOL_HALO - k:POOL_HALO - k + bm, cols]
        cnt = jnp.minimum(t + 1, w).astype(F32)
        mixed = (acc / cnt - x).astype(BF16)
        y = jnp.dot(mixed, wp_ref[g], preferred_element_type=F32) * sc_ref[:, cols]
        o_ref[:, cols] = y.astype(BF16)


def _pool(z_main, w_pool, pool_scale, batch, seq, *, bm=512):
    g, c, _ = w_pool.shape
    pw = g * c
    bm = _tile(seq, bm)
    nb = seq // bm
    halo_blocks = bm // POOL_HALO
    return pl.pallas_call(
        functools.partial(_pool_kernel, bm=bm, c=c),
        grid=(batch, nb),
        in_specs=[
            pl.BlockSpec((bm, pw), lambda b, i: (b * nb + i, 0)),
            pl.BlockSpec((POOL_HALO, pw),
                         lambda b, i: (jnp.maximum((b * nb + i) * halo_blocks - 1, 0), 0)),
            pl.BlockSpec((g, c, c), lambda b, i: (0, 0, 0)),
            pl.BlockSpec((1, pw), lambda b, i: (0, 0)),
        ],
        out_specs=pl.BlockSpec((bm, pw), lambda b, i: (b * nb + i, 0)),
        out_shape=jax.ShapeDtypeStruct((batch * seq, pw), BF16),
        scratch_shapes=[pltpu.VMEM((POOL_HALO + bm, pw), F32)],
        compiler_params=_params("parallel", "parallel"),
        name="pool",
    )(z_main, z_main, w_pool, pool_scale.reshape(1, pw))


def _rope_tables(pos, inv):
    ang = pos.astype(F32) * inv
    cos, sin = jnp.cos(ang), jnp.sin(ang)
    lane = lax.broadcasted_iota(jnp.int32, ang.shape, 1)
    half = QK_ROPE // 2
    c = jnp.where(lane < QK_ROPE, cos, 0.0)
    sa = jnp.where(lane < half, -sin, 0.0)
    sb = jnp.where((lane >= half) & (lane < QK_ROPE), sin, 0.0)
    return c, sa, sb


def _rope_norm(r, gain, c, sa, sb):
    ms = jnp.sum(r * r, axis=-1, keepdims=True) * (1.0 / QK_ROPE)
    r = r * lax.rsqrt(ms + EPS) * gain
    half = QK_ROPE // 2
    return r * c + pltpu.roll(r, LANES - half, 1) * sa + pltpu.roll(r, half, 1) * sb


def _qkv_kernel(zq_ref, zkv_ref, zkr_ref, pos_ref, post_ref, inv_ref, invt_ref, gql_ref, gkvl_ref, gqt_ref,
                gkn_ref, gkr_ref, wuqt_ref, wuk_ref, wuvt_ref, qt_ref, k_ref, vt_ref,
                cqt_ref, ckv_ref, ckvt_ref, kr_ref, cost_ref, sint_ref):
    half = QK_ROPE // 2
    r1, r2, r3 = QK_NOPE, QK_NOPE + half, QK_HEAD

    @pl.when(pl.program_id(1) == 0)
    def _():
        cqt_ref[...] = _rms(zq_ref[...], gql_ref[...]).T.astype(BF16)
        ckv = _rms(zkv_ref[...], gkvl_ref[...])
        ckv_ref[...] = ckv.astype(BF16)
        ckvt_ref[...] = ckv.T.astype(BF16)
        c, sa, sb = _rope_tables(pos_ref[...], inv_ref[...])
        lane = lax.broadcasted_iota(jnp.int32, zkr_ref.shape, 1)
        zkr = jnp.where(lane < QK_ROPE, zkr_ref[...], 0.0)
        kr_ref[...] = _rope_norm(zkr, gkr_ref[...], c, sa, sb).astype(BF16)
        ang = post_ref[...].astype(F32) * invt_ref[...]
        cost_ref[...] = jnp.cos(ang)
        sint_ref[...] = jnp.sin(ang)

    bm = qt_ref.shape[-1]
    for s in range(qt_ref.shape[0]):
        q = jnp.dot(wuqt_ref[s], cqt_ref[...], preferred_element_type=F32)
        nope = q[:r1]
        rs = lax.rsqrt(jnp.mean(nope * nope, axis=0, keepdims=True) + EPS)
        qt_ref[s, :r1, :] = (nope * rs * gqt_ref[:r1, :]).astype(BF16)
        x1, x2 = q[r1:r2], q[r2:r3]
        ms = (jnp.sum(x1 * x1, axis=0, keepdims=True)
              + jnp.sum(x2 * x2, axis=0, keepdims=True)) * (1.0 / QK_ROPE)
        rs = lax.rsqrt(ms + EPS)
        y1, y2 = x1 * rs * gqt_ref[r1:r2, :], x2 * rs * gqt_ref[r2:r3, :]
        cos, sin = cost_ref[...], sint_ref[...]
        qt_ref[s, r1:r2, :] = (y1 * cos - y2 * sin).astype(BF16)
        qt_ref[s, r2:r3, :] = (y2 * cos + y1 * sin).astype(BF16)
        qt_ref[s, r3:, :] = jnp.zeros((QK_PAD - r3, bm), BF16)

        k = jnp.dot(ckv_ref[...], wuk_ref[s], preferred_element_type=F32)
        k_ref[s, :, :QK_NOPE] = _rms(k, gkn_ref[...]).astype(BF16)
        k_ref[s, :, QK_NOPE:] = kr_ref[...]
        vt = jnp.dot(wuvt_ref[s], ckvt_ref[...], preferred_element_type=F32)
        vt_ref[s, :V_HEAD, :] = vt.astype(BF16)
        vt_ref[s, V_HEAD:, :] = jnp.ones((V_ROWS - V_HEAD, bm), BF16)


def _qkv(z, pos, inv, gql, gkvl, gqt, gkn, gkr, wuqt, wuk, wuvt, batch, seq, off_q, off_kv, *, bm=512, hp=8):
    h, _, ql = wuqt.shape
    hp = min(hp, h)
    kvl = wuk.shape[1]
    bm = _tile(seq, bm)
    nb = seq // bm
    t = batch * seq
    half = QK_ROPE // 2
    assert off_q % ql == 0 and off_kv % kvl == 0 and (off_kv + kvl) % LANES == 0 and h % hp == 0
    const = lambda i, hh: (0, 0)
    head = lambda i, hh: (hh, 0, 0)
    out_map = lambda i, hh: (i // nb, hh, i % nb, 0)
    out_map_t = lambda i, hh: (i // nb, hh, i % nb, 0, 0)
    invt = jnp.broadcast_to(inv[0, :half].reshape(half, 1), (half, bm))
    gqt = jnp.broadcast_to(gqt.reshape(QK_PAD, 1), (QK_PAD, bm))
    return pl.pallas_call(
        _qkv_kernel,
        grid=(t // bm, h // hp),
        in_specs=[
            pl.BlockSpec((bm, ql), lambda i, hh: (i, off_q // ql)),
            pl.BlockSpec((bm, kvl), lambda i, hh: (i, off_kv // kvl)),
            pl.BlockSpec((bm, LANES), lambda i, hh: (i, (off_kv + kvl) // LANES)),
            pl.BlockSpec((bm, 1), lambda i, hh: (i, 0)),
            pl.BlockSpec((1, bm), lambda i, hh: (0, i)),
            pl.BlockSpec((1, LANES), const),
            pl.BlockSpec((half, bm), const),
            pl.BlockSpec((1, ql), const),
            pl.BlockSpec((1, kvl), const),
            pl.BlockSpec((QK_PAD, bm), const),
            pl.BlockSpec((1, QK_NOPE), const),
            pl.BlockSpec((1, LANES), const),
            pl.BlockSpec((hp, QK_PAD, ql), head),
            pl.BlockSpec((hp, kvl, QK_NOPE), head),
            pl.BlockSpec((hp, V_HEAD, kvl), head),
        ],
        out_specs=[
            pl.BlockSpec((None, hp, None, QK_PAD, bm), out_map_t),
            pl.BlockSpec((None, hp, bm, QK_PAD), out_map),
            pl.BlockSpec((None, hp, None, V_ROWS, bm), out_map_t),
        ],
        out_shape=[
            jax.ShapeDtypeStruct((batch, h, nb, QK_PAD, bm), BF16),
            jax.ShapeDtypeStruct((batch, h, seq, QK_PAD), BF16),
            jax.ShapeDtypeStruct((batch, h, nb, V_ROWS, bm), BF16),
        ],
        scratch_shapes=[
            pltpu.VMEM((ql, bm), BF16),
            pltpu.VMEM((bm, kvl), BF16),
            pltpu.VMEM((kvl, bm), BF16),
            pltpu.VMEM((bm, LANES), BF16),
            pltpu.VMEM((half, bm), F32),
            pltpu.VMEM((half, bm), F32),
        ],
        compiler_params=_params("parallel", "arbitrary"),
        name="qkv_prep",
    )(z, z, z, pos.reshape(t, 1), pos.reshape(1, t), inv, invt, gql, gkvl, gqt, gkn, gkr, wuqt, wuk, wuvt)


def _flash_kernel(qt_ref, k_ref, vt_ref, *refs, blk, chunk, n_cast):
    cast_in, o_ref, cast_out = refs[:n_cast], refs[n_cast], refs[n_cast + 1:2 * n_cast + 1]
    sa_ref, sb_ref, m_ref, acc_ref = refs[2 * n_cast + 1:]
    _cast_block(cast_in, cast_out)

    qi = pl.program_id(2)
    nch = 2 * blk // chunk
    per_blk = blk // chunk
    m_ref[...] = jnp.full(m_ref.shape, -jnp.inf, F32)
    acc_ref[...] = jnp.zeros(acc_ref.shape, F32)

    def visibility(c, diag):
        if diag is None or c * chunk >= (diag + 1) * blk:
            return "all"
        return "none" if (c + 1) * chunk <= diag * blk else "some"

    def scores(kj, s_ref, diag=None):
        start = pl.multiple_of(kj * blk, blk)
        k = k_ref[pl.ds(start, blk), :]
        for c in range(nch):
            if visibility(c, diag) != "none":
                qt = qt_ref[c // per_blk, :, (c % per_blk) * chunk:(c % per_blk + 1) * chunk]
                s_ref[c] = jnp.dot(k, qt, preferred_element_type=F32)

    def update(kj, s_ref, diag=None):
        vt = vt_ref[kj]
        for c in range(nch):
            see = visibility(c, diag)
            if see == "none":
                continue
            cols = slice(c * chunk, (c + 1) * chunk)
            st = s_ref[c]
            if see == "some":
                key = lax.broadcasted_iota(jnp.int32, st.shape, 0) + diag * blk
                qry = lax.broadcasted_iota(jnp.int32, st.shape, 1) + c * chunk
                st = jnp.where(key <= qry, st, -jnp.inf)
            m_old = m_ref[:, cols]
            m_new = jnp.maximum(m_old, jnp.max(st, axis=0, keepdims=True))
            p = jnp.exp2(st - m_new).astype(BF16)
            alpha = jnp.exp2(m_old - m_new)
            acc_ref[:, cols] = alpha * acc_ref[:, cols] + jnp.dot(vt, p, preferred_element_type=F32)
            m_ref[:, cols] = m_new

    scores(0, sa_ref)

    def pair(t):
        scores(2 * t + 1, sb_ref)
        update(2 * t, sa_ref)
        scores(2 * t + 2, sa_ref)
        update(2 * t + 1, sb_ref)

    def trip(u, carry):
        for t in range(PAIRS_PER_TRIP):
            pair(PAIRS_PER_TRIP * u + t)
        return carry

    lax.fori_loop(0, qi // PAIRS_PER_TRIP, trip, 0)
    done = qi // PAIRS_PER_TRIP * PAIRS_PER_TRIP
    for rest in range(PAIRS_PER_TRIP):
        @pl.when(qi - done == rest)
        def _():
            for t in range(rest):
                pair(done + t)
            scores(2 * qi + 1, sb_ref, diag=1)
            update(2 * qi, sa_ref, diag=0)
            update(2 * qi + 1, sb_ref, diag=1)
            o_ref[...] = (acc_ref[:V_HEAD, :] / acc_ref[V_HEAD:V_HEAD + 1, :]).T.astype(o_ref.dtype)


def _flash(qt, k, vt, cast=(), layer=0, *, chunk=256):
    b, h, nb, _, blk = qt.shape
    s = nb * blk
    chunk = min(chunk, blk)
    assert nb % 2 == 0
    cast_in, cast_out = _cast_specs(cast, layer, b * h, nb // 2, lambda bi, hi, qi: (bi * h + hi, qi))
    return pl.pallas_call(
        functools.partial(_flash_kernel, blk=blk, chunk=chunk, n_cast=len(cast)),
        grid=(b, h, nb // 2),
        in_specs=[
            pl.BlockSpec((None, None, 2, QK_PAD, blk), lambda bi, hi, qi: (bi, hi, qi, 0, 0)),
            pl.BlockSpec((None, None, s, QK_PAD), lambda bi, hi, qi: (bi, hi, 0, 0)),
            pl.BlockSpec((None, None, nb, V_ROWS, blk), lambda bi, hi, qi: (bi, hi, 0, 0, 0)),
            *cast_in,
        ],
        out_specs=[pl.BlockSpec((None, 2 * blk, V_HEAD), lambda bi, hi, qi: (bi, qi, hi)), *cast_out],
        out_shape=[jax.ShapeDtypeStruct((b, s, h * V_HEAD), BF16)]
        + [jax.ShapeDtypeStruct(w.shape[1:], BF16) for w in cast],
        scratch_shapes=[
            pltpu.VMEM((2 * blk // chunk, blk, chunk), F32),
            pltpu.VMEM((2 * blk // chunk, blk, chunk), F32),
            pltpu.VMEM((1, 2 * blk), F32),
            pltpu.VMEM((V_ROWS, 2 * blk), F32),
        ],
        compiler_params=_params("parallel", "parallel", "arbitrary"),
        name="mla_flash",
    )(qt, k, vt, *cast)


def _merge_kernel(a_ref, b_ref, wa_ref, wb_ref, gp_ref, gpx_ref, gm_ref, gmx_ref, o_ref, *, shift):
    bn = o_ref.shape[1]

    def gate(lo_ref, hi_ref):
        g = jnp.concatenate([lo_ref[...], hi_ref[...]], axis=1)[:, shift:shift + bn]
        return jax.nn.sigmoid(g)

    pa = jnp.dot(a_ref[...], wa_ref[...], preferred_element_type=F32)
    pb = jnp.dot(b_ref[...], wb_ref[...], preferred_element_type=F32)
    o_ref[...] = (gate(gp_ref, gpx_ref) * pa + gate(gm_ref, gmx_ref) * pb).astype(BF16)


def _merge(a, b, wa, wb, z, gate_off, *, bm=1024, bn=512):
    t, pw = a.shape
    mw = b.shape[1]
    d = wa.shape[1]
    shift = gate_off % LANES
    base = gate_off - shift
    bm, bn = _tile(t, bm), _tile(math.gcd(d, base), bn)
    lanes_per_tile = bn // LANES
    lo = lambda off: (lambda i, j: (i, off // bn + j))
    hi = lambda off: (lambda i, j: (i, (off + bn) // LANES + j * lanes_per_tile))
    return pl.pallas_call(
        functools.partial(_merge_kernel, shift=shift),
        grid=(t // bm, d // bn),
        in_specs=[
            pl.BlockSpec((bm, pw), lambda i, j: (i, 0)),
            pl.BlockSpec((bm, mw), lambda i, j: (i, 0)),
            pl.BlockSpec((pw, bn), lambda i, j: (0, j)),
            pl.BlockSpec((mw, bn), lambda i, j: (0, j)),
            pl.BlockSpec((bm, bn), lo(base)),
            pl.BlockSpec((bm, LANES), hi(base)),
            pl.BlockSpec((bm, bn), lo(base + d)),
            pl.BlockSpec((bm, LANES), hi(base + d)),
        ],
        out_specs=pl.BlockSpec((bm, bn), lambda i, j: (i, j)),
        out_shape=jax.ShapeDtypeStruct((t, d), BF16),
        compiler_params=_params("parallel", "arbitrary"),
        name="branch_merge",
    )(a, b, wa, wb, z, z, z, z)


def _proj_residual_kernel(a_ref, w_ref, h_ref, o_ref):
    o_ref[...] = h_ref[...] + jnp.dot(a_ref[...], w_ref[...], preferred_element_type=F32)


def _proj_residual(a, w, h, *, bm=1024, bn=1024):
    t, kdim = a.shape
    d = w.shape[1]
    bm, bn = _tile(t, bm), _tile(d, bn)
    return pl.pallas_call(
        _proj_residual_kernel,
        grid=(t // bm, d // bn),
        in_specs=[
            pl.BlockSpec((bm, kdim), lambda i, j: (i, 0)),
            pl.BlockSpec((kdim, bn), lambda i, j: (0, j)),
            pl.BlockSpec((bm, bn), lambda i, j: (i, j)),
        ],
        out_specs=pl.BlockSpec((bm, bn), lambda i, j: (i, j)),
        out_shape=jax.ShapeDtypeStruct((t, d), F32),
        compiler_params=_params("parallel", "arbitrary"),
        name="out_proj",
    )(a, w, h)


def _mem_kv_kernel(mem_ref, gain_ref, w_ref, gk_ref, k_ref, v_ref):
    mn = _rms(mem_ref[...], gain_ref[...]).astype(BF16)
    kv = jnp.dot(mn, w_ref[...], preferred_element_type=F32)
    xw = X_HEADS * X_HEAD_DIM
    for hh in range(X_HEADS):
        cols = slice(hh * X_HEAD_DIM, (hh + 1) * X_HEAD_DIM)
        k_ref[:, cols] = _rms(kv[:, cols], gk_ref[...]).astype(BF16)
    v_ref[...] = kv[:, xw:].astype(BF16)


def _mem_kv(mem, gain, w_xkv, gk):
    b, m, d = mem.shape
    xw = X_HEADS * X_HEAD_DIM
    return pl.pallas_call(
        _mem_kv_kernel,
        grid=(b,),
        in_specs=[
            pl.BlockSpec((None, m, d), lambda bi: (bi, 0, 0)),
            pl.BlockSpec((1, d), lambda bi: (0, 0)),
            pl.BlockSpec((d, 2 * xw), lambda bi: (0, 0)),
            pl.BlockSpec((1, X_HEAD_DIM), lambda bi: (0, 0)),
        ],
        out_specs=[
            pl.BlockSpec((None, m, xw), lambda bi: (bi, 0, 0)),
            pl.BlockSpec((None, m, xw), lambda bi: (bi, 0, 0)),
        ],
        out_shape=[
            jax.ShapeDtypeStruct((b, m, xw), BF16),
            jax.ShapeDtypeStruct((b, m, xw), BF16),
        ],
        compiler_params=_params("parallel"),
        name="mem_kv",
    )(mem, gain.reshape(1, d), w_xkv, gk)


def _cross_kernel(h_ref, gain_ref, wq_ref, gq_ref, k_ref, v_ref, wo_ref, o_ref, xo_ref):
    h = h_ref[...]
    uq = _rms(h, gain_ref[...]).astype(BF16)
    xq = jnp.dot(uq, wq_ref[...], preferred_element_type=F32)
    scale = X_HEAD_DIM ** -0.5
    for hh in range(X_HEADS):
        cols = slice(hh * X_HEAD_DIM, (hh + 1) * X_HEAD_DIM)
        qh = _rms(xq[:, cols], gq_ref[...]).astype(BF16)
        s = lax.dot_general(qh, k_ref[:, cols], (((1,), (1,)), ((), ())),
                            preferred_element_type=F32) * scale
        p = jnp.exp(s - jnp.max(s, axis=-1, keepdims=True))
        p = (p / jnp.sum(p, axis=-1, keepdims=True)).astype(BF16)
        xo_ref[:, cols] = jnp.dot(p, v_ref[:, cols], preferred_element_type=F32).astype(BF16)
    o_ref[...] = h + jnp.dot(xo_ref[...], wo_ref[...], preferred_element_type=F32)


def _cross(h, gain, wq, gq, xk, xv, wo, seq, *, bm=512):
    t, d = h.shape
    m = xk.shape[1]
    xw = X_HEADS * X_HEAD_DIM
    bm = _tile(seq, bm)
    nb = seq // bm
    return pl.pallas_call(
        _cross_kernel,
        grid=(t // bm,),
        in_specs=[
            pl.BlockSpec((bm, d), lambda i: (i, 0)),
            pl.BlockSpec((1, d), lambda i: (0, 0)),
            pl.BlockSpec((d, xw), lambda i: (0, 0)),
            pl.BlockSpec((1, X_HEAD_DIM), lambda i: (0, 0)),
            pl.BlockSpec((None, m, xw), lambda i: (i // nb, 0, 0)),
            pl.BlockSpec((None, m, xw), lambda i: (i // nb, 0, 0)),
            pl.BlockSpec((xw, d), lambda i: (0, 0)),
        ],
        out_specs=pl.BlockSpec((bm, d), lambda i: (i, 0)),
        out_shape=jax.ShapeDtypeStruct((t, d), F32),
        scratch_shapes=[pltpu.VMEM((bm, xw), BF16)],
        compiler_params=_params("parallel"),
        name="cross_attn",
    )(h, gain.reshape(1, d), wq, gq, xk, xv, wo)


def _pad_lanes(v, width):
    return jnp.pad(v, (0, width - v.shape[0])).reshape(1, width)


def kernel(x, mem, positions, ffn1_norm, ffn1_w_gu, ffn1_w_down, mix_norm, w_in, w_pool, pool_scale,
           q_latent_norm, kv_latent_norm, w_uq, w_ukv, q_nope_norm, k_nope_norm, q_rope_norm,
           k_rope_norm, w_branch_pool, w_branch_mla, w_out, x_norm, mem_norm, w_xq, w_xkv, xq_norm,
           xk_norm, w_xo, ffn2_norm, ffn2_w_gu, ffn2_w_down):
    batch, seq, d = x.shape
    t = batch * seq
    depth = w_in.shape[0]
    pw = w_branch_pool.shape[1]
    ql = q_latent_norm.shape[1]
    kvl = kv_latent_norm.shape[1]
    heads = w_uq.shape[2] // QK_HEAD

    half = QK_ROPE // 2
    inv = 1.0 / (ROPE_THETA ** (jnp.arange(half, dtype=F32) * (2.0 / QK_ROPE)))
    inv = _pad_lanes(jnp.concatenate([inv, inv]), LANES)
    c2 = QK_HEAD ** -0.5 * math.log2(math.e)

    h = x.reshape(t, d)
    for l in range(depth):
        h = _ffn(h, ffn1_norm[l], ffn1_w_gu[l].astype(BF16), ffn1_w_down[l].astype(BF16))

        off_kv = pw + ql
        off_gate = off_kv + kvl + QK_ROPE
        z = _norm_matmul(h, mix_norm[l], w_in[l].astype(BF16), bn=IN_TILE, name="in_proj")

        a_out = _pool(z, w_pool[l].astype(BF16), pool_scale[l], batch, seq)

        wuq = jnp.pad(w_uq[l].reshape(ql, heads, QK_HEAD), ((0, 0), (0, 0), (0, QK_PAD - QK_HEAD)))
        wuqt = wuq.transpose(1, 2, 0).astype(BF16)
        wukv = w_ukv[l].reshape(kvl, heads, QK_NOPE + V_HEAD)
        wuk = wukv[:, :, :QK_NOPE].transpose(1, 0, 2).astype(BF16)
        wuvt = wukv[:, :, QK_NOPE:].transpose(1, 2, 0).astype(BF16)
        gqt = jnp.concatenate([q_nope_norm[l], q_rope_norm[l], jnp.zeros((QK_PAD - QK_HEAD,), F32)]) * c2
        qt, k, vt = _qkv(z, positions, inv,
                         q_latent_norm[l].reshape(1, ql), kv_latent_norm[l].reshape(1, kvl), gqt,
                         k_nope_norm[l].reshape(1, QK_NOPE), _pad_lanes(k_rope_norm[l], LANES),
                         wuqt, wuk, wuvt, batch, seq, pw, off_kv)
        later = (ffn2_w_gu, ffn2_w_down, w_out, w_branch_pool, w_branch_mla)
        b_out, w_gu2, w_down2, w_out_b, w_bp, w_bm = _flash(qt, k, vt, later, l)
        b_out = b_out.reshape(t, heads * V_HEAD)

        merged = _merge(a_out, b_out, w_bp, w_bm, z, off_gate)
        h = _proj_residual(merged, w_out_b, h)

        xk, xv = _mem_kv(mem, mem_norm[l], w_xkv[l].astype(BF16), xk_norm[l].reshape(1, X_HEAD_DIM))
        h = _cross(h, x_norm[l], w_xq[l].astype(BF16), xq_norm[l].reshape(1, X_HEAD_DIM), xk, xv,
                   w_xo[l].astype(BF16), seq)

        h = _ffn(h, ffn2_norm[l], w_gu2, w_down2)
    return h.reshape(batch, seq, d)
```

```python
import functools
import math

import jax
import jax.numpy as jnp
from jax import lax
from jax.experimental import pallas as pl
from jax.experimental.pallas import tpu as pltpu

F32 = jnp.float32
BF16 = jnp.bfloat16

POOL_WINDOWS = (2, 4, 8, 16)
POOL_HALO = 16
QK_NOPE = 128
QK_ROPE = 64
QK_HEAD = QK_NOPE + QK_ROPE
QK_PAD = 256
V_HEAD = 128
V_ROWS = V_HEAD + 16
X_HEADS = 4
X_HEAD_DIM = 128
ROPE_THETA = 10000.0
EPS = 1e-6
LANES = 128
BF16_ROWS = 16
VMEM_LIMIT = 62 * 1024 * 1024
NORM_ROWS = 256
PAIRS_PER_TRIP = 4
IN_TILE = 1024


def _params(*sem):
    return pltpu.CompilerParams(dimension_semantics=sem, vmem_limit_bytes=VMEM_LIMIT)


def _tile(n, pref):
    if n <= pref:
        return n
    t = (pref // LANES) * LANES
    while t >= LANES:
        if n % t == 0:
            return t
        t -= LANES
    raise ValueError(f"no lane-aligned tile of {n} below {pref}")


def _ceil_to(x, m):
    return -(-x // m) * m


def _rms(x, gain):
    ms = jnp.mean(x * x, axis=-1, keepdims=True)
    return x * lax.rsqrt(ms + EPS) * gain


def _ffn_kernel(x_ref, gain_ref, wg_ref, wu_ref, wd_ref, o_ref, xn_ref):
    def add_tile(base, xn):
        g = jnp.dot(xn, wg_ref[...], preferred_element_type=F32)
        u = jnp.dot(xn, wu_ref[...], preferred_element_type=F32)
        hid = (g * jax.nn.sigmoid(g) * u * 0.5).astype(BF16)
        return base + jnp.dot(hid, wd_ref[...], preferred_element_type=F32)

    @pl.when(pl.program_id(1) == 0)
    def _():
        for r in range(0, x_ref.shape[0], NORM_ROWS):
            rows = slice(r, min(r + NORM_ROWS, x_ref.shape[0]))
            x = x_ref[rows, :]
            xn = _rms(x, gain_ref[...]).astype(BF16)
            xn_ref[rows, :] = xn
            o_ref[rows, :] = add_tile(x, xn)

    @pl.when(pl.program_id(1) > 0)
    def _():
        o_ref[...] = add_tile(o_ref[...], xn_ref[...])


def _ffn_big_kernel(x_hbm, gain_ref, wg_ref, wu_ref, wd_ref, o_ref, xn_ref, sem, *, bm):
    def add_tile(base, xn):
        g = jnp.dot(xn, wg_ref[...], preferred_element_type=F32)
        u = jnp.dot(xn, wu_ref[...], preferred_element_type=F32)
        hid = (g * jax.nn.sigmoid(g) * u * 0.5).astype(BF16)
        return base + jnp.dot(hid, wd_ref[...], preferred_element_type=F32)

    @pl.when(pl.program_id(1) == 0)
    def _():
        row0 = pl.program_id(0) * bm
        starts = range(0, bm, NORM_ROWS)

        def chunk_copy(c, r):
            return pltpu.make_async_copy(x_hbm.at[pl.ds(row0 + r, NORM_ROWS), :],
                                         o_ref.at[pl.ds(r, NORM_ROWS), :], sem.at[c])

        for c, r in enumerate(starts):
            chunk_copy(c, r).start()
        for c, r in enumerate(starts):
            chunk_copy(c, r).wait()
            rows = slice(r, r + NORM_ROWS)
            x = o_ref[rows, :]
            xn = _rms(x, gain_ref[...]).astype(BF16)
            xn_ref[rows, :] = xn
            o_ref[rows, :] = add_tile(x, xn)

    @pl.when(pl.program_id(1) > 0)
    def _():
        o_ref[...] = add_tile(o_ref[...], xn_ref[...])


def _ffn_big(x, gain, w_gu, w_down, *, bm=1024, bf=256):
    t, d = x.shape
    f = w_down.shape[0]
    bm, bf = _tile(t, bm), _tile(f, bf)
    assert bm % NORM_ROWS == 0
    nf = f // bf
    return pl.pallas_call(
        functools.partial(_ffn_big_kernel, bm=bm),
        grid=(t // bm, nf),
        in_specs=[
            pl.BlockSpec(memory_space=pl.ANY),
            pl.BlockSpec((1, d), lambda i, j: (0, 0)),
            pl.BlockSpec((d, bf), lambda i, j: (0, j)),
            pl.BlockSpec((d, bf), lambda i, j: (0, j + nf)),
            pl.BlockSpec((bf, d), lambda i, j: (j, 0)),
        ],
        out_specs=pl.BlockSpec((bm, d), lambda i, j: (i, 0)),
        out_shape=jax.ShapeDtypeStruct((t, d), F32),
        scratch_shapes=[pltpu.VMEM((bm, d), BF16), pltpu.SemaphoreType.DMA((bm // NORM_ROWS,))],
        compiler_params=_params("parallel", "arbitrary"),
        name="ffn_big",
    )(x, gain.reshape(1, d), w_gu, w_gu, w_down)


def _cast_specs(ws, layer, gi, gj, rc):
    ins, outs = [], []
    for w in ws:
        _, r, c = w.shape
        br = _ceil_to(pl.cdiv(r, gi), BF16_ROWS)
        bc = _ceil_to(pl.cdiv(c, gj), LANES)
        last = pl.cdiv(c, bc) - 1
        assert (gi - 1) * br < r <= gi * br and last < gj

        def block(*g, last=last):
            i, j = rc(*g)
            return i, jnp.minimum(j, last)

        ins.append(pl.BlockSpec((None, br, bc), lambda *g, block=block: (layer, *block(*g))))
        outs.append(pl.BlockSpec((br, bc), block))
    return ins, outs


def _cast_block(cast_in, cast_out):
    for src, dst in zip(cast_in, cast_out):
        dst[...] = src[...].astype(BF16)


def _ffn(x, gain, w_gu, w_down, *, bm=512, bf=256):
    t, d = x.shape
    f = w_down.shape[0]
    bm, bf = _tile(t, bm), _tile(f, bf)
    nf = f // bf
    return pl.pallas_call(
        _ffn_kernel,
        grid=(t // bm, nf),
        in_specs=[
            pl.BlockSpec((bm, d), lambda i, j: (i, 0)),
            pl.BlockSpec((1, d), lambda i, j: (0, 0)),
            pl.BlockSpec((d, bf), lambda i, j: (0, j)),
            pl.BlockSpec((d, bf), lambda i, j: (0, j + nf)),
            pl.BlockSpec((bf, d), lambda i, j: (j, 0)),
        ],
        out_specs=pl.BlockSpec((bm, d), lambda i, j: (i, 0)),
        out_shape=jax.ShapeDtypeStruct((t, d), F32),
        scratch_shapes=[pltpu.VMEM((bm, d), BF16)],
        compiler_params=_params("parallel", "arbitrary"),
        name="ffn",
    )(x, gain.reshape(1, d), w_gu, w_gu, w_down)


def _norm_matmul_kernel(x_ref, gain_ref, w_ref, o_ref, xn_ref):
    @pl.when(pl.program_id(1) == 0)
    def _():
        for r in range(0, x_ref.shape[0], NORM_ROWS):
            rows = slice(r, min(r + NORM_ROWS, x_ref.shape[0]))
            xn = _rms(x_ref[rows, :], gain_ref[...]).astype(BF16)
            xn_ref[rows, :] = xn
            o_ref[rows, :] = jnp.dot(xn, w_ref[...], preferred_element_type=F32)

    @pl.when(pl.program_id(1) > 0)
    def _():
        o_ref[...] = jnp.dot(xn_ref[...], w_ref[...], preferred_element_type=F32)


def _norm_matmul(x, gain, w, *, bm=1024, bn=1024, name):
    t, d = x.shape
    n = w.shape[1]
    bm, bn = _tile(t, bm), min(bn, n)
    return pl.pallas_call(
        _norm_matmul_kernel,
        grid=(t // bm, pl.cdiv(n, bn)),
        in_specs=[
            pl.BlockSpec((bm, d), lambda i, j: (i, 0), pipeline_mode=pl.Buffered(1)),
            pl.BlockSpec((1, d), lambda i, j: (0, 0)),
            pl.BlockSpec((d, bn), lambda i, j: (0, j)),
        ],
        out_specs=pl.BlockSpec((bm, bn), lambda i, j: (i, j)),
        out_shape=jax.ShapeDtypeStruct((t, n), F32),
        scratch_shapes=[pltpu.VMEM((bm, d), BF16)],
        compiler_params=_params("parallel", "arbitrary"),
        name=name,
    )(x, gain.reshape(1, d), w)


def _pool_kernel(zc_ref, zp_ref, wp_ref, sc_ref, o_ref, ext_ref, *, bm, c):
    i = pl.program_id(1)
    ext_ref[0:POOL_HALO, :] = jnp.where(i > 0, zp_ref[...], 0.0)
    ext_ref[POOL_HALO:POOL_HALO + bm, :] = zc_ref[...]
    t = i * bm + lax.broadcasted_iota(jnp.int32, (bm, 1), 0)
    for g, w in enumerate(POOL_WINDOWS):
        cols = slice(g * c, (g + 1) * c)
        x = zc_ref[:, cols]
        acc = x
        for k in range(1, w):
            acc = acc + ext_ref[POOL_HALO - k:POOL_HALO - k + bm, cols]
        cnt = jnp.minimum(t + 1, w).astype(F32)
        mixed = (acc / cnt - x).astype(BF16)
        y = jnp.dot(mixed, wp_ref[g], preferred_element_type=F32) * sc_ref[:, cols]
        o_ref[:, cols] = y.astype(BF16)


def _pool(z_main, w_pool, pool_scale, batch, seq, *, bm=512):
    g, c, _ = w_pool.shape
    pw = g * c
    bm = _tile(seq, bm)
    nb = seq // bm
    halo_blocks = bm // POOL_HALO
    return pl.pallas_call(
        functools.partial(_pool_kernel, bm=bm, c=c),
        grid=(batch, nb),
        in_specs=[
            pl.BlockSpec((bm, pw), lambda b, i: (b * nb + i, 0)),
            pl.BlockSpec((POOL_HALO, pw),
                         lambda b, i: (jnp.maximum((b * nb + i) * halo_blocks - 1, 0), 0)),
            pl.BlockSpec((g, c, c), lambda b, i: (0, 0, 0)),
            pl.BlockSpec((1, pw), lambda b, i: (0, 0)),
        ],
        out_specs=pl.BlockSpec((bm, pw), lambda b, i: (b * nb + i, 0)),
        out_shape=jax.ShapeDtypeStruct((batch * seq, pw), BF16),
        scratch_shapes=[pltpu.VMEM((POOL_HALO + bm, pw), F32)],
        compiler_params=_params("parallel", "parallel"),
        name="pool",
    )(z_main, z_main, w_pool, pool_scale.reshape(1, pw))


def _rope_tables(pos, inv):
    ang = pos.astype(F32) * inv
    cos, sin = jnp.cos(ang), jnp.sin(ang)
    lane = lax.broadcasted_iota(jnp.int32, ang.shape, 1)
    half = QK_ROPE // 2
    c = jnp.where(lane < QK_ROPE, cos, 0.0)
    sa = jnp.where(lane < half, -sin, 0.0)
    sb = jnp.where((lane >= half) & (lane < QK_ROPE), sin, 0.0)
    return c, sa, sb


def _rope_norm(r, gain, c, sa, sb):
    ms = jnp.sum(r * r, axis=-1, keepdims=True) * (1.0 / QK_ROPE)
    r = r * lax.rsqrt(ms + EPS) * gain
    half = QK_ROPE // 2
    return r * c + pltpu.roll(r, LANES - half, 1) * sa + pltpu.roll(r, half, 1) * sb


def _qkv_kernel(zq_ref, zkv_ref, zkr_ref, pos_ref, post_ref, inv_ref, invt_ref, gql_ref, gkvl_ref, gqt_ref,
                gkn_ref, gkr_ref, wuqt_ref, wuk_ref, wuvt_ref, qt_ref, k_ref, vt_ref,
                cqt_ref, ckv_ref, ckvt_ref, kr_ref, cost_ref, sint_ref):
    half = QK_ROPE // 2
    r1, r2, r3 = QK_NOPE, QK_NOPE + half, QK_HEAD

    @pl.when(pl.program_id(1) == 0)
    def _():
        cqt_ref[...] = _rms(zq_ref[...], gql_ref[...]).T.astype(BF16)
        ckv = _rms(zkv_ref[...], gkvl_ref[...])
        ckv_ref[...] = ckv.astype(BF16)
        ckvt_ref[...] = ckv.T.astype(BF16)
        c, sa, sb = _rope_tables(pos_ref[...], inv_ref[...])
        lane = lax.broadcasted_iota(jnp.int32, zkr_ref.shape, 1)
        zkr = jnp.where(lane < QK_ROPE, zkr_ref[...], 0.0)
        kr_ref[...] = _rope_norm(zkr, gkr_ref[...], c, sa, sb).astype(BF16)
        ang = post_ref[...].astype(F32) * invt_ref[...]
        cost_ref[...] = jnp.cos(ang)
        sint_ref[...] = jnp.sin(ang)

    bm = qt_ref.shape[-1]
    for s in range(qt_ref.shape[0]):
        q = jnp.dot(wuqt_ref[s], cqt_ref[...], preferred_element_type=F32)
        nope = q[:r1]
        rs = lax.rsqrt(jnp.mean(nope * nope, axis=0, keepdims=True) + EPS)
        qt_ref[s, :r1, :] = (nope * rs * gqt_ref[:r1, :]).astype(BF16)
        x1, x2 = q[r1:r2], q[r2:r3]
        ms = (jnp.sum(x1 * x1, axis=0, keepdims=True)
              + jnp.sum(x2 * x2, axis=0, keepdims=True)) * (1.0 / QK_ROPE)
        rs = lax.rsqrt(ms + EPS)
        y1, y2 = x1 * rs * gqt_ref[r1:r2, :], x2 * rs * gqt_ref[r2:r3, :]
        cos, sin = cost_ref[...], sint_ref[...]
        qt_ref[s, r1:r2, :] = (y1 * cos - y2 * sin).astype(BF16)
        qt_ref[s, r2:r3, :] = (y2 * cos + y1 * sin).astype(BF16)
        qt_ref[s, r3:, :] = jnp.zeros((QK_PAD - r3, bm), BF16)

        k = jnp.dot(ckv_ref[...], wuk_ref[s], preferred_element_type=F32)
        k_ref[s, :, :QK_NOPE] = _rms(k, gkn_ref[...]).astype(BF16)
        k_ref[s, :, QK_NOPE:] = kr_ref[...]
        vt = jnp.dot(wuvt_ref[s], ckvt_ref[...], preferred_element_type=F32)
        vt_ref[s, :V_HEAD, :] = vt.astype(BF16)
        vt_ref[s, V_HEAD:, :] = jnp.ones((V_ROWS - V_HEAD, bm), BF16)


def _qkv(z, pos, inv, gql, gkvl, gqt, gkn, gkr, wuqt, wuk, wuvt, batch, seq, off_q, off_kv, *, bm=512, hp=8):
    h, _, ql = wuqt.shape
    hp = min(hp, h)
    kvl = wuk.shape[1]
    bm = _tile(seq, bm)
    nb = seq // bm
    t = batch * seq
    half = QK_ROPE // 2
    assert off_q % ql == 0 and off_kv % kvl == 0 and (off_kv + kvl) % LANES == 0 and h % hp == 0
    const = lambda i, hh: (0, 0)
    head = lambda i, hh: (hh, 0, 0)
    out_map = lambda i, hh: (i // nb, hh, i % nb, 0)
    out_map_t = lambda i, hh: (i // nb, hh, i % nb, 0, 0)
    invt = jnp.broadcast_to(inv[0, :half].reshape(half, 1), (half, bm))
    gqt = jnp.broadcast_to(gqt.reshape(QK_PAD, 1), (QK_PAD, bm))
    return pl.pallas_call(
        _qkv_kernel,
        grid=(t // bm, h // hp),
        in_specs=[
            pl.BlockSpec((bm, ql), lambda i, hh: (i, off_q // ql)),
            pl.BlockSpec((bm, kvl), lambda i, hh: (i, off_kv // kvl)),
            pl.BlockSpec((bm, LANES), lambda i, hh: (i, (off_kv + kvl) // LANES)),
            pl.BlockSpec((bm, 1), lambda i, hh: (i, 0)),
            pl.BlockSpec((1, bm), lambda i, hh: (0, i)),
            pl.BlockSpec((1, LANES), const),
            pl.BlockSpec((half, bm), const),
            pl.BlockSpec((1, ql), const),
            pl.BlockSpec((1, kvl), const),
            pl.BlockSpec((QK_PAD, bm), const),
            pl.BlockSpec((1, QK_NOPE), const),
            pl.BlockSpec((1, LANES), const),
            pl.BlockSpec((hp, QK_PAD, ql), head),
            pl.BlockSpec((hp, kvl, QK_NOPE), head),
            pl.BlockSpec((hp, V_HEAD, kvl), head),
        ],
        out_specs=[
            pl.BlockSpec((None, hp, None, QK_PAD, bm), out_map_t),
            pl.BlockSpec((None, hp, bm, QK_PAD), out_map),
            pl.BlockSpec((None, hp, None, V_ROWS, bm), out_map_t),
        ],
        out_shape=[
            jax.ShapeDtypeStruct((batch, h, nb, QK_PAD, bm), BF16),
            jax.ShapeDtypeStruct((batch, h, seq, QK_PAD), BF16),
            jax.ShapeDtypeStruct((batch, h, nb, V_ROWS, bm), BF16),
        ],
        scratch_shapes=[
            pltpu.VMEM((ql, bm), BF16),
            pltpu.VMEM((bm, kvl), BF16),
            pltpu.VMEM((kvl, bm), BF16),
            pltpu.VMEM((bm, LANES), BF16),
            pltpu.VMEM((half, bm), F32),
            pltpu.VMEM((half, bm), F32),
        ],
        compiler_params=_params("parallel", "arbitrary"),
        name="qkv_prep",
    )(z, z, z, pos.reshape(t, 1), pos.reshape(1, t), inv, invt, gql, gkvl, gqt, gkn, gkr, wuqt, wuk, wuvt)


def _flash_kernel(qt_ref, k_ref, vt_ref, *refs, blk, chunk, n_cast):
    cast_in, o_ref, cast_out = refs[:n_cast], refs[n_cast], refs[n_cast + 1:2 * n_cast + 1]
    sa_ref, sb_ref, m_ref, acc_ref = refs[2 * n_cast + 1:]
    _cast_block(cast_in, cast_out)

    qi = pl.program_id(2)
    nch = 2 * blk // chunk
    per_blk = blk // chunk
    m_ref[...] = jnp.full(m_ref.shape, -jnp.inf, F32)
    acc_ref[...] = jnp.zeros(acc_ref.shape, F32)

    def visibility(c, diag):
        if diag is None or c * chunk >= (diag + 1) * blk:
            return "all"
        return "none" if (c + 1) * chunk <= diag * blk else "some"

    def scores(kj, s_ref, diag=None):
        start = pl.multiple_of(kj * blk, blk)
        k = k_ref[pl.ds(start, blk), :]
        for c in range(nch):
            if visibility(c, diag) != "none":
                qt = qt_ref[c // per_blk, :, (c % per_blk) * chunk:(c % per_blk + 1) * chunk]
                s_ref[c] = jnp.dot(k, qt, preferred_element_type=F32)

    def update(kj, s_ref, diag=None):
        vt = vt_ref[kj]
        for c in range(nch):
            see = visibility(c, diag)
            if see == "none":
                continue
            cols = slice(c * chunk, (c + 1) * chunk)
            st = s_ref[c]
            if see == "some":
                key = lax.broadcasted_iota(jnp.int32, st.shape, 0) + diag * blk
                qry = lax.broadcasted_iota(jnp.int32, st.shape, 1) + c * chunk
                st = jnp.where(key <= qry, st, -jnp.inf)
            m_old = m_ref[:, cols]
            m_new = jnp.maximum(m_old, jnp.max(st, axis=0, keepdims=True))
            p = jnp.exp2(st - m_new).astype(BF16)
            alpha = jnp.exp2(m_old - m_new)
            acc_ref[:, cols] = alpha * acc_ref[:, cols] + jnp.dot(vt, p, preferred_element_type=F32)
            m_ref[:, cols] = m_new

    scores(0, sa_ref)

    def pair(t):
        scores(2 * t + 1, sb_ref)
        update(2 * t, sa_ref)
        scores(2 * t + 2, sa_ref)
        update(2 * t + 1, sb_ref)

    def trip(u, carry):
        for t in range(PAIRS_PER_TRIP):
            pair(PAIRS_PER_TRIP * u + t)
        return carry

    lax.fori_loop(0, qi // PAIRS_PER_TRIP, trip, 0)
    done = qi // PAIRS_PER_TRIP * PAIRS_PER_TRIP
    for rest in range(PAIRS_PER_TRIP):
        @pl.when(qi - done == rest)
        def _():
            for t in range(rest):
                pair(done + t)
            scores(2 * qi + 1, sb_ref, diag=1)
            update(2 * qi, sa_ref, diag=0)
            update(2 * qi + 1, sb_ref, diag=1)
            o_ref[...] = (acc_ref[:V_HEAD, :] / acc_ref[V_HEAD:V_HEAD + 1, :]).T.astype(o_ref.dtype)


def _flash(qt, k, vt, cast=(), layer=0, *, chunk=256):
    b, h, nb, _, blk = qt.shape
    s = nb * blk
    chunk = min(chunk, blk)
    assert nb % 2 == 0
    cast_in, cast_out = _cast_specs(cast, layer, b * h, nb // 2, lambda bi, hi, qi: (bi * h + hi, qi))
    return pl.pallas_call(
        functools.partial(_flash_kernel, blk=blk, chunk=chunk, n_cast=len(cast)),
        grid=(b, h, nb // 2),
        in_specs=[
            pl.BlockSpec((None, None, 2, QK_PAD, blk), lambda bi, hi, qi: (bi, hi, qi, 0, 0)),
            pl.BlockSpec((None, None, s, QK_PAD), lambda bi, hi, qi: (bi, hi, 0, 0)),
            pl.BlockSpec((None, None, nb, V_ROWS, blk), lambda bi, hi, qi: (bi, hi, 0, 0, 0)),
            *cast_in,
        ],
        out_specs=[pl.BlockSpec((None, 2 * blk, V_HEAD), lambda bi, hi, qi: (bi, qi, hi)), *cast_out],
        out_shape=[jax.ShapeDtypeStruct((b, s, h * V_HEAD), BF16)]
        + [jax.ShapeDtypeStruct(w.shape[1:], BF16) for w in cast],
        scratch_shapes=[
            pltpu.VMEM((2 * blk // chunk, blk, chunk), F32),
            pltpu.VMEM((2 * blk // chunk, blk, chunk), F32),
            pltpu.VMEM((1, 2 * blk), F32),
            pltpu.VMEM((V_ROWS, 2 * blk), F32),
        ],
        compiler_params=_params("parallel", "parallel", "arbitrary"),
        name="mla_flash",
    )(qt, k, vt, *cast)


def _merge_kernel(a_ref, b_ref, wa_ref, wb_ref, gp_ref, gpx_ref, gm_ref, gmx_ref, o_ref, *, shift):
    bn = o_ref.shape[1]

    def gate(lo_ref, hi_ref):
        g = jnp.concatenate([lo_ref[...], hi_ref[...]], axis=1)[:, shift:shift + bn]
        return jax.nn.sigmoid(g)

    pa = jnp.dot(a_ref[...], wa_ref[...], preferred_element_type=F32)
    pb = jnp.dot(b_ref[...], wb_ref[...], preferred_element_type=F32)
    o_ref[...] = (gate(gp_ref, gpx_ref) * pa + gate(gm_ref, gmx_ref) * pb).astype(BF16)


def _merge(a, b, wa, wb, z, gate_off, *, bm=1024, bn=512):
    t, pw = a.shape
    mw = b.shape[1]
    d = wa.shape[1]
    shift = gate_off % LANES
    base = gate_off - shift
    bm, bn = _tile(t, bm), _tile(math.gcd(d, base), bn)
    lanes_per_tile = bn // LANES
    lo = lambda off: (lambda i, j: (i, off // bn + j))
    hi = lambda off: (lambda i, j: (i, (off + bn) // LANES + j * lanes_per_tile))
    return pl.pallas_call(
        functools.partial(_merge_kernel, shift=shift),
        grid=(t // bm, d // bn),
        in_specs=[
            pl.BlockSpec((bm, pw), lambda i, j: (i, 0)),
            pl.BlockSpec((bm, mw), lambda i, j: (i, 0)),
            pl.BlockSpec((pw, bn), lambda i, j: (0, j)),
            pl.BlockSpec((mw, bn), lambda i, j: (0, j)),
            pl.BlockSpec((bm, bn), lo(base)),
            pl.BlockSpec((bm, LANES), hi(base)),
            pl.BlockSpec((bm, bn), lo(base + d)),
            pl.BlockSpec((bm, LANES), hi(base + d)),
        ],
        out_specs=pl.BlockSpec((bm, bn), lambda i, j: (i, j)),
        out_shape=jax.ShapeDtypeStruct((t, d), BF16),
        compiler_params=_params("parallel", "arbitrary"),
        name="branch_merge",
    )(a, b, wa, wb, z, z, z, z)


def _proj_residual_kernel(a_ref, w_ref, h_ref, o_ref):
    o_ref[...] = h_ref[...] + jnp.dot(a_ref[...], w_ref[...], preferred_element_type=F32)


def _proj_residual(a, w, h, *, bm=1024, bn=1024):
    t, kdim = a.shape
    d = w.shape[1]
    bm, bn = _tile(t, bm), _tile(d, bn)
    return pl.pallas_call(
        _proj_residual_kernel,
        grid=(t // bm, d // bn),
        in_specs=[
            pl.BlockSpec((bm, kdim), lambda i, j: (i, 0)),
            pl.BlockSpec((kdim, bn), lambda i, j: (0, j)),
            pl.BlockSpec((bm, bn), lambda i, j: (i, j)),
        ],
        out_specs=pl.BlockSpec((bm, bn), lambda i, j: (i, j)),
        out_shape=jax.ShapeDtypeStruct((t, d), F32),
        compiler_params=_params("parallel", "arbitrary"),
        name="out_proj",
    )(a, w, h)


def _mem_kv_kernel(mem_ref, gain_ref, w_ref, gk_ref, k_ref, v_ref):
    mn = _rms(mem_ref[...], gain_ref[...]).astype(BF16)
    kv = jnp.dot(mn, w_ref[...], preferred_element_type=F32)
    xw = X_HEADS * X_HEAD_DIM
    for hh in range(X_HEADS):
        cols = slice(hh * X_HEAD_DIM, (hh + 1) * X_HEAD_DIM)
        k_ref[:, cols] = _rms(kv[:, cols], gk_ref[...]).astype(BF16)
    v_ref[...] = kv[:, xw:].astype(BF16)


def _mem_kv(mem, gain, w_xkv, gk):
    b, m, d = mem.shape
    xw = X_HEADS * X_HEAD_DIM
    return pl.pallas_call(
        _mem_kv_kernel,
        grid=(b,),
        in_specs=[
            pl.BlockSpec((None, m, d), lambda bi: (bi, 0, 0)),
            pl.BlockSpec((1, d), lambda bi: (0, 0)),
            pl.BlockSpec((d, 2 * xw), lambda bi: (0, 0)),
            pl.BlockSpec((1, X_HEAD_DIM), lambda bi: (0, 0)),
        ],
        out_specs=[
            pl.BlockSpec((None, m, xw), lambda bi: (bi, 0, 0)),
            pl.BlockSpec((None, m, xw), lambda bi: (bi, 0, 0)),
        ],
        out_shape=[
            jax.ShapeDtypeStruct((b, m, xw), BF16),
            jax.ShapeDtypeStruct((b, m, xw), BF16),
        ],
        compiler_params=_params("parallel"),
        name="mem_kv",
    )(mem, gain.reshape(1, d), w_xkv, gk)


def _cross_kernel(h_ref, gain_ref, wq_ref, gq_ref, k_ref, v_ref, wo_ref, o_ref, xo_ref):
    h = h_ref[...]
    uq = _rms(h, gain_ref[...]).astype(BF16)
    xq = jnp.dot(uq, wq_ref[...], preferred_element_type=F32)
    scale = X_HEAD_DIM ** -0.5
    for hh in range(X_HEADS):
        cols = slice(hh * X_HEAD_DIM, (hh + 1) * X_HEAD_DIM)
        qh = _rms(xq[:, cols], gq_ref[...]).astype(BF16)
        s = lax.dot_general(qh, k_ref[:, cols], (((1,), (1,)), ((), ())),
                            preferred_element_type=F32) * scale
        p = jnp.exp(s - jnp.max(s, axis=-1, keepdims=True))
        p = (p / jnp.sum(p, axis=-1, keepdims=True)).astype(BF16)
        xo_ref[:, cols] = jnp.dot(p, v_ref[:, cols], preferred_element_type=F32).astype(BF16)
    o_ref[...] = h + jnp.dot(xo_ref[...], wo_ref[...], preferred_element_type=F32)


def _cross(h, gain, wq, gq, xk, xv, wo, seq, *, bm=512):
    t, d = h.shape
    m = xk.shape[1]
    xw = X_HEADS * X_HEAD_DIM
    bm = _tile(seq, bm)
    nb = seq // bm
    return pl.pallas_call(
        _cross_kernel,
        grid=(t // bm,),
        in_specs=[
            pl.BlockSpec((bm, d), lambda i: (i, 0)),
            pl.BlockSpec((1, d), lambda i: (0, 0)),
            pl.BlockSpec((d, xw), lambda i: (0, 0)),
            pl.BlockSpec((1, X_HEAD_DIM), lambda i: (0, 0)),
            pl.BlockSpec((None, m, xw), lambda i: (i // nb, 0, 0)),
            pl.BlockSpec((None, m, xw), lambda i: (i // nb, 0, 0)),
            pl.BlockSpec((xw, d), lambda i: (0, 0)),
        ],
        out_specs=pl.BlockSpec((bm, d), lambda i: (i, 0)),
        out_shape=jax.ShapeDtypeStruct((t, d), F32),
        scratch_shapes=[pltpu.VMEM((bm, xw), BF16)],
        compiler_params=_params("parallel"),
        name="cross_attn",
    )(h, gain.reshape(1, d), wq, gq, xk, xv, wo)


def _pad_lanes(v, width):
    return jnp.pad(v, (0, width - v.shape[0])).reshape(1, width)


def kernel(x, mem, positions, ffn1_norm, ffn1_w_gu, ffn1_w_down, mix_norm, w_in, w_pool, pool_scale,
           q_latent_norm, kv_latent_norm, w_uq, w_ukv, q_nope_norm, k_nope_norm, q_rope_norm,
           k_rope_norm, w_branch_pool, w_branch_mla, w_out, x_norm, mem_norm, w_xq, w_xkv, xq_norm,
           xk_norm, w_xo, ffn2_norm, ffn2_w_gu, ffn2_w_down):
    batch, seq, d = x.shape
    t = batch * seq
    depth = w_in.shape[0]
    pw = w_branch_pool.shape[1]
    ql = q_latent_norm.shape[1]
    kvl = kv_latent_norm.shape[1]
    heads = w_uq.shape[2] // QK_HEAD

    half = QK_ROPE // 2
    inv = 1.0 / (ROPE_THETA ** (jnp.arange(half, dtype=F32) * (2.0 / QK_ROPE)))
    inv = _pad_lanes(jnp.concatenate([inv, inv]), LANES)
    c2 = QK_HEAD ** -0.5 * math.log2(math.e)

    h = x.reshape(t, d)
    for l in range(depth):
        h = _ffn_big(h, ffn1_norm[l], ffn1_w_gu[l].astype(BF16), ffn1_w_down[l].astype(BF16))

        off_kv = pw + ql
        off_gate = off_kv + kvl + QK_ROPE
        z = _norm_matmul(h, mix_norm[l], w_in[l].astype(BF16), bn=IN_TILE, name="in_proj")

        a_out = _pool(z, w_pool[l].astype(BF16), pool_scale[l], batch, seq)

        wuq = jnp.pad(w_uq[l].reshape(ql, heads, QK_HEAD), ((0, 0), (0, 0), (0, QK_PAD - QK_HEAD)))
        wuqt = wuq.transpose(1, 2, 0).astype(BF16)
        wukv = w_ukv[l].reshape(kvl, heads, QK_NOPE + V_HEAD)
        wuk = wukv[:, :, :QK_NOPE].transpose(1, 0, 2).astype(BF16)
        wuvt = wukv[:, :, QK_NOPE:].transpose(1, 2, 0).astype(BF16)
        gqt = jnp.concatenate([q_nope_norm[l], q_rope_norm[l], jnp.zeros((QK_PAD - QK_HEAD,), F32)]) * c2
        qt, k, vt = _qkv(z, positions, inv,
                         q_latent_norm[l].reshape(1, ql), kv_latent_norm[l].reshape(1, kvl), gqt,
                         k_nope_norm[l].reshape(1, QK_NOPE), _pad_lanes(k_rope_norm[l], LANES),
                         wuqt, wuk, wuvt, batch, seq, pw, off_kv)
        later = (ffn2_w_gu, ffn2_w_down, w_out, w_branch_pool, w_branch_mla)
        b_out, w_gu2, w_down2, w_out_b, w_bp, w_bm = _flash(qt, k, vt, later, l)
        b_out = b_out.reshape(t, heads * V_HEAD)

        merged = _merge(a_out, b_out, w_bp, w_bm, z, off_gate)
        h = _proj_residual(merged, w_out_b, h)

        xk, xv = _mem_kv(mem, mem_norm[l], w_xkv[l].astype(BF16), xk_norm[l].reshape(1, X_HEAD_DIM))
        h = _cross(h, x_norm[l], w_xq[l].astype(BF16), xq_norm[l].reshape(1, X_HEAD_DIM), xk, xv,
                   w_xo[l].astype(BF16), seq)

        h = _ffn(h, ffn2_norm[l], w_gu2, w_down2)
    return h.reshape(batch, seq, d)
```

```python
import functools
import math

import jax
import jax.numpy as jnp
from jax import lax
from jax.experimental import pallas as pl
from jax.experimental.pallas import tpu as pltpu

F32 = jnp.float32
BF16 = jnp.bfloat16

POOL_WINDOWS = (2, 4, 8, 16)
POOL_HALO = 16
QK_NOPE = 128
QK_ROPE = 64
QK_HEAD = QK_NOPE + QK_ROPE
QK_PAD = 256
V_HEAD = 128
V_ROWS = V_HEAD + 16
X_HEADS = 4
X_HEAD_DIM = 128
ROPE_THETA = 10000.0
EPS = 1e-6
LANES = 128
BF16_ROWS = 16
VMEM_LIMIT = 62 * 1024 * 1024
NORM_ROWS = 256
PAIRS_PER_TRIP = 4
IN_TILE = 1024


def _params(*sem):
    return pltpu.CompilerParams(dimension_semantics=sem, vmem_limit_bytes=VMEM_LIMIT)


def _tile(n, pref):
    if n <= pref:
        return n
    t = (pref // LANES) * LANES
    while t >= LANES:
        if n % t == 0:
            return t
        t -= LANES
    raise ValueError(f"no lane-aligned tile of {n} below {pref}")


def _ceil_to(x, m):
    return -(-x // m) * m


def _rms(x, gain):
    ms = jnp.mean(x * x, axis=-1, keepdims=True)
    return x * lax.rsqrt(ms + EPS) * gain


def _row_chunk_copies(src_hbm, dst_ref, sem, row0):
    return [pltpu.make_async_copy(src_hbm.at[pl.ds(row0 + r, NORM_ROWS), :],
                                  dst_ref.at[pl.ds(r, NORM_ROWS), :], sem.at[c])
            for c, r in enumerate(range(0, dst_ref.shape[0], NORM_ROWS))]


def _ffn_kernel(x_hbm, gain_ref, wg_ref, wu_ref, wd_ref, o_ref, xn_ref, sem):
    def add_tile(base, xn):
        g = jnp.dot(xn, wg_ref[...], preferred_element_type=F32)
        u = jnp.dot(xn, wu_ref[...], preferred_element_type=F32)
        hid = (g * jax.nn.sigmoid(g) * u * 0.5).astype(BF16)
        return base + jnp.dot(hid, wd_ref[...], preferred_element_type=F32)

    @pl.when(pl.program_id(1) == 0)
    def _():
        copies = _row_chunk_copies(x_hbm, o_ref, sem, pl.program_id(0) * o_ref.shape[0])
        for copy in copies:
            copy.start()
        for c, copy in enumerate(copies):
            copy.wait()
            rows = slice(c * NORM_ROWS, (c + 1) * NORM_ROWS)
            x = o_ref[rows, :]
            xn = _rms(x, gain_ref[...]).astype(BF16)
            xn_ref[rows, :] = xn
            o_ref[rows, :] = add_tile(x, xn)

    @pl.when(pl.program_id(1) > 0)
    def _():
        o_ref[...] = add_tile(o_ref[...], xn_ref[...])


def _ffn(x, gain, w_gu, w_down, *, bm=1024, bf=256):
    t, d = x.shape
    f = w_down.shape[0]
    bm, bf = _tile(t, bm), _tile(f, bf)
    assert bm % NORM_ROWS == 0
    nf = f // bf
    return pl.pallas_call(
        _ffn_kernel,
        grid=(t // bm, nf),
        in_specs=[
            pl.BlockSpec(memory_space=pl.ANY),
            pl.BlockSpec((1, d), lambda i, j: (0, 0)),
            pl.BlockSpec((d, bf), lambda i, j: (0, j)),
            pl.BlockSpec((d, bf), lambda i, j: (0, j + nf)),
            pl.BlockSpec((bf, d), lambda i, j: (j, 0)),
        ],
        out_specs=pl.BlockSpec((bm, d), lambda i, j: (i, 0)),
        out_shape=jax.ShapeDtypeStruct((t, d), F32),
        scratch_shapes=[pltpu.VMEM((bm, d), BF16), pltpu.SemaphoreType.DMA((bm // NORM_ROWS,))],
        compiler_params=_params("parallel", "arbitrary"),
        name="ffn",
    )(x, gain.reshape(1, d), w_gu, w_gu, w_down)


def _cast_specs(ws, layer, gi, gj, rc):
    ins, outs = [], []
    for w in ws:
        _, r, c = w.shape
        br = _ceil_to(pl.cdiv(r, gi), BF16_ROWS)
        bc = _ceil_to(pl.cdiv(c, gj), LANES)
        last = pl.cdiv(c, bc) - 1
        assert (gi - 1) * br < r <= gi * br and last < gj

        def block(*g, last=last):
            i, j = rc(*g)
            return i, jnp.minimum(j, last)

        ins.append(pl.BlockSpec((None, br, bc), lambda *g, block=block: (layer, *block(*g))))
        outs.append(pl.BlockSpec((br, bc), block))
    return ins, outs


def _cast_block(cast_in, cast_out):
    for src, dst in zip(cast_in, cast_out):
        dst[...] = src[...].astype(BF16)


def _norm_matmul_kernel(x_hbm, gain_ref, w_ref, o_ref, xn_ref, x_ref, sem):
    @pl.when(pl.program_id(1) == 0)
    def _():
        copies = _row_chunk_copies(x_hbm, x_ref, sem, pl.program_id(0) * x_ref.shape[0])
        for copy in copies:
            copy.start()
        for c, copy in enumerate(copies):
            copy.wait()
            rows = slice(c * NORM_ROWS, (c + 1) * NORM_ROWS)
            xn = _rms(x_ref[rows, :], gain_ref[...]).astype(BF16)
            xn_ref[rows, :] = xn
            o_ref[rows, :] = jnp.dot(xn, w_ref[...], preferred_element_type=F32)

    @pl.when(pl.program_id(1) > 0)
    def _():
        o_ref[...] = jnp.dot(xn_ref[...], w_ref[...], preferred_element_type=F32)


def _norm_matmul(x, gain, w, *, bm=1024, bn=1024, name):
    t, d = x.shape
    n = w.shape[1]
    bm, bn = _tile(t, bm), min(bn, n)
    assert bm % NORM_ROWS == 0
    return pl.pallas_call(
        _norm_matmul_kernel,
        grid=(t // bm, pl.cdiv(n, bn)),
        in_specs=[
            pl.BlockSpec(memory_space=pl.ANY),
            pl.BlockSpec((1, d), lambda i, j: (0, 0)),
            pl.BlockSpec((d, bn), lambda i, j: (0, j)),
        ],
        out_specs=pl.BlockSpec((bm, bn), lambda i, j: (i, j)),
        out_shape=jax.ShapeDtypeStruct((t, n), F32),
        scratch_shapes=[pltpu.VMEM((bm, d), BF16), pltpu.VMEM((bm, d), F32),
                        pltpu.SemaphoreType.DMA((bm // NORM_ROWS,))],
        compiler_params=_params("parallel", "arbitrary"),
        name=name,
    )(x, gain.reshape(1, d), w)


def _pool_kernel(zc_ref, zp_ref, wp_ref, sc_ref, o_ref, ext_ref, *, bm, c):
    i = pl.program_id(1)
    ext_ref[0:POOL_HALO, :] = jnp.where(i > 0, zp_ref[...], 0.0)
    ext_ref[POOL_HALO:POOL_HALO + bm, :] = zc_ref[...]
    t = i * bm + lax.broadcasted_iota(jnp.int32, (bm, 1), 0)
    for g, w in enumerate(POOL_WINDOWS):
        cols = slice(g * c, (g + 1) * c)
        x = zc_ref[:, cols]
        acc = x
        for k in range(1, w):
            acc = acc + ext_ref[POOL_HALO - k:POOL_HALO - k + bm, cols]
        cnt = jnp.minimum(t + 1, w).astype(F32)
        mixed = (acc / cnt - x).astype(BF16)
        y = jnp.dot(mixed, wp_ref[g], preferred_element_type=F32) * sc_ref[:, cols]
        o_ref[:, cols] = y.astype(BF16)


def _pool(z_main, w_pool, pool_scale, batch, seq, *, bm=512):
    g, c, _ = w_pool.shape
    pw = g * c
    bm = _tile(seq, bm)
    nb = seq // bm
    halo_blocks = bm // POOL_HALO
    return pl.pallas_call(
        functools.partial(_pool_kernel, bm=bm, c=c),
        grid=(batch, nb),
        in_specs=[
            pl.BlockSpec((bm, pw), lambda b, i: (b * nb + i, 0)),
            pl.BlockSpec((POOL_HALO, pw),
                         lambda b, i: (jnp.maximum((b * nb + i) * halo_blocks - 1, 0), 0)),
            pl.BlockSpec((g, c, c), lambda b, i: (0, 0, 0)),
            pl.BlockSpec((1, pw), lambda b, i: (0, 0)),
        ],
        out_specs=pl.BlockSpec((bm, pw), lambda b, i: (b * nb + i, 0)),
        out_shape=jax.ShapeDtypeStruct((batch * seq, pw), BF16),
        scratch_shapes=[pltpu.VMEM((POOL_HALO + bm, pw), F32)],
        compiler_params=_params("parallel", "parallel"),
        name="pool",
    )(z_main, z_main, w_pool, pool_scale.reshape(1, pw))


def _rope_tables(pos, inv):
    ang = pos.astype(F32) * inv
    cos, sin = jnp.cos(ang), jnp.sin(ang)
    lane = lax.broadcasted_iota(jnp.int32, ang.shape, 1)
    half = QK_ROPE // 2
    c = jnp.where(lane < QK_ROPE, cos, 0.0)
    sa = jnp.where(lane < half, -sin, 0.0)
    sb = jnp.where((lane >= half) & (lane < QK_ROPE), sin, 0.0)
    return c, sa, sb


def _rope_norm(r, gain, c, sa, sb):
    ms = jnp.sum(r * r, axis=-1, keepdims=True) * (1.0 / QK_ROPE)
    r = r * lax.rsqrt(ms + EPS) * gain
    half = QK_ROPE // 2
    return r * c + pltpu.roll(r, LANES - half, 1) * sa + pltpu.roll(r, half, 1) * sb


def _qkv_kernel(zq_ref, zkv_ref, zkr_ref, pos_ref, post_ref, inv_ref, invt_ref, gql_ref, gkvl_ref, gqt_ref,
                gkn_ref, gkr_ref, wuqt_ref, wuk_ref, wuvt_ref, qt_ref, k_ref, vt_ref,
                cqt_ref, ckv_ref, ckvt_ref, kr_ref, cost_ref, sint_ref):
    half = QK_ROPE // 2
    r1, r2, r3 = QK_NOPE, QK_NOPE + half, QK_HEAD

    @pl.when(pl.program_id(1) == 0)
    def _():
        cqt_ref[...] = _rms(zq_ref[...], gql_ref[...]).T.astype(BF16)
        ckv = _rms(zkv_ref[...], gkvl_ref[...])
        ckv_ref[...] = ckv.astype(BF16)
        ckvt_ref[...] = ckv.T.astype(BF16)
        c, sa, sb = _rope_tables(pos_ref[...], inv_ref[...])
        lane = lax.broadcasted_iota(jnp.int32, zkr_ref.shape, 1)
        zkr = jnp.where(lane < QK_ROPE, zkr_ref[...], 0.0)
        kr_ref[...] = _rope_norm(zkr, gkr_ref[...], c, sa, sb).astype(BF16)
        ang = post_ref[...].astype(F32) * invt_ref[...]
        cost_ref[...] = jnp.cos(ang)
        sint_ref[...] = jnp.sin(ang)

    bm = qt_ref.shape[-1]
    for s in range(qt_ref.shape[0]):
        q = jnp.dot(wuqt_ref[s], cqt_ref[...], preferred_element_type=F32)
        nope = q[:r1]
        rs = lax.rsqrt(jnp.mean(nope * nope, axis=0, keepdims=True) + EPS)
        qt_ref[s, :r1, :] = (nope * rs * gqt_ref[:r1, :]).astype(BF16)
        x1, x2 = q[r1:r2], q[r2:r3]
        ms = (jnp.sum(x1 * x1, axis=0, keepdims=True)
              + jnp.sum(x2 * x2, axis=0, keepdims=True)) * (1.0 / QK_ROPE)
        rs = lax.rsqrt(ms + EPS)
        y1, y2 = x1 * rs * gqt_ref[r1:r2, :], x2 * rs * gqt_ref[r2:r3, :]
        cos, sin = cost_ref[...], sint_ref[...]
        qt_ref[s, r1:r2, :] = (y1 * cos - y2 * sin).astype(BF16)
        qt_ref[s, r2:r3, :] = (y2 * cos + y1 * sin).astype(BF16)
        qt_ref[s, r3:, :] = jnp.zeros((QK_PAD - r3, bm), BF16)

        k = jnp.dot(ckv_ref[...], wuk_ref[s], preferred_element_type=F32)
        k_ref[s, :, :QK_NOPE] = _rms(k, gkn_ref[...]).astype(BF16)
        k_ref[s, :, QK_NOPE:] = kr_ref[...]
        vt = jnp.dot(wuvt_ref[s], ckvt_ref[...], preferred_element_type=F32)
        vt_ref[s, :V_HEAD, :] = vt.astype(BF16)
        vt_ref[s, V_HEAD:, :] = jnp.ones((V_ROWS - V_HEAD, bm), BF16)


def _qkv(z, pos, inv, gql, gkvl, gqt, gkn, gkr, wuqt, wuk, wuvt, batch, seq, off_q, off_kv, *, bm=512, hp=8):
    h, _, ql = wuqt.shape
    hp = min(hp, h)
    kvl = wuk.shape[1]
    bm = _tile(seq, bm)
    nb = seq // bm
    t = batch * seq
    half = QK_ROPE // 2
    assert off_q % ql == 0 and off_kv % kvl == 0 and (off_kv + kvl) % LANES == 0 and h % hp == 0
    const = lambda i, hh: (0, 0)
    head = lambda i, hh: (hh, 0, 0)
    out_map = lambda i, hh: (i // nb, hh, i % nb, 0)
    out_map_t = lambda i, hh: (i // nb, hh, i % nb, 0, 0)
    invt = jnp.broadcast_to(inv[0, :half].reshape(half, 1), (half, bm))
    gqt = jnp.broadcast_to(gqt.reshape(QK_PAD, 1), (QK_PAD, bm))
    return pl.pallas_call(
        _qkv_kernel,
        grid=(t // bm, h // hp),
        in_specs=[
            pl.BlockSpec((bm, ql), lambda i, hh: (i, off_q // ql)),
            pl.BlockSpec((bm, kvl), lambda i, hh: (i, off_kv // kvl)),
            pl.BlockSpec((bm, LANES), lambda i, hh: (i, (off_kv + kvl) // LANES)),
            pl.BlockSpec((bm, 1), lambda i, hh: (i, 0)),
            pl.BlockSpec((1, bm), lambda i, hh: (0, i)),
            pl.BlockSpec((1, LANES), const),
            pl.BlockSpec((half, bm), const),
            pl.BlockSpec((1, ql), const),
            pl.BlockSpec((1, kvl), const),
            pl.BlockSpec((QK_PAD, bm), const),
            pl.BlockSpec((1, QK_NOPE), const),
            pl.BlockSpec((1, LANES), const),
            pl.BlockSpec((hp, QK_PAD, ql), head),
            pl.BlockSpec((hp, kvl, QK_NOPE), head),
            pl.BlockSpec((hp, V_HEAD, kvl), head),
        ],
        out_specs=[
            pl.BlockSpec((None, hp, None, QK_PAD, bm), out_map_t),
            pl.BlockSpec((None, hp, bm, QK_PAD), out_map),
            pl.BlockSpec((None, hp, None, V_ROWS, bm), out_map_t),
        ],
        out_shape=[
            jax.ShapeDtypeStruct((batch, h, nb, QK_PAD, bm), BF16),
            jax.ShapeDtypeStruct((batch, h, seq, QK_PAD), BF16),
            jax.ShapeDtypeStruct((batch, h, nb, V_ROWS, bm), BF16),
        ],
        scratch_shapes=[
            pltpu.VMEM((ql, bm), BF16),
            pltpu.VMEM((bm, kvl), BF16),
            pltpu.VMEM((kvl, bm), BF16),
            pltpu.VMEM((bm, LANES), BF16),
            pltpu.VMEM((half, bm), F32),
            pltpu.VMEM((half, bm), F32),
        ],
        compiler_params=_params("parallel", "arbitrary"),
        name="qkv_prep",
    )(z, z, z, pos.reshape(t, 1), pos.reshape(1, t), inv, invt, gql, gkvl, gqt, gkn, gkr, wuqt, wuk, wuvt)


def _flash_kernel(qt_ref, k_ref, vt_ref, *refs, blk, chunk, n_cast):
    cast_in, o_ref, cast_out = refs[:n_cast], refs[n_cast], refs[n_cast + 1:2 * n_cast + 1]
    sa_ref, sb_ref, m_ref, acc_ref = refs[2 * n_cast + 1:]
    _cast_block(cast_in, cast_out)

    qi = pl.program_id(2)
    nch = 2 * blk // chunk
    per_blk = blk // chunk
    m_ref[...] = jnp.full(m_ref.shape, -jnp.inf, F32)
    acc_ref[...] = jnp.zeros(acc_ref.shape, F32)

    def visibility(c, diag):
        if diag is None or c * chunk >= (diag + 1) * blk:
            return "all"
        return "none" if (c + 1) * chunk <= diag * blk else "some"

    def scores(kj, s_ref, diag=None):
        start = pl.multiple_of(kj * blk, blk)
        k = k_ref[pl.ds(start, blk), :]
        for c in range(nch):
            if visibility(c, diag) != "none":
                qt = qt_ref[c // per_blk, :, (c % per_blk) * chunk:(c % per_blk + 1) * chunk]
                s_ref[c] = jnp.dot(k, qt, preferred_element_type=F32)

    def update(kj, s_ref, diag=None):
        vt = vt_ref[kj]
        for c in range(nch):
            see = visibility(c, diag)
            if see == "none":
                continue
            cols = slice(c * chunk, (c + 1) * chunk)
            st = s_ref[c]
            if see == "some":
                key = lax.broadcasted_iota(jnp.int32, st.shape, 0) + diag * blk
                qry = lax.broadcasted_iota(jnp.int32, st.shape, 1) + c * chunk
                st = jnp.where(key <= qry, st, -jnp.inf)
            m_old = m_ref[:, cols]
            m_new = jnp.maximum(m_old, jnp.max(st, axis=0, keepdims=True))
            p = jnp.exp2(st - m_new).astype(BF16)
            alpha = jnp.exp2(m_old - m_new)
            acc_ref[:, cols] = alpha * acc_ref[:, cols] + jnp.dot(vt, p, preferred_element_type=F32)
            m_ref[:, cols] = m_new

    scores(0, sa_ref)

    def pair(t):
        scores(2 * t + 1, sb_ref)
        update(2 * t, sa_ref)
        scores(2 * t + 2, sa_ref)
        update(2 * t + 1, sb_ref)

    def trip(u, carry):
        for t in range(PAIRS_PER_TRIP):
            pair(PAIRS_PER_TRIP * u + t)
        return carry

    lax.fori_loop(0, qi // PAIRS_PER_TRIP, trip, 0)
    done = qi // PAIRS_PER_TRIP * PAIRS_PER_TRIP
    for rest in range(PAIRS_PER_TRIP):
        @pl.when(qi - done == rest)
        def _():
            for t in range(rest):
                pair(done + t)
            scores(2 * qi + 1, sb_ref, diag=1)
            update(2 * qi, sa_ref, diag=0)
            update(2 * qi + 1, sb_ref, diag=1)
            o_ref[...] = (acc_ref[:V_HEAD, :] / acc_ref[V_HEAD:V_HEAD + 1, :]).T.astype(o_ref.dtype)


def _flash(qt, k, vt, cast=(), layer=0, *, chunk=256):
    b, h, nb, _, blk = qt.shape
    s = nb * blk
    chunk = min(chunk, blk)
    assert nb % 2 == 0
    cast_in, cast_out = _cast_specs(cast, layer, b * h, nb // 2, lambda bi, hi, qi: (bi * h + hi, qi))
    return pl.pallas_call(
        functools.partial(_flash_kernel, blk=blk, chunk=chunk, n_cast=len(cast)),
        grid=(b, h, nb // 2),
        in_specs=[
            pl.BlockSpec((None, None, 2, QK_PAD, blk), lambda bi, hi, qi: (bi, hi, qi, 0, 0)),
            pl.BlockSpec((None, None, s, QK_PAD), lambda bi, hi, qi: (bi, hi, 0, 0)),
            pl.BlockSpec((None, None, nb, V_ROWS, blk), lambda bi, hi, qi: (bi, hi, 0, 0, 0)),
            *cast_in,
        ],
        out_specs=[pl.BlockSpec((None, 2 * blk, V_HEAD), lambda bi, hi, qi: (bi, qi, hi)), *cast_out],
        out_shape=[jax.ShapeDtypeStruct((b, s, h * V_HEAD), BF16)]
        + [jax.ShapeDtypeStruct(w.shape[1:], BF16) for w in cast],
        scratch_shapes=[
            pltpu.VMEM((2 * blk // chunk, blk, chunk), F32),
            pltpu.VMEM((2 * blk // chunk, blk, chunk), F32),
            pltpu.VMEM((1, 2 * blk), F32),
            pltpu.VMEM((V_ROWS, 2 * blk), F32),
        ],
        compiler_params=_params("parallel", "parallel", "arbitrary"),
        name="mla_flash",
    )(qt, k, vt, *cast)


def _merge_kernel(a_ref, b_ref, wa_ref, wb_ref, gp_ref, gpx_ref, gm_ref, gmx_ref, o_ref, *, shift):
    bn = o_ref.shape[1]

    def gate(lo_ref, hi_ref):
        g = jnp.concatenate([lo_ref[...], hi_ref[...]], axis=1)[:, shift:shift + bn]
        return jax.nn.sigmoid(g)

    pa = jnp.dot(a_ref[...], wa_ref[...], preferred_element_type=F32)
    pb = jnp.dot(b_ref[...], wb_ref[...], preferred_element_type=F32)
    o_ref[...] = (gate(gp_ref, gpx_ref) * pa + gate(gm_ref, gmx_ref) * pb).astype(BF16)


def _merge(a, b, wa, wb, z, gate_off, *, bm=1024, bn=512):
    t, pw = a.shape
    mw = b.shape[1]
    d = wa.shape[1]
    shift = gate_off % LANES
    base = gate_off - shift
    bm, bn = _tile(t, bm), _tile(math.gcd(d, base), bn)
    lanes_per_tile = bn // LANES
    lo = lambda off: (lambda i, j: (i, off // bn + j))
    hi = lambda off: (lambda i, j: (i, (off + bn) // LANES + j * lanes_per_tile))
    return pl.pallas_call(
        functools.partial(_merge_kernel, shift=shift),
        grid=(t // bm, d // bn),
        in_specs=[
            pl.BlockSpec((bm, pw), lambda i, j: (i, 0)),
            pl.BlockSpec((bm, mw), lambda i, j: (i, 0)),
            pl.BlockSpec((pw, bn), lambda i, j: (0, j)),
            pl.BlockSpec((mw, bn), lambda i, j: (0, j)),
            pl.BlockSpec((bm, bn), lo(base)),
            pl.BlockSpec((bm, LANES), hi(base)),
            pl.BlockSpec((bm, bn), lo(base + d)),
            pl.BlockSpec((bm, LANES), hi(base + d)),
        ],
        out_specs=pl.BlockSpec((bm, bn), lambda i, j: (i, j)),
        out_shape=jax.ShapeDtypeStruct((t, d), BF16),
        compiler_params=_params("parallel", "arbitrary"),
        name="branch_merge",
    )(a, b, wa, wb, z, z, z, z)


def _proj_residual_kernel(a_ref, w_ref, h_ref, o_ref):
    o_ref[...] = h_ref[...] + jnp.dot(a_ref[...], w_ref[...], preferred_element_type=F32)


def _proj_residual(a, w, h, *, bm=1024, bn=1024):
    t, kdim = a.shape
    d = w.shape[1]
    bm, bn = _tile(t, bm), _tile(d, bn)
    return pl.pallas_call(
        _proj_residual_kernel,
        grid=(t // bm, d // bn),
        in_specs=[
            pl.BlockSpec((bm, kdim), lambda i, j: (i, 0)),
            pl.BlockSpec((kdim, bn), lambda i, j: (0, j)),
            pl.BlockSpec((bm, bn), lambda i, j: (i, j)),
        ],
        out_specs=pl.BlockSpec((bm, bn), lambda i, j: (i, j)),
        out_shape=jax.ShapeDtypeStruct((t, d), F32),
        compiler_params=_params("parallel", "arbitrary"),
        name="out_proj",
    )(a, w, h)


def _mem_kv_kernel(mem_ref, gain_ref, w_ref, gk_ref, k_ref, v_ref):
    mn = _rms(mem_ref[...], gain_ref[...]).astype(BF16)
    kv = jnp.dot(mn, w_ref[...], preferred_element_type=F32)
    xw = X_HEADS * X_HEAD_DIM
    for hh in range(X_HEADS):
        cols = slice(hh * X_HEAD_DIM, (hh + 1) * X_HEAD_DIM)
        k_ref[:, cols] = _rms(kv[:, cols], gk_ref[...]).astype(BF16)
    v_ref[...] = kv[:, xw:].astype(BF16)


def _mem_kv(mem, gain, w_xkv, gk):
    b, m, d = mem.shape
    xw = X_HEADS * X_HEAD_DIM
    return pl.pallas_call(
        _mem_kv_kernel,
        grid=(b,),
        in_specs=[
            pl.BlockSpec((None, m, d), lambda bi: (bi, 0, 0)),
            pl.BlockSpec((1, d), lambda bi: (0, 0)),
            pl.BlockSpec((d, 2 * xw), lambda bi: (0, 0)),
            pl.BlockSpec((1, X_HEAD_DIM), lambda bi: (0, 0)),
        ],
        out_specs=[
            pl.BlockSpec((None, m, xw), lambda bi: (bi, 0, 0)),
            pl.BlockSpec((None, m, xw), lambda bi: (bi, 0, 0)),
        ],
        out_shape=[
            jax.ShapeDtypeStruct((b, m, xw), BF16),
            jax.ShapeDtypeStruct((b, m, xw), BF16),
        ],
        compiler_params=_params("parallel"),
        name="mem_kv",
    )(mem, gain.reshape(1, d), w_xkv, gk)


def _cross_kernel(h_ref, gain_ref, wq_ref, gq_ref, k_ref, v_ref, wo_ref, o_ref, xo_ref):
    h = h_ref[...]
    uq = _rms(h, gain_ref[...]).astype(BF16)
    xq = jnp.dot(uq, wq_ref[...], preferred_element_type=F32)
    scale = X_HEAD_DIM ** -0.5
    for hh in range(X_HEADS):
        cols = slice(hh * X_HEAD_DIM, (hh + 1) * X_HEAD_DIM)
        qh = _rms(xq[:, cols], gq_ref[...]).astype(BF16)
        s = lax.dot_general(qh, k_ref[:, cols], (((1,), (1,)), ((), ())),
                            preferred_element_type=F32) * scale
        p = jnp.exp(s - jnp.max(s, axis=-1, keepdims=True))
        p = (p / jnp.sum(p, axis=-1, keepdims=True)).astype(BF16)
        xo_ref[:, cols] = jnp.dot(p, v_ref[:, cols], preferred_element_type=F32).astype(BF16)
    o_ref[...] = h + jnp.dot(xo_ref[...], wo_ref[...], preferred_element_type=F32)


def _cross(h, gain, wq, gq, xk, xv, wo, seq, *, bm=512):
    t, d = h.shape
    m = xk.shape[1]
    xw = X_HEADS * X_HEAD_DIM
    bm = _tile(seq, bm)
    nb = seq // bm
    return pl.pallas_call(
        _cross_kernel,
        grid=(t // bm,),
        in_specs=[
            pl.BlockSpec((bm, d), lambda i: (i, 0)),
            pl.BlockSpec((1, d), lambda i: (0, 0)),
            pl.BlockSpec((d, xw), lambda i: (0, 0)),
            pl.BlockSpec((1, X_HEAD_DIM), lambda i: (0, 0)),
            pl.BlockSpec((None, m, xw), lambda i: (i // nb, 0, 0)),
            pl.BlockSpec((None, m, xw), lambda i: (i // nb, 0, 0)),
            pl.BlockSpec((xw, d), lambda i: (0, 0)),
        ],
        out_specs=pl.BlockSpec((bm, d), lambda i: (i, 0)),
        out_shape=jax.ShapeDtypeStruct((t, d), F32),
        scratch_shapes=[pltpu.VMEM((bm, xw), BF16)],
        compiler_params=_params("parallel"),
        name="cross_attn",
    )(h, gain.reshape(1, d), wq, gq, xk, xv, wo)


def _pad_lanes(v, width):
    return jnp.pad(v, (0, width - v.shape[0])).reshape(1, width)


def kernel(x, mem, positions, ffn1_norm, ffn1_w_gu, ffn1_w_down, mix_norm, w_in, w_pool, pool_scale,
           q_latent_norm, kv_latent_norm, w_uq, w_ukv, q_nope_norm, k_nope_norm, q_rope_norm,
           k_rope_norm, w_branch_pool, w_branch_mla, w_out, x_norm, mem_norm, w_xq, w_xkv, xq_norm,
           xk_norm, w_xo, ffn2_norm, ffn2_w_gu, ffn2_w_down):
    batch, seq, d = x.shape
    t = batch * seq
    depth = w_in.shape[0]
    pw = w_branch_pool.shape[1]
    ql = q_latent_norm.shape[1]
    kvl = kv_latent_norm.shape[1]
    heads = w_uq.shape[2] // QK_HEAD

    half = QK_ROPE // 2
    inv = 1.0 / (ROPE_THETA ** (jnp.arange(half, dtype=F32) * (2.0 / QK_ROPE)))
    inv = _pad_lanes(jnp.concatenate([inv, inv]), LANES)
    c2 = QK_HEAD ** -0.5 * math.log2(math.e)

    h = x.reshape(t, d)
    for l in range(depth):
        h = _ffn(h, ffn1_norm[l], ffn1_w_gu[l].astype(BF16), ffn1_w_down[l].astype(BF16))

        off_kv = pw + ql
        off_gate = off_kv + kvl + QK_ROPE
        z = _norm_matmul(h, mix_norm[l], w_in[l].astype(BF16), bn=IN_TILE, name="in_proj")

        a_out = _pool(z, w_pool[l].astype(BF16), pool_scale[l], batch, seq)

        wuq = jnp.pad(w_uq[l].reshape(ql, heads, QK_HEAD), ((0, 0), (0, 0), (0, QK_PAD - QK_HEAD)))
        wuqt = wuq.transpose(1, 2, 0).astype(BF16)
        wukv = w_ukv[l].reshape(kvl, heads, QK_NOPE + V_HEAD)
        wuk = wukv[:, :, :QK_NOPE].transpose(1, 0, 2).astype(BF16)
        wuvt = wukv[:, :, QK_NOPE:].transpose(1, 2, 0).astype(BF16)
        gqt = jnp.concatenate([q_nope_norm[l], q_rope_norm[l], jnp.zeros((QK_PAD - QK_HEAD,), F32)]) * c2
        qt, k, vt = _qkv(z, positions, inv,
                         q_latent_norm[l].reshape(1, ql), kv_latent_norm[l].reshape(1, kvl), gqt,
                         k_nope_norm[l].reshape(1, QK_NOPE), _pad_lanes(k_rope_norm[l], LANES),
                         wuqt, wuk, wuvt, batch, seq, pw, off_kv)
        later = (ffn2_w_gu, ffn2_w_down, w_out, w_branch_pool, w_branch_mla)
        b_out, w_gu2, w_down2, w_out_b, w_bp, w_bm = _flash(qt, k, vt, later, l)
        b_out = b_out.reshape(t, heads * V_HEAD)

        merged = _merge(a_out, b_out, w_bp, w_bm, z, off_gate)
        h = _proj_residual(merged, w_out_b, h)

        xk, xv = _mem_kv(mem, mem_norm[l], w_xkv[l].astype(BF16), xk_norm[l].reshape(1, X_HEAD_DIM))
        h = _cross(h, x_norm[l], w_xq[l].astype(BF16), xq_norm[l].reshape(1, X_HEAD_DIM), xk, xv,
                   w_xo[l].astype(BF16), seq)

        h = _ffn(h, ffn2_norm[l], w_gu2, w_down2)
    return h.reshape(batch, seq, d)
```

```python
import functools
import math

import jax
import jax.numpy as jnp
from jax import lax
from jax.experimental import pallas as pl
from jax.experimental.pallas import tpu as pltpu

F32 = jnp.float32
BF16 = jnp.bfloat16

POOL_WINDOWS = (2, 4, 8, 16)
POOL_HALO = 16
QK_NOPE = 128
QK_ROPE = 64
QK_HEAD = QK_NOPE + QK_ROPE
QK_PAD = 256
V_HEAD = 128
V_ROWS = V_HEAD + 16
X_HEADS = 4
X_HEAD_DIM = 128
ROPE_THETA = 10000.0
EPS = 1e-6
LANES = 128
BF16_ROWS = 16
VMEM_LIMIT = 62 * 1024 * 1024
NORM_ROWS = 256
IN_TILE = 1024
PAIRS_PER_TRIP = 4


def _params(*sem):
    return pltpu.CompilerParams(dimension_semantics=sem, vmem_limit_bytes=VMEM_LIMIT)


def _tile(n, pref):
    if n <= pref:
        return n
    t = (pref // LANES) * LANES
    while t >= LANES:
        if n % t == 0:
            return t
        t -= LANES
    raise ValueError(f"no lane-aligned tile of {n} below {pref}")


def _ceil_to(x, m):
    return -(-x // m) * m


def _rms(x, gain):
    ms = jnp.mean(x * x, axis=-1, keepdims=True)
    return x * lax.rsqrt(ms + EPS) * gain


def _row_chunk_copies(src_hbm, dst_ref, sem, row0):
    return [pltpu.make_async_copy(src_hbm.at[pl.ds(row0 + r, NORM_ROWS), :],
                                  dst_ref.at[pl.ds(r, NORM_ROWS), :], sem.at[c])
            for c, r in enumerate(range(0, dst_ref.shape[0], NORM_ROWS))]


def _ffn_kernel(x_hbm, gain_ref, wg_ref, wu_ref, wd_ref, *refs, regroup_blocks):
    if regroup_blocks is None:
        o_ref, xn_ref, sem = refs
    else:
        src_ref, o_ref, dst_ref, xn_ref, sem = refs
        step = pl.program_id(0) * pl.num_programs(1) + pl.program_id(1)
        dst_ref[...] = jnp.where(step < regroup_blocks, src_ref[...], 0.0).astype(BF16)

    def add_tile(base, xn):
        g = jnp.dot(xn, wg_ref[...], preferred_element_type=F32)
        u = jnp.dot(xn, wu_ref[...], preferred_element_type=F32)
        hid = (g * jax.nn.sigmoid(g) * u * 0.5).astype(BF16)
        return base + jnp.dot(hid, wd_ref[...], preferred_element_type=F32)

    @pl.when(pl.program_id(1) == 0)
    def _():
        copies = _row_chunk_copies(x_hbm, o_ref, sem, pl.program_id(0) * o_ref.shape[0])
        for copy in copies:
            copy.start()
        for c, copy in enumerate(copies):
            copy.wait()
            rows = slice(c * NORM_ROWS, (c + 1) * NORM_ROWS)
            x = o_ref[rows, :]
            xn = _rms(x, gain_ref[...]).astype(BF16)
            xn_ref[rows, :] = xn
            o_ref[rows, :] = add_tile(x, xn)

    @pl.when(pl.program_id(1) > 0)
    def _():
        o_ref[...] = add_tile(o_ref[...], xn_ref[...])


def _regroup_specs(wt, layer, split, base, steps):
    _, n, d = wt.shape
    assert base > split
    rb = math.gcd(split, base - split)
    nblk, first_gap, gap = pl.cdiv(n, rb), split // rb, (base - split) // rb
    assert rb % BF16_ROWS == 0 and nblk + gap <= steps[0] * steps[1]

    def src_block(i, j):
        return jnp.minimum(i * steps[1] + j, nblk - 1)

    def dst_block(i, j):
        s = i * steps[1] + j
        copied = jnp.where(s < first_gap, s, s + gap)
        zeroed = first_gap + jnp.minimum(s - nblk, gap - 1)
        return jnp.where(s < nblk, copied, zeroed)

    return (pl.BlockSpec((None, rb, d), lambda i, j: (layer, src_block(i, j), 0)),
            pl.BlockSpec((rb, d), lambda i, j: (dst_block(i, j), 0)),
            jax.ShapeDtypeStruct((n + base - split, d), BF16), nblk)


def _ffn(x, gain, w_gu, w_down, regroup=None, *, bm=1024, bf=256):
    t, d = x.shape
    f = w_down.shape[0]
    bm, bf = _tile(t, bm), _tile(f, bf)
    assert bm % NORM_ROWS == 0
    nf = f // bf
    grid = (t // bm, nf)
    cast_in, cast_out, cast_shape, cast_args, nblk = [], [], [], [], None
    if regroup is not None:
        src, dst, shape, nblk = _regroup_specs(*regroup, grid)
        cast_in, cast_out, cast_shape, cast_args = [src], [dst], [shape], [regroup[0]]
    out = pl.pallas_call(
        functools.partial(_ffn_kernel, regroup_blocks=nblk),
        grid=grid,
        in_specs=[
            pl.BlockSpec(memory_space=pl.ANY),
            pl.BlockSpec((1, d), lambda i, j: (0, 0)),
            pl.BlockSpec((d, bf), lambda i, j: (0, j)),
            pl.BlockSpec((d, bf), lambda i, j: (0, j + nf)),
            pl.BlockSpec((bf, d), lambda i, j: (j, 0)),
            *cast_in,
        ],
        out_specs=[pl.BlockSpec((bm, d), lambda i, j: (i, 0)), *cast_out],
        out_shape=[jax.ShapeDtypeStruct((t, d), F32), *cast_shape],
        scratch_shapes=[pltpu.VMEM((bm, d), BF16), pltpu.SemaphoreType.DMA((bm // NORM_ROWS,))],
        compiler_params=_params("arbitrary", "arbitrary"),
        name="ffn",
    )(x, gain.reshape(1, d), w_gu, w_gu, w_down, *cast_args)
    return out if regroup is not None else out[0]


def _cast_specs(ws, layer, gi, gj, rc):
    ins, outs = [], []
    for w in ws:
        _, r, c = w.shape
        br = _ceil_to(pl.cdiv(r, gi), BF16_ROWS)
        bc = _ceil_to(pl.cdiv(c, gj), LANES)
        last = pl.cdiv(c, bc) - 1
        assert (gi - 1) * br < r <= gi * br and last < gj

        def block(*g, last=last):
            i, j = rc(*g)
            return i, jnp.minimum(j, last)

        ins.append(pl.BlockSpec((None, br, bc), lambda *g, block=block: (layer, *block(*g))))
        outs.append(pl.BlockSpec((br, bc), block))
    return ins, outs


def _cast_block(cast_in, cast_out):
    for src, dst in zip(cast_in, cast_out):
        dst[...] = src[...].astype(BF16)


def _dot_nt(a, wt):
    return lax.dot_general(a, wt, (((1,), (1,)), ((), ())), preferred_element_type=F32)


def _norm_matmul_kernel(x_hbm, gain_ref, wt_ref, o_ref, xn_ref, x_ref, sem):
    @pl.when(pl.program_id(1) == 0)
    def _():
        copies = _row_chunk_copies(x_hbm, x_ref, sem, pl.program_id(0) * x_ref.shape[0])
        for copy in copies:
            copy.start()
        for c, copy in enumerate(copies):
            copy.wait()
            rows = slice(c * NORM_ROWS, (c + 1) * NORM_ROWS)
            xn = _rms(x_ref[rows, :], gain_ref[...]).astype(BF16)
            xn_ref[rows, :] = xn
            o_ref[rows, :] = _dot_nt(xn, wt_ref[...])

    @pl.when(pl.program_id(1) > 0)
    def _():
        o_ref[...] = _dot_nt(xn_ref[...], wt_ref[...])


def _norm_matmul(x, gain, wt, *, bm=1024, bn=1024, name):
    t, d = x.shape
    n = wt.shape[0]
    bm, bn = _tile(t, bm), _tile(n, bn)
    assert bm % NORM_ROWS == 0
    return pl.pallas_call(
        _norm_matmul_kernel,
        grid=(t // bm, n // bn),
        in_specs=[
            pl.BlockSpec(memory_space=pl.ANY),
            pl.BlockSpec((1, d), lambda i, j: (0, 0)),
            pl.BlockSpec((bn, d), lambda i, j: (j, 0)),
        ],
        out_specs=pl.BlockSpec((bm, bn), lambda i, j: (i, j)),
        out_shape=jax.ShapeDtypeStruct((t, n), F32),
        scratch_shapes=[pltpu.VMEM((bm, d), BF16), pltpu.VMEM((bm, d), F32),
                        pltpu.SemaphoreType.DMA((bm // NORM_ROWS,))],
        compiler_params=_params("parallel", "arbitrary"),
        name=name,
    )(x, gain.reshape(1, d), wt)


def _pool_kernel(zc_ref, zp_ref, wp_ref, sc_ref, o_ref, ext_ref, *, bm, c):
    i = pl.program_id(1)
    ext_ref[0:POOL_HALO, :] = jnp.where(i > 0, zp_ref[...], 0.0)
    ext_ref[POOL_HALO:POOL_HALO + bm, :] = zc_ref[...]
    t = i * bm + lax.broadcasted_iota(jnp.int32, (bm, 1), 0)
    for g, w in enumerate(POOL_WINDOWS):
        cols = slice(g * c, (g + 1) * c)
        x = zc_ref[:, cols]
        acc = x
        for k in range(1, w):
            acc = acc + ext_ref[POOL_HALO - k:POOL_HALO - k + bm, cols]
        cnt = jnp.minimum(t + 1, w).astype(F32)
        mixed = (acc / cnt - x).astype(BF16)
        y = jnp.dot(mixed, wp_ref[g], preferred_element_type=F32) * sc_ref[:, cols]
        o_ref[:, cols] = y.astype(BF16)


def _pool(z_main, w_pool, pool_scale, batch, seq, *, bm=512):
    g, c, _ = w_pool.shape
    pw = g * c
    bm = _tile(seq, bm)
    nb = seq // bm
    halo_blocks = bm // POOL_HALO
    return pl.pallas_call(
        functools.partial(_pool_kernel, bm=bm, c=c),
        grid=(batch, nb),
        in_specs=[
            pl.BlockSpec((bm, pw), lambda b, i: (b * nb + i, 0)),
            pl.BlockSpec((POOL_HALO, pw),
                         lambda b, i: (jnp.maximum((b * nb + i) * halo_blocks - 1, 0), 0)),
            pl.BlockSpec((g, c, c), lambda b, i: (0, 0, 0)),
            pl.BlockSpec((1, pw), lambda b, i: (0, 0)),
        ],
        out_specs=pl.BlockSpec((bm, pw), lambda b, i: (b * nb + i, 0)),
        out_shape=jax.ShapeDtypeStruct((batch * seq, pw), BF16),
        scratch_shapes=[pltpu.VMEM((POOL_HALO + bm, pw), F32)],
        compiler_params=_params("parallel", "parallel"),
        name="pool",
    )(z_main, z_main, w_pool, pool_scale.reshape(1, pw))


def _rope_tables(pos, inv):
    ang = pos.astype(F32) * inv
    cos, sin = jnp.cos(ang), jnp.sin(ang)
    lane = lax.broadcasted_iota(jnp.int32, ang.shape, 1)
    half = QK_ROPE // 2
    c = jnp.where(lane < QK_ROPE, cos, 0.0)
    sa = jnp.where(lane < half, -sin, 0.0)
    sb = jnp.where((lane >= half) & (lane < QK_ROPE), sin, 0.0)
    return c, sa, sb


def _rope_norm(r, gain, c, sa, sb):
    ms = jnp.sum(r * r, axis=-1, keepdims=True) * (1.0 / QK_ROPE)
    r = r * lax.rsqrt(ms + EPS) * gain
    half = QK_ROPE // 2
    return r * c + pltpu.roll(r, LANES - half, 1) * sa + pltpu.roll(r, half, 1) * sb


def _qkv_kernel(zq_ref, zkv_ref, zkr_ref, pos_ref, post_ref, inv_ref, invt_ref, gql_ref, gkvl_ref, gqt_ref,
                gkn_ref, gkr_ref, wuqt_ref, wuk_ref, wuvt_ref, qt_ref, k_ref, vt_ref,
                cqt_ref, ckv_ref, ckvt_ref, kr_ref, cost_ref, sint_ref):
    half = QK_ROPE // 2
    r1, r2, r3 = QK_NOPE, QK_NOPE + half, QK_HEAD

    @pl.when(pl.program_id(1) == 0)
    def _():
        cqt_ref[...] = _rms(zq_ref[...], gql_ref[...]).T.astype(BF16)
        ckv = _rms(zkv_ref[...], gkvl_ref[...])
        ckv_ref[...] = ckv.astype(BF16)
        ckvt_ref[...] = ckv.T.astype(BF16)
        c, sa, sb = _rope_tables(pos_ref[...], inv_ref[...])
        lane = lax.broadcasted_iota(jnp.int32, zkr_ref.shape, 1)
        zkr = jnp.where(lane < QK_ROPE, zkr_ref[...], 0.0)
        kr_ref[...] = _rope_norm(zkr, gkr_ref[...], c, sa, sb).astype(BF16)
        ang = post_ref[...].astype(F32) * invt_ref[...]
        cost_ref[...] = jnp.cos(ang)
        sint_ref[...] = jnp.sin(ang)

    bm = qt_ref.shape[-1]
    for s in range(qt_ref.shape[0]):
        q = jnp.dot(wuqt_ref[s], cqt_ref[...], preferred_element_type=F32)
        nope = q[:r1]
        rs = lax.rsqrt(jnp.mean(nope * nope, axis=0, keepdims=True) + EPS)
        qt_ref[s, :r1, :] = (nope * rs * gqt_ref[:r1, :]).astype(BF16)
        x1, x2 = q[r1:r2], q[r2:r3]
        ms = (jnp.sum(x1 * x1, axis=0, keepdims=True)
              + jnp.sum(x2 * x2, axis=0, keepdims=True)) * (1.0 / QK_ROPE)
        rs = lax.rsqrt(ms + EPS)
        y1, y2 = x1 * rs * gqt_ref[r1:r2, :], x2 * rs * gqt_ref[r2:r3, :]
        cos, sin = cost_ref[...], sint_ref[...]
        qt_ref[s, r1:r2, :] = (y1 * cos - y2 * sin).astype(BF16)
        qt_ref[s, r2:r3, :] = (y2 * cos + y1 * sin).astype(BF16)
        qt_ref[s, r3:, :] = jnp.zeros((QK_PAD - r3, bm), BF16)

        k = jnp.dot(ckv_ref[...], wuk_ref[s], preferred_element_type=F32)
        k_ref[s, :, :QK_NOPE] = _rms(k, gkn_ref[...]).astype(BF16)
        k_ref[s, :, QK_NOPE:] = kr_ref[...]
        vt = jnp.dot(wuvt_ref[s], ckvt_ref[...], preferred_element_type=F32)
        vt_ref[s, :V_HEAD, :] = vt.astype(BF16)
        vt_ref[s, V_HEAD:, :] = jnp.ones((V_ROWS - V_HEAD, bm), BF16)


def _qkv(z, pos, inv, gql, gkvl, gqt, gkn, gkr, wuqt, wuk, wuvt, batch, seq, off_q, off_kv, *, bm=512, hp=8):
    h, _, ql = wuqt.shape
    hp = min(hp, h)
    kvl = wuk.shape[1]
    bm = _tile(seq, bm)
    nb = seq // bm
    t = batch * seq
    half = QK_ROPE // 2
    assert off_q % ql == 0 and off_kv % kvl == 0 and (off_kv + kvl) % LANES == 0 and h % hp == 0
    const = lambda i, hh: (0, 0)
    head = lambda i, hh: (hh, 0, 0)
    out_map = lambda i, hh: (i // nb, hh, i % nb, 0)
    out_map_t = lambda i, hh: (i // nb, hh, i % nb, 0, 0)
    invt = jnp.broadcast_to(inv[0, :half].reshape(half, 1), (half, bm))
    gqt = jnp.broadcast_to(gqt.reshape(QK_PAD, 1), (QK_PAD, bm))
    return pl.pallas_call(
        _qkv_kernel,
        grid=(t // bm, h // hp),
        in_specs=[
            pl.BlockSpec((bm, ql), lambda i, hh: (i, off_q // ql)),
            pl.BlockSpec((bm, kvl), lambda i, hh: (i, off_kv // kvl)),
            pl.BlockSpec((bm, LANES), lambda i, hh: (i, (off_kv + kvl) // LANES)),
            pl.BlockSpec((bm, 1), lambda i, hh: (i, 0)),
            pl.BlockSpec((1, bm), lambda i, hh: (0, i)),
            pl.BlockSpec((1, LANES), const),
            pl.BlockSpec((half, bm), const),
            pl.BlockSpec((1, ql), const),
            pl.BlockSpec((1, kvl), const),
            pl.BlockSpec((QK_PAD, bm), const),
            pl.BlockSpec((1, QK_NOPE), const),
            pl.BlockSpec((1, LANES), const),
            pl.BlockSpec((hp, QK_PAD, ql), head),
            pl.BlockSpec((hp, kvl, QK_NOPE), head),
            pl.BlockSpec((hp, V_HEAD, kvl), head),
        ],
        out_specs=[
            pl.BlockSpec((None, hp, None, QK_PAD, bm), out_map_t),
            pl.BlockSpec((None, hp, bm, QK_PAD), out_map),
            pl.BlockSpec((None, hp, None, V_ROWS, bm), out_map_t),
        ],
        out_shape=[
            jax.ShapeDtypeStruct((batch, h, nb, QK_PAD, bm), BF16),
            jax.ShapeDtypeStruct((batch, h, seq, QK_PAD), BF16),
            jax.ShapeDtypeStruct((batch, h, nb, V_ROWS, bm), BF16),
        ],
        scratch_shapes=[
            pltpu.VMEM((ql, bm), BF16),
            pltpu.VMEM((bm, kvl), BF16),
            pltpu.VMEM((kvl, bm), BF16),
            pltpu.VMEM((bm, LANES), BF16),
            pltpu.VMEM((half, bm), F32),
            pltpu.VMEM((half, bm), F32),
        ],
        compiler_params=_params("parallel", "arbitrary"),
        name="qkv_prep",
    )(z, z, z, pos.reshape(t, 1), pos.reshape(1, t), inv, invt, gql, gkvl, gqt, gkn, gkr, wuqt, wuk, wuvt)


def _flash_kernel(qt_ref, k_ref, vt_ref, *refs, blk, chunk, n_cast):
    cast_in, o_ref, cast_out = refs[:n_cast], refs[n_cast], refs[n_cast + 1:2 * n_cast + 1]
    sa_ref, sb_ref, m_ref, acc_ref = refs[2 * n_cast + 1:]
    _cast_block(cast_in, cast_out)

    qi = pl.program_id(2)
    nch = 2 * blk // chunk
    per_blk = blk // chunk
    m_ref[...] = jnp.full(m_ref.shape, -jnp.inf, F32)
    acc_ref[...] = jnp.zeros(acc_ref.shape, F32)

    def visibility(c, diag):
        if diag is None or c * chunk >= (diag + 1) * blk:
            return "all"
        return "none" if (c + 1) * chunk <= diag * blk else "some"

    def scores(kj, s_ref, diag=None):
        start = pl.multiple_of(kj * blk, blk)
        k = k_ref[pl.ds(start, blk), :]
        for c in range(nch):
            if visibility(c, diag) != "none":
                qt = qt_ref[c // per_blk, :, (c % per_blk) * chunk:(c % per_blk + 1) * chunk]
                s_ref[c] = jnp.dot(k, qt, preferred_element_type=F32)

    def update(kj, s_ref, diag=None):
        vt = vt_ref[kj]
        for c in range(nch):
            see = visibility(c, diag)
            if see == "none":
                continue
            cols = slice(c * chunk, (c + 1) * chunk)
            st = s_ref[c]
            if see == "some":
                key = lax.broadcasted_iota(jnp.int32, st.shape, 0) + diag * blk
                qry = lax.broadcasted_iota(jnp.int32, st.shape, 1) + c * chunk
                st = jnp.where(key <= qry, st, -jnp.inf)
            m_old = m_ref[:, cols]
            m_new = jnp.maximum(m_old, jnp.max(st, axis=0, keepdims=True))
            p = jnp.exp2(st - m_new).astype(BF16)
            alpha = jnp.exp2(m_old - m_new)
            acc_ref[:, cols] = alpha * acc_ref[:, cols] + jnp.dot(vt, p, preferred_element_type=F32)
            m_ref[:, cols] = m_new

    scores(0, sa_ref)

    def pair(t):
        scores(2 * t + 1, sb_ref)
        update(2 * t, sa_ref)
        scores(2 * t + 2, sa_ref)
        update(2 * t + 1, sb_ref)

    def trip(u, carry):
        for t in range(PAIRS_PER_TRIP):
            pair(PAIRS_PER_TRIP * u + t)
        return carry

    lax.fori_loop(0, qi // PAIRS_PER_TRIP, trip, 0)
    done = qi // PAIRS_PER_TRIP * PAIRS_PER_TRIP
    for rest in range(PAIRS_PER_TRIP):
        @pl.when(qi - done == rest)
        def _():
            for t in range(rest):
                pair(done + t)
            scores(2 * qi + 1, sb_ref, diag=1)
            update(2 * qi, sa_ref, diag=0)
            update(2 * qi + 1, sb_ref, diag=1)
            o_ref[...] = (acc_ref[:V_HEAD, :] / acc_ref[V_HEAD:V_HEAD + 1, :]).T.astype(o_ref.dtype)


def _flash(qt, k, vt, cast=(), layer=0, *, chunk=256):
    b, h, nb, _, blk = qt.shape
    s = nb * blk
    chunk = min(chunk, blk)
    assert nb % 2 == 0
    cast_in, cast_out = _cast_specs(cast, layer, b * h, nb // 2, lambda bi, hi, qi: (bi * h + hi, qi))
    return pl.pallas_call(
        functools.partial(_flash_kernel, blk=blk, chunk=chunk, n_cast=len(cast)),
        grid=(b, h, nb // 2),
        in_specs=[
            pl.BlockSpec((None, None, 2, QK_PAD, blk), lambda bi, hi, qi: (bi, hi, qi, 0, 0)),
            pl.BlockSpec((None, None, s, QK_PAD), lambda bi, hi, qi: (bi, hi, 0, 0)),
            pl.BlockSpec((None, None, nb, V_ROWS, blk), lambda bi, hi, qi: (bi, hi, 0, 0, 0)),
            *cast_in,
        ],
        out_specs=[pl.BlockSpec((None, 2 * blk, V_HEAD), lambda bi, hi, qi: (bi, qi, hi)), *cast_out],
        out_shape=[jax.ShapeDtypeStruct((b, s, h * V_HEAD), BF16)]
        + [jax.ShapeDtypeStruct(w.shape[1:], BF16) for w in cast],
        scratch_shapes=[
            pltpu.VMEM((2 * blk // chunk, blk, chunk), F32),
            pltpu.VMEM((2 * blk // chunk, blk, chunk), F32),
            pltpu.VMEM((1, 2 * blk), F32),
            pltpu.VMEM((V_ROWS, 2 * blk), F32),
        ],
        compiler_params=_params("parallel", "parallel", "arbitrary"),
        name="mla_flash",
    )(qt, k, vt, *cast)


def _merge_kernel(a_ref, b_ref, wa_ref, wb_ref, gp_ref, gm_ref, o_ref):
    pa = jnp.dot(a_ref[...], wa_ref[...], preferred_element_type=F32)
    pb = jnp.dot(b_ref[...], wb_ref[...], preferred_element_type=F32)
    o_ref[...] = (jax.nn.sigmoid(gp_ref[...]) * pa + jax.nn.sigmoid(gm_ref[...]) * pb).astype(BF16)


def _merge(a, b, wa, wb, z, gate_off, *, bm=1024, bn=512):
    t, pw = a.shape
    mw = b.shape[1]
    d = wa.shape[1]
    bm, bn = _tile(t, bm), _tile(math.gcd(d, gate_off), bn)
    nj, g0 = d // bn, gate_off // bn
    return pl.pallas_call(
        _merge_kernel,
        grid=(t // bm, nj),
        in_specs=[
            pl.BlockSpec((bm, pw), lambda i, j: (i, 0)),
            pl.BlockSpec((bm, mw), lambda i, j: (i, 0)),
            pl.BlockSpec((pw, bn), lambda i, j: (0, j)),
            pl.BlockSpec((mw, bn), lambda i, j: (0, j)),
            pl.BlockSpec((bm, bn), lambda i, j: (i, g0 + j)),
            pl.BlockSpec((bm, bn), lambda i, j: (i, g0 + nj + j)),
        ],
        out_specs=pl.BlockSpec((bm, bn), lambda i, j: (i, j)),
        out_shape=jax.ShapeDtypeStruct((t, d), BF16),
        compiler_params=_params("parallel", "arbitrary"),
        name="branch_merge",
    )(a, b, wa, wb, z, z)


def _proj_residual_kernel(a_ref, w_ref, h_ref, o_ref):
    o_ref[...] = h_ref[...] + jnp.dot(a_ref[...], w_ref[...], preferred_element_type=F32)


def _proj_residual(a, w, h, *, bm=1024, bn=1024):
    t, kdim = a.shape
    d = w.shape[1]
    bm, bn = _tile(t, bm), _tile(d, bn)
    return pl.pallas_call(
        _proj_residual_kernel,
        grid=(t // bm, d // bn),
        in_specs=[
            pl.BlockSpec((bm, kdim), lambda i, j: (i, 0)),
            pl.BlockSpec((kdim, bn), lambda i, j: (0, j)),
            pl.BlockSpec((bm, bn), lambda i, j: (i, j)),
        ],
        out_specs=pl.BlockSpec((bm, bn), lambda i, j: (i, j)),
        out_shape=jax.ShapeDtypeStruct((t, d), F32),
        compiler_params=_params("parallel", "arbitrary"),
        name="out_proj",
    )(a, w, h)


def _mem_kv_kernel(mem_ref, gain_ref, w_ref, gk_ref, k_ref, v_ref):
    mn = _rms(mem_ref[...], gain_ref[...]).astype(BF16)
    kv = jnp.dot(mn, w_ref[...], preferred_element_type=F32)
    xw = X_HEADS * X_HEAD_DIM
    for hh in range(X_HEADS):
        cols = slice(hh * X_HEAD_DIM, (hh + 1) * X_HEAD_DIM)
        k_ref[:, cols] = _rms(kv[:, cols], gk_ref[...]).astype(BF16)
    v_ref[...] = kv[:, xw:].astype(BF16)


def _mem_kv(mem, gain, w_xkv, gk):
    b, m, d = mem.shape
    xw = X_HEADS * X_HEAD_DIM
    return pl.pallas_call(
        _mem_kv_kernel,
        grid=(b,),
        in_specs=[
            pl.BlockSpec((None, m, d), lambda bi: (bi, 0, 0)),
            pl.BlockSpec((1, d), lambda bi: (0, 0)),
            pl.BlockSpec((d, 2 * xw), lambda bi: (0, 0)),
            pl.BlockSpec((1, X_HEAD_DIM), lambda bi: (0, 0)),
        ],
        out_specs=[
            pl.BlockSpec((None, m, xw), lambda bi: (bi, 0, 0)),
            pl.BlockSpec((None, m, xw), lambda bi: (bi, 0, 0)),
        ],
        out_shape=[
            jax.ShapeDtypeStruct((b, m, xw), BF16),
            jax.ShapeDtypeStruct((b, m, xw), BF16),
        ],
        compiler_params=_params("parallel"),
        name="mem_kv",
    )(mem, gain.reshape(1, d), w_xkv, gk)


def _cross_kernel(h_ref, gain_ref, wq_ref, gq_ref, k_ref, v_ref, wo_ref, o_ref, xo_ref):
    h = h_ref[...]
    uq = _rms(h, gain_ref[...]).astype(BF16)
    xq = jnp.dot(uq, wq_ref[...], preferred_element_type=F32)
    scale = X_HEAD_DIM ** -0.5
    for hh in range(X_HEADS):
        cols = slice(hh * X_HEAD_DIM, (hh + 1) * X_HEAD_DIM)
        qh = _rms(xq[:, cols], gq_ref[...]).astype(BF16)
        s = lax.dot_general(qh, k_ref[:, cols], (((1,), (1,)), ((), ())),
                            preferred_element_type=F32) * scale
        p = jnp.exp(s - jnp.max(s, axis=-1, keepdims=True))
        p = (p / jnp.sum(p, axis=-1, keepdims=True)).astype(BF16)
        xo_ref[:, cols] = jnp.dot(p, v_ref[:, cols], preferred_element_type=F32).astype(BF16)
    o_ref[...] = h + jnp.dot(xo_ref[...], wo_ref[...], preferred_element_type=F32)


def _cross(h, gain, wq, gq, xk, xv, wo, seq, *, bm=512):
    t, d = h.shape
    m = xk.shape[1]
    xw = X_HEADS * X_HEAD_DIM
    bm = _tile(seq, bm)
    nb = seq // bm
    return pl.pallas_call(
        _cross_kernel,
        grid=(t // bm,),
        in_specs=[
            pl.BlockSpec((bm, d), lambda i: (i, 0)),
            pl.BlockSpec((1, d), lambda i: (0, 0)),
            pl.BlockSpec((d, xw), lambda i: (0, 0)),
            pl.BlockSpec((1, X_HEAD_DIM), lambda i: (0, 0)),
            pl.BlockSpec((None, m, xw), lambda i: (i // nb, 0, 0)),
            pl.BlockSpec((None, m, xw), lambda i: (i // nb, 0, 0)),
            pl.BlockSpec((xw, d), lambda i: (0, 0)),
        ],
        out_specs=pl.BlockSpec((bm, d), lambda i: (i, 0)),
        out_shape=jax.ShapeDtypeStruct((t, d), F32),
        scratch_shapes=[pltpu.VMEM((bm, xw), BF16)],
        compiler_params=_params("parallel"),
        name="cross_attn",
    )(h, gain.reshape(1, d), wq, gq, xk, xv, wo)


def _pad_lanes(v, width):
    return jnp.pad(v, (0, width - v.shape[0])).reshape(1, width)


def kernel(x, mem, positions, ffn1_norm, ffn1_w_gu, ffn1_w_down, mix_norm, w_in, w_pool, pool_scale,
           q_latent_norm, kv_latent_norm, w_uq, w_ukv, q_nope_norm, k_nope_norm, q_rope_norm,
           k_rope_norm, w_branch_pool, w_branch_mla, w_out, x_norm, mem_norm, w_xq, w_xkv, xq_norm,
           xk_norm, w_xo, ffn2_norm, ffn2_w_gu, ffn2_w_down):
    batch, seq, d = x.shape
    t = batch * seq
    depth = w_in.shape[0]
    pw = w_branch_pool.shape[1]
    ql = q_latent_norm.shape[1]
    kvl = kv_latent_norm.shape[1]
    heads = w_uq.shape[2] // QK_HEAD

    half = QK_ROPE // 2
    inv = 1.0 / (ROPE_THETA ** (jnp.arange(half, dtype=F32) * (2.0 / QK_ROPE)))
    inv = _pad_lanes(jnp.concatenate([inv, inv]), LANES)
    c2 = QK_HEAD ** -0.5 * math.log2(math.e)

    h = x.reshape(t, d)
    for l in range(depth):
        off_kv = pw + ql
        off_gate = off_kv + kvl + QK_ROPE
        gate_base = _ceil_to(off_gate, IN_TILE)
        h, w_in_t = _ffn(h, ffn1_norm[l], ffn1_w_gu[l].astype(BF16), ffn1_w_down[l].astype(BF16),
                         (jnp.swapaxes(w_in, 1, 2), l, off_gate, gate_base))
        z = _norm_matmul(h, mix_norm[l], w_in_t, bn=IN_TILE, name="in_proj")

        a_out = _pool(z, w_pool[l].astype(BF16), pool_scale[l], batch, seq)

        wuq = jnp.pad(w_uq[l].reshape(ql, heads, QK_HEAD), ((0, 0), (0, 0), (0, QK_PAD - QK_HEAD)))
        wuqt = wuq.transpose(1, 2, 0).astype(BF16)
        wukv = w_ukv[l].reshape(kvl, heads, QK_NOPE + V_HEAD)
        wuk = wukv[:, :, :QK_NOPE].transpose(1, 0, 2).astype(BF16)
        wuvt = wukv[:, :, QK_NOPE:].transpose(1, 2, 0).astype(BF16)
        gqt = jnp.concatenate([q_nope_norm[l], q_rope_norm[l], jnp.zeros((QK_PAD - QK_HEAD,), F32)]) * c2
        qt, k, vt = _qkv(z, positions, inv,
                         q_latent_norm[l].reshape(1, ql), kv_latent_norm[l].reshape(1, kvl), gqt,
                         k_nope_norm[l].reshape(1, QK_NOPE), _pad_lanes(k_rope_norm[l], LANES),
                         wuqt, wuk, wuvt, batch, seq, pw, off_kv)
        later = (ffn2_w_gu, ffn2_w_down, w_out, w_branch_pool, w_branch_mla)
        b_out, w_gu2, w_down2, w_out_b, w_bp, w_bm = _flash(qt, k, vt, later, l)
        b_out = b_out.reshape(t, heads * V_HEAD)

        merged = _merge(a_out, b_out, w_bp, w_bm, z, gate_base)
        h = _proj_residual(merged, w_out_b, h)

        xk, xv = _mem_kv(mem, mem_norm[l], w_xkv[l].astype(BF16), xk_norm[l].reshape(1, X_HEAD_DIM))
        h = _cross(h, x_norm[l], w_xq[l].astype(BF16), xq_norm[l].reshape(1, X_HEAD_DIM), xk, xv,
                   w_xo[l].astype(BF16), seq)

        h = _ffn(h, ffn2_norm[l], w_gu2, w_down2)
    return h.reshape(batch, seq, d)
```

```python
import functools
import math

import jax
import jax.numpy as jnp
from jax import lax
from jax.experimental import pallas as pl
from jax.experimental.pallas import tpu as pltpu

F32 = jnp.float32
BF16 = jnp.bfloat16

POOL_WINDOWS = (2, 4, 8, 16)
POOL_HALO = 16
QK_NOPE = 128
QK_ROPE = 64
QK_HEAD = QK_NOPE + QK_ROPE
QK_PAD = 256
V_HEAD = 128
V_ROWS = V_HEAD + 16
X_HEADS = 4
X_HEAD_DIM = 128
ROPE_THETA = 10000.0
EPS = 1e-6
LANES = 128
BF16_ROWS = 16
VMEM_LIMIT = 62 * 1024 * 1024
NORM_ROWS = 256
IN_TILE = 1024
PAIRS_PER_TRIP = 4


def _params(*sem):
    return pltpu.CompilerParams(dimension_semantics=sem, vmem_limit_bytes=VMEM_LIMIT)


def _tile(n, pref):
    if n <= pref:
        return n
    t = (pref // LANES) * LANES
    while t >= LANES:
        if n % t == 0:
            return t
        t -= LANES
    raise ValueError(f"no lane-aligned tile of {n} below {pref}")


def _ceil_to(x, m):
    return -(-x // m) * m


def _rms(x, gain):
    ms = jnp.mean(x * x, axis=-1, keepdims=True)
    return x * lax.rsqrt(ms + EPS) * gain


def _row_chunk_copies(src_hbm, dst_ref, sem, row0):
    return [pltpu.make_async_copy(src_hbm.at[pl.ds(row0 + r, NORM_ROWS), :],
                                  dst_ref.at[pl.ds(r, NORM_ROWS), :], sem.at[c])
            for c, r in enumerate(range(0, dst_ref.shape[0], NORM_ROWS))]


def _ffn_kernel(x_hbm, gain_ref, wg_ref, wu_ref, wd_ref, *refs, regroup_blocks):
    if regroup_blocks is None:
        o_ref, xn_ref, sem = refs
    else:
        src_ref, o_ref, dst_ref, xn_ref, sem = refs
        step = pl.program_id(0) * pl.num_programs(1) + pl.program_id(1)
        dst_ref[...] = jnp.where(step < regroup_blocks, src_ref[...], 0.0).astype(BF16)

    def add_tile(base, xn):
        g = jnp.dot(xn, wg_ref[...], preferred_element_type=F32)
        u = jnp.dot(xn, wu_ref[...], preferred_element_type=F32)
        hid = (g * jax.nn.sigmoid(g) * u * 0.5).astype(BF16)
        return base + jnp.dot(hid, wd_ref[...], preferred_element_type=F32)

    @pl.when(pl.program_id(1) == 0)
    def _():
        copies = _row_chunk_copies(x_hbm, o_ref, sem, pl.program_id(0) * o_ref.shape[0])
        for copy in copies:
            copy.start()
        for c, copy in enumerate(copies):
            copy.wait()
            rows = slice(c * NORM_ROWS, (c + 1) * NORM_ROWS)
            x = o_ref[rows, :]
            xn = _rms(x, gain_ref[...]).astype(BF16)
            xn_ref[rows, :] = xn
            o_ref[rows, :] = add_tile(x, xn)

    @pl.when(pl.program_id(1) > 0)
    def _():
        o_ref[...] = add_tile(o_ref[...], xn_ref[...])


def _regroup_specs(wt, layer, split, base, steps):
    _, n, d = wt.shape
    assert base > split
    rb = math.gcd(split, base - split)
    nblk, first_gap, gap = pl.cdiv(n, rb), split // rb, (base - split) // rb
    assert rb % BF16_ROWS == 0 and nblk + gap <= steps[0] * steps[1]

    def src_block(i, j):
        return jnp.minimum(i * steps[1] + j, nblk - 1)

    def dst_block(i, j):
        s = i * steps[1] + j
        copied = jnp.where(s < first_gap, s, s + gap)
        zeroed = first_gap + jnp.minimum(s - nblk, gap - 1)
        return jnp.where(s < nblk, copied, zeroed)

    return (pl.BlockSpec((None, rb, d), lambda i, j: (layer, src_block(i, j), 0)),
            pl.BlockSpec((rb, d), lambda i, j: (dst_block(i, j), 0)),
            jax.ShapeDtypeStruct((n + base - split, d), BF16), nblk)


def _ffn(x, gain, w_gu, w_down, regroup=None, *, bm=1024, bf=256):
    t, d = x.shape
    f = w_down.shape[0]
    bm, bf = _tile(t, bm), _tile(f, bf)
    assert bm % NORM_ROWS == 0
    nf = f // bf
    grid = (t // bm, nf)
    cast_in, cast_out, cast_shape, cast_args, nblk = [], [], [], [], None
    if regroup is not None:
        src, dst, shape, nblk = _regroup_specs(*regroup, grid)
        cast_in, cast_out, cast_shape, cast_args = [src], [dst], [shape], [regroup[0]]
    out = pl.pallas_call(
        functools.partial(_ffn_kernel, regroup_blocks=nblk),
        grid=grid,
        in_specs=[
            pl.BlockSpec(memory_space=pl.ANY),
            pl.BlockSpec((1, d), lambda i, j: (0, 0)),
            pl.BlockSpec((d, bf), lambda i, j: (0, j)),
            pl.BlockSpec((d, bf), lambda i, j: (0, j + nf)),
            pl.BlockSpec((bf, d), lambda i, j: (j, 0)),
            *cast_in,
        ],
        out_specs=[pl.BlockSpec((bm, d), lambda i, j: (i, 0)), *cast_out],
        out_shape=[jax.ShapeDtypeStruct((t, d), F32), *cast_shape],
        scratch_shapes=[pltpu.VMEM((bm, d), BF16), pltpu.SemaphoreType.DMA((bm // NORM_ROWS,))],
        compiler_params=_params("arbitrary", "arbitrary"),
        name="ffn",
    )(x, gain.reshape(1, d), w_gu, w_gu, w_down, *cast_args)
    return out if regroup is not None else out[0]


def _cast_specs(ws, layer, gi, gj, rc):
    ins, outs = [], []
    for w in ws:
        _, r, c = w.shape
        br = _ceil_to(pl.cdiv(r, gi), BF16_ROWS)
        bc = _ceil_to(pl.cdiv(c, gj), LANES)
        last = pl.cdiv(c, bc) - 1
        assert (gi - 1) * br < r <= gi * br and last < gj

        def block(*g, last=last):
            i, j = rc(*g)
            return i, jnp.minimum(j, last)

        ins.append(pl.BlockSpec((None, br, bc), lambda *g, block=block: (layer, *block(*g))))
        outs.append(pl.BlockSpec((br, bc), block))
    return ins, outs


def _cast_block(cast_in, cast_out):
    for src, dst in zip(cast_in, cast_out):
        dst[...] = src[...].astype(BF16)


def _dot_nt(a, wt):
    return lax.dot_general(a, wt, (((1,), (1,)), ((), ())), preferred_element_type=F32)


def _norm_matmul_kernel(x_hbm, gain_ref, wt_ref, o_ref, xn_ref, x_ref, sem):
    @pl.when(pl.program_id(1) == 0)
    def _():
        copies = _row_chunk_copies(x_hbm, x_ref, sem, pl.program_id(0) * x_ref.shape[0])
        for copy in copies:
            copy.start()
        for c, copy in enumerate(copies):
            copy.wait()
            rows = slice(c * NORM_ROWS, (c + 1) * NORM_ROWS)
            xn = _rms(x_ref[rows, :], gain_ref[...]).astype(BF16)
            xn_ref[rows, :] = xn
            o_ref[rows, :] = _dot_nt(xn, wt_ref[...])

    @pl.when(pl.program_id(1) > 0)
    def _():
        o_ref[...] = _dot_nt(xn_ref[...], wt_ref[...])


def _norm_matmul(x, gain, wt, *, bm=1024, bn=1024, name):
    t, d = x.shape
    n = wt.shape[0]
    bm, bn = _tile(t, bm), _tile(n, bn)
    assert bm % NORM_ROWS == 0
    return pl.pallas_call(
        _norm_matmul_kernel,
        grid=(t // bm, n // bn),
        in_specs=[
            pl.BlockSpec(memory_space=pl.ANY),
            pl.BlockSpec((1, d), lambda i, j: (0, 0)),
            pl.BlockSpec((bn, d), lambda i, j: (j, 0)),
        ],
        out_specs=pl.BlockSpec((bm, bn), lambda i, j: (i, j)),
        out_shape=jax.ShapeDtypeStruct((t, n), F32),
        scratch_shapes=[pltpu.VMEM((bm, d), BF16), pltpu.VMEM((bm, d), F32),
                        pltpu.SemaphoreType.DMA((bm // NORM_ROWS,))],
        compiler_params=_params("parallel", "arbitrary"),
        name=name,
    )(x, gain.reshape(1, d), wt)


def _pool_kernel(zc_ref, zp_ref, wp_ref, sc_ref, o_ref, ext_ref, *, bm, c):
    i = pl.program_id(1)
    ext_ref[0:POOL_HALO, :] = jnp.where(i > 0, zp_ref[...], 0.0)
    ext_ref[POOL_HALO:POOL_HALO + bm, :] = zc_ref[...]
    t = i * bm + lax.broadcasted_iota(jnp.int32, (bm, 1), 0)
    for g, w in enumerate(POOL_WINDOWS):
        cols = slice(g * c, (g + 1) * c)
        x = zc_ref[:, cols]
        acc = x
        for k in range(1, w):
            acc = acc + ext_ref[POOL_HALO - k:POOL_HALO - k + bm, cols]
        cnt = jnp.minimum(t + 1, w).astype(F32)
        mixed = (acc / cnt - x).astype(BF16)
        y = jnp.dot(mixed, wp_ref[g], preferred_element_type=F32) * sc_ref[:, cols]
        o_ref[:, cols] = y.astype(BF16)


def _pool(z_main, w_pool, pool_scale, batch, seq, *, bm=512):
    g, c, _ = w_pool.shape
    pw = g * c
    bm = _tile(seq, bm)
    nb = seq // bm
    halo_blocks = bm // POOL_HALO
    return pl.pallas_call(
        functools.partial(_pool_kernel, bm=bm, c=c),
        grid=(batch, nb),
        in_specs=[
            pl.BlockSpec((bm, pw), lambda b, i: (b * nb + i, 0)),
            pl.BlockSpec((POOL_HALO, pw),
                         lambda b, i: (jnp.maximum((b * nb + i) * halo_blocks - 1, 0), 0)),
            pl.BlockSpec((g, c, c), lambda b, i: (0, 0, 0)),
            pl.BlockSpec((1, pw), lambda b, i: (0, 0)),
        ],
        out_specs=pl.BlockSpec((bm, pw), lambda b, i: (b * nb + i, 0)),
        out_shape=jax.ShapeDtypeStruct((batch * seq, pw), BF16),
        scratch_shapes=[pltpu.VMEM((POOL_HALO + bm, pw), F32)],
        compiler_params=_params("parallel", "parallel"),
        name="pool",
    )(z_main, z_main, w_pool, pool_scale.reshape(1, pw))


def _rope_tables(pos, inv):
    ang = pos.astype(F32) * inv
    cos, sin = jnp.cos(ang), jnp.sin(ang)
    lane = lax.broadcasted_iota(jnp.int32, ang.shape, 1)
    half = QK_ROPE // 2
    c = jnp.where(lane < QK_ROPE, cos, 0.0)
    sa = jnp.where(lane < half, -sin, 0.0)
    sb = jnp.where((lane >= half) & (lane < QK_ROPE), sin, 0.0)
    return c, sa, sb


def _rope_norm(r, gain, c, sa, sb):
    ms = jnp.sum(r * r, axis=-1, keepdims=True) * (1.0 / QK_ROPE)
    r = r * lax.rsqrt(ms + EPS) * gain
    half = QK_ROPE // 2
    return r * c + pltpu.roll(r, LANES - half, 1) * sa + pltpu.roll(r, half, 1) * sb


def _qkv_kernel(zq_ref, zkv_ref, zkr_ref, pos_ref, post_ref, inv_ref, invt_ref, gql_ref, gkvl_ref, gqt_ref,
                gkn_ref, gkr_ref, wuqt_ref, wuk_ref, wuvt_ref, qt_ref, k_ref, vt_ref,
                cqt_ref, ckv_ref, ckvt_ref, kr_ref, cost_ref, sint_ref):
    half = QK_ROPE // 2
    r1, r2, r3 = QK_NOPE, QK_NOPE + half, QK_HEAD

    @pl.when(pl.program_id(1) == 0)
    def _():
        cqt_ref[...] = _rms(zq_ref[...], gql_ref[...]).T.astype(BF16)
        ckv = _rms(zkv_ref[...], gkvl_ref[...])
        ckv_ref[...] = ckv.astype(BF16)
        ckvt_ref[...] = ckv.T.astype(BF16)
        c, sa, sb = _rope_tables(pos_ref[...], inv_ref[...])
        lane = lax.broadcasted_iota(jnp.int32, zkr_ref.shape, 1)
        zkr = jnp.where(lane < QK_ROPE, zkr_ref[...], 0.0)
        kr_ref[...] = _rope_norm(zkr, gkr_ref[...], c, sa, sb).astype(BF16)
        ang = post_ref[...].astype(F32) * invt_ref[...]
        cost_ref[...] = jnp.cos(ang)
        sint_ref[...] = jnp.sin(ang)

    bm = qt_ref.shape[-1]
    for s in range(qt_ref.shape[0]):
        q = jnp.dot(wuqt_ref[s], cqt_ref[...], preferred_element_type=F32)
        nope = q[:r1]
        rs = lax.rsqrt(jnp.mean(nope * nope, axis=0, keepdims=True) + EPS)
        qt_ref[s, :r1, :] = (nope * rs * gqt_ref[:r1, :]).astype(BF16)
        x1, x2 = q[r1:r2], q[r2:r3]
        ms = (jnp.sum(x1 * x1, axis=0, keepdims=True)
              + jnp.sum(x2 * x2, axis=0, keepdims=True)) * (1.0 / QK_ROPE)
        rs = lax.rsqrt(ms + EPS)
        y1, y2 = x1 * rs * gqt_ref[r1:r2, :], x2 * rs * gqt_ref[r2:r3, :]
        cos, sin = cost_ref[...], sint_ref[...]
        qt_ref[s, r1:r2, :] = (y1 * cos - y2 * sin).astype(BF16)
        qt_ref[s, r2:r3, :] = (y2 * cos + y1 * sin).astype(BF16)
        qt_ref[s, r3:, :] = jnp.zeros((QK_PAD - r3, bm), BF16)

        if s % 2 == 0:
            k2 = jnp.dot(ckv_ref[...], wuk_ref[s // 2], preferred_element_type=F32)
        k = k2[:, (s % 2) * QK_NOPE:(s % 2 + 1) * QK_NOPE]
        k_ref[s, :, :QK_NOPE] = _rms(k, gkn_ref[...]).astype(BF16)
        k_ref[s, :, QK_NOPE:] = kr_ref[...]
        vt = jnp.dot(wuvt_ref[s], ckvt_ref[...], preferred_element_type=F32)
        vt_ref[s, :V_HEAD, :] = vt.astype(BF16)
        vt_ref[s, V_HEAD:, :] = jnp.ones((V_ROWS - V_HEAD, bm), BF16)


def _qkv(z, pos, inv, gql, gkvl, gqt, gkn, gkr, wuqt, wuk, wuvt, batch, seq, off_q, off_kv, *, bm=512, hp=8):
    h, _, ql = wuqt.shape
    hp = min(hp, h)
    kvl = wuk.shape[1]
    bm = _tile(seq, bm)
    nb = seq // bm
    t = batch * seq
    half = QK_ROPE // 2
    assert off_q % ql == 0 and off_kv % kvl == 0 and (off_kv + kvl) % LANES == 0 and h % hp == 0 and hp % 2 == 0
    const = lambda i, hh: (0, 0)
    head = lambda i, hh: (hh, 0, 0)
    out_map = lambda i, hh: (i // nb, hh, i % nb, 0)
    out_map_t = lambda i, hh: (i // nb, hh, i % nb, 0, 0)
    invt = jnp.broadcast_to(inv[0, :half].reshape(half, 1), (half, bm))
    gqt = jnp.broadcast_to(gqt.reshape(QK_PAD, 1), (QK_PAD, bm))
    return pl.pallas_call(
        _qkv_kernel,
        grid=(t // bm, h // hp),
        in_specs=[
            pl.BlockSpec((bm, ql), lambda i, hh: (i, off_q // ql)),
            pl.BlockSpec((bm, kvl), lambda i, hh: (i, off_kv // kvl)),
            pl.BlockSpec((bm, LANES), lambda i, hh: (i, (off_kv + kvl) // LANES)),
            pl.BlockSpec((bm, 1), lambda i, hh: (i, 0)),
            pl.BlockSpec((1, bm), lambda i, hh: (0, i)),
            pl.BlockSpec((1, LANES), const),
            pl.BlockSpec((half, bm), const),
            pl.BlockSpec((1, ql), const),
            pl.BlockSpec((1, kvl), const),
            pl.BlockSpec((QK_PAD, bm), const),
            pl.BlockSpec((1, QK_NOPE), const),
            pl.BlockSpec((1, LANES), const),
            pl.BlockSpec((hp, QK_PAD, ql), head),
            pl.BlockSpec((hp // 2, kvl, 2 * QK_NOPE), head),
            pl.BlockSpec((hp, V_HEAD, kvl), head),
        ],
        out_specs=[
            pl.BlockSpec((None, hp, None, QK_PAD, bm), out_map_t),
            pl.BlockSpec((None, hp, bm, QK_PAD), out_map),
            pl.BlockSpec((None, hp, None, V_ROWS, bm), out_map_t),
        ],
        out_shape=[
            jax.ShapeDtypeStruct((batch, h, nb, QK_PAD, bm), BF16),
            jax.ShapeDtypeStruct((batch, h, seq, QK_PAD), BF16),
            jax.ShapeDtypeStruct((batch, h, nb, V_ROWS, bm), BF16),
        ],
        scratch_shapes=[
            pltpu.VMEM((ql, bm), BF16),
            pltpu.VMEM((bm, kvl), BF16),
            pltpu.VMEM((kvl, bm), BF16),
            pltpu.VMEM((bm, LANES), BF16),
            pltpu.VMEM((half, bm), F32),
            pltpu.VMEM((half, bm), F32),
        ],
        compiler_params=_params("parallel", "arbitrary"),
        name="qkv_prep",
    )(z, z, z, pos.reshape(t, 1), pos.reshape(1, t), inv, invt, gql, gkvl, gqt, gkn, gkr, wuqt, wuk, wuvt)


def _flash_kernel(qt_ref, k_ref, vt_ref, *refs, blk, chunk, n_cast):
    cast_in, o_ref, cast_out = refs[:n_cast], refs[n_cast], refs[n_cast + 1:2 * n_cast + 1]
    sa_ref, sb_ref, m_ref, acc_ref = refs[2 * n_cast + 1:]
    _cast_block(cast_in, cast_out)

    qi = pl.program_id(2)
    nch = 2 * blk // chunk
    per_blk = blk // chunk
    m_ref[...] = jnp.full(m_ref.shape, -jnp.inf, F32)
    acc_ref[...] = jnp.zeros(acc_ref.shape, F32)

    def visibility(c, diag):
        if diag is None or c * chunk >= (diag + 1) * blk:
            return "all"
        return "none" if (c + 1) * chunk <= diag * blk else "some"

    def scores(kj, s_ref, diag=None):
        start = pl.multiple_of(kj * blk, blk)
        k = k_ref[pl.ds(start, blk), :]
        for c in range(nch):
            if visibility(c, diag) != "none":
                qt = qt_ref[c // per_blk, :, (c % per_blk) * chunk:(c % per_blk + 1) * chunk]
                s_ref[c] = jnp.dot(k, qt, preferred_element_type=F32)

    def update(kj, s_ref, diag=None):
        vt = vt_ref[kj]
        for c in range(nch):
            see = visibility(c, diag)
            if see == "none":
                continue
            cols = slice(c * chunk, (c + 1) * chunk)
            st = s_ref[c]
            if see == "some":
                key = lax.broadcasted_iota(jnp.int32, st.shape, 0) + diag * blk
                qry = lax.broadcasted_iota(jnp.int32, st.shape, 1) + c * chunk
                st = jnp.where(key <= qry, st, -jnp.inf)
            m_old = m_ref[:, cols]
            m_new = jnp.maximum(m_old, jnp.max(st, axis=0, keepdims=True))
            p = jnp.exp2(st - m_new).astype(BF16)
            alpha = jnp.exp2(m_old - m_new)
            acc_ref[:, cols] = alpha * acc_ref[:, cols] + jnp.dot(vt, p, preferred_element_type=F32)
            m_ref[:, cols] = m_new

    scores(0, sa_ref)

    def pair(t):
        scores(2 * t + 1, sb_ref)
        update(2 * t, sa_ref)
        scores(2 * t + 2, sa_ref)
        update(2 * t + 1, sb_ref)

    def trip(u, carry):
        for t in range(PAIRS_PER_TRIP):
            pair(PAIRS_PER_TRIP * u + t)
        return carry

    lax.fori_loop(0, qi // PAIRS_PER_TRIP, trip, 0)
    done = qi // PAIRS_PER_TRIP * PAIRS_PER_TRIP
    for rest in range(PAIRS_PER_TRIP):
        @pl.when(qi - done == rest)
        def _():
            for t in range(rest):
                pair(done + t)
            scores(2 * qi + 1, sb_ref, diag=1)
            update(2 * qi, sa_ref, diag=0)
            update(2 * qi + 1, sb_ref, diag=1)
            o_ref[...] = (acc_ref[:V_HEAD, :] / acc_ref[V_HEAD:V_HEAD + 1, :]).T.astype(o_ref.dtype)


def _flash(qt, k, vt, cast=(), layer=0, *, chunk=256):
    b, h, nb, _, blk = qt.shape
    s = nb * blk
    chunk = min(chunk, blk)
    assert nb % 2 == 0
    cast_in, cast_out = _cast_specs(cast, layer, b * h, nb // 2, lambda bi, hi, qi: (bi * h + hi, qi))
    return pl.pallas_call(
        functools.partial(_flash_kernel, blk=blk, chunk=chunk, n_cast=len(cast)),
        grid=(b, h, nb // 2),
        in_specs=[
            pl.BlockSpec((None, None, 2, QK_PAD, blk), lambda bi, hi, qi: (bi, hi, qi, 0, 0)),
            pl.BlockSpec((None, None, s, QK_PAD), lambda bi, hi, qi: (bi, hi, 0, 0)),
            pl.BlockSpec((None, None, nb, V_ROWS, blk), lambda bi, hi, qi: (bi, hi, 0, 0, 0)),
            *cast_in,
        ],
        out_specs=[pl.BlockSpec((None, 2 * blk, V_HEAD), lambda bi, hi, qi: (bi, qi, hi)), *cast_out],
        out_shape=[jax.ShapeDtypeStruct((b, s, h * V_HEAD), BF16)]
        + [jax.ShapeDtypeStruct(w.shape[1:], BF16) for w in cast],
        scratch_shapes=[
            pltpu.VMEM((2 * blk // chunk, blk, chunk), F32),
            pltpu.VMEM((2 * blk // chunk, blk, chunk), F32),
            pltpu.VMEM((1, 2 * blk), F32),
            pltpu.VMEM((V_ROWS, 2 * blk), F32),
        ],
        compiler_params=_params("parallel", "parallel", "arbitrary"),
        name="mla_flash",
    )(qt, k, vt, *cast)


def _merge_kernel(a_ref, b_ref, wa_ref, wb_ref, gp_ref, gm_ref, o_ref):
    pa = jnp.dot(a_ref[...], wa_ref[...], preferred_element_type=F32)
    pb = jnp.dot(b_ref[...], wb_ref[...], preferred_element_type=F32)
    o_ref[...] = (jax.nn.sigmoid(gp_ref[...]) * pa + jax.nn.sigmoid(gm_ref[...]) * pb).astype(BF16)


def _merge(a, b, wa, wb, z, gate_off, *, bm=1024, bn=512):
    t, pw = a.shape
    mw = b.shape[1]
    d = wa.shape[1]
    bm, bn = _tile(t, bm), _tile(math.gcd(d, gate_off), bn)
    nj, g0 = d // bn, gate_off // bn
    return pl.pallas_call(
        _merge_kernel,
        grid=(t // bm, nj),
        in_specs=[
            pl.BlockSpec((bm, pw), lambda i, j: (i, 0)),
            pl.BlockSpec((bm, mw), lambda i, j: (i, 0)),
            pl.BlockSpec((pw, bn), lambda i, j: (0, j)),
            pl.BlockSpec((mw, bn), lambda i, j: (0, j)),
            pl.BlockSpec((bm, bn), lambda i, j: (i, g0 + j)),
            pl.BlockSpec((bm, bn), lambda i, j: (i, g0 + nj + j)),
        ],
        out_specs=pl.BlockSpec((bm, bn), lambda i, j: (i, j)),
        out_shape=jax.ShapeDtypeStruct((t, d), BF16),
        compiler_params=_params("parallel", "arbitrary"),
        name="branch_merge",
    )(a, b, wa, wb, z, z)


def _proj_residual_kernel(a_ref, w_ref, h_ref, o_ref):
    o_ref[...] = h_ref[...] + jnp.dot(a_ref[...], w_ref[...], preferred_element_type=F32)


def _proj_residual(a, w, h, *, bm=1024, bn=1024):
    t, kdim = a.shape
    d = w.shape[1]
    bm, bn = _tile(t, bm), _tile(d, bn)
    return pl.pallas_call(
        _proj_residual_kernel,
        grid=(t // bm, d // bn),
        in_specs=[
            pl.BlockSpec((bm, kdim), lambda i, j: (i, 0)),
            pl.BlockSpec((kdim, bn), lambda i, j: (0, j)),
            pl.BlockSpec((bm, bn), lambda i, j: (i, j)),
        ],
        out_specs=pl.BlockSpec((bm, bn), lambda i, j: (i, j)),
        out_shape=jax.ShapeDtypeStruct((t, d), F32),
        compiler_params=_params("parallel", "arbitrary"),
        name="out_proj",
    )(a, w, h)


def _mem_kv_kernel(mem_ref, gain_ref, w_ref, gk_ref, k_ref, v_ref):
    mn = _rms(mem_ref[...], gain_ref[...]).astype(BF16)
    kv = jnp.dot(mn, w_ref[...], preferred_element_type=F32)
    xw = X_HEADS * X_HEAD_DIM
    for hh in range(X_HEADS):
        cols = slice(hh * X_HEAD_DIM, (hh + 1) * X_HEAD_DIM)
        k_ref[:, cols] = _rms(kv[:, cols], gk_ref[...]).astype(BF16)
    v_ref[...] = kv[:, xw:].astype(BF16)


def _mem_kv(mem, gain, w_xkv, gk):
    b, m, d = mem.shape
    xw = X_HEADS * X_HEAD_DIM
    return pl.pallas_call(
        _mem_kv_kernel,
        grid=(b,),
        in_specs=[
            pl.BlockSpec((None, m, d), lambda bi: (bi, 0, 0)),
            pl.BlockSpec((1, d), lambda bi: (0, 0)),
            pl.BlockSpec((d, 2 * xw), lambda bi: (0, 0)),
            pl.BlockSpec((1, X_HEAD_DIM), lambda bi: (0, 0)),
        ],
        out_specs=[
            pl.BlockSpec((None, m, xw), lambda bi: (bi, 0, 0)),
            pl.BlockSpec((None, m, xw), lambda bi: (bi, 0, 0)),
        ],
        out_shape=[
            jax.ShapeDtypeStruct((b, m, xw), BF16),
            jax.ShapeDtypeStruct((b, m, xw), BF16),
        ],
        compiler_params=_params("parallel"),
        name="mem_kv",
    )(mem, gain.reshape(1, d), w_xkv, gk)


def _cross_kernel(h_ref, gain_ref, wq_ref, gq_ref, k_ref, v_ref, wo_ref, o_ref, xo_ref):
    h = h_ref[...]
    uq = _rms(h, gain_ref[...]).astype(BF16)
    xq = jnp.dot(uq, wq_ref[...], preferred_element_type=F32)
    scale = X_HEAD_DIM ** -0.5
    for hh in range(X_HEADS):
        cols = slice(hh * X_HEAD_DIM, (hh + 1) * X_HEAD_DIM)
        qh = _rms(xq[:, cols], gq_ref[...]).astype(BF16)
        s = lax.dot_general(qh, k_ref[:, cols], (((1,), (1,)), ((), ())),
                            preferred_element_type=F32) * scale
        p = jnp.exp(s - jnp.max(s, axis=-1, keepdims=True))
        p = (p / jnp.sum(p, axis=-1, keepdims=True)).astype(BF16)
        xo_ref[:, cols] = jnp.dot(p, v_ref[:, cols], preferred_element_type=F32).astype(BF16)
    o_ref[...] = h + jnp.dot(xo_ref[...], wo_ref[...], preferred_element_type=F32)


def _cross(h, gain, wq, gq, xk, xv, wo, seq, *, bm=512):
    t, d = h.shape
    m = xk.shape[1]
    xw = X_HEADS * X_HEAD_DIM
    bm = _tile(seq, bm)
    nb = seq // bm
    return pl.pallas_call(
        _cross_kernel,
        grid=(t // bm,),
        in_specs=[
            pl.BlockSpec((bm, d), lambda i: (i, 0)),
            pl.BlockSpec((1, d), lambda i: (0, 0)),
            pl.BlockSpec((d, xw), lambda i: (0, 0)),
            pl.BlockSpec((1, X_HEAD_DIM), lambda i: (0, 0)),
            pl.BlockSpec((None, m, xw), lambda i: (i // nb, 0, 0)),
            pl.BlockSpec((None, m, xw), lambda i: (i // nb, 0, 0)),
            pl.BlockSpec((xw, d), lambda i: (0, 0)),
        ],
        out_specs=pl.BlockSpec((bm, d), lambda i: (i, 0)),
        out_shape=jax.ShapeDtypeStruct((t, d), F32),
        scratch_shapes=[pltpu.VMEM((bm, xw), BF16)],
        compiler_params=_params("parallel"),
        name="cross_attn",
    )(h, gain.reshape(1, d), wq, gq, xk, xv, wo)


def _pad_lanes(v, width):
    return jnp.pad(v, (0, width - v.shape[0])).reshape(1, width)


def kernel(x, mem, positions, ffn1_norm, ffn1_w_gu, ffn1_w_down, mix_norm, w_in, w_pool, pool_scale,
           q_latent_norm, kv_latent_norm, w_uq, w_ukv, q_nope_norm, k_nope_norm, q_rope_norm,
           k_rope_norm, w_branch_pool, w_branch_mla, w_out, x_norm, mem_norm, w_xq, w_xkv, xq_norm,
           xk_norm, w_xo, ffn2_norm, ffn2_w_gu, ffn2_w_down):
    batch, seq, d = x.shape
    t = batch * seq
    depth = w_in.shape[0]
    pw = w_branch_pool.shape[1]
    ql = q_latent_norm.shape[1]
    kvl = kv_latent_norm.shape[1]
    heads = w_uq.shape[2] // QK_HEAD

    half = QK_ROPE // 2
    inv = 1.0 / (ROPE_THETA ** (jnp.arange(half, dtype=F32) * (2.0 / QK_ROPE)))
    inv = _pad_lanes(jnp.concatenate([inv, inv]), LANES)
    c2 = QK_HEAD ** -0.5 * math.log2(math.e)

    h = x.reshape(t, d)
    for l in range(depth):
        off_kv = pw + ql
        off_gate = off_kv + kvl + QK_ROPE
        gate_base = _ceil_to(off_gate, IN_TILE)
        h, w_in_t = _ffn(h, ffn1_norm[l], ffn1_w_gu[l].astype(BF16), ffn1_w_down[l].astype(BF16),
                         (jnp.swapaxes(w_in, 1, 2), l, off_gate, gate_base))
        z = _norm_matmul(h, mix_norm[l], w_in_t, bn=IN_TILE, name="in_proj")

        a_out = _pool(z, w_pool[l].astype(BF16), pool_scale[l], batch, seq)

        wuq = jnp.pad(w_uq[l].reshape(ql, heads, QK_HEAD), ((0, 0), (0, 0), (0, QK_PAD - QK_HEAD)))
        wuqt = wuq.transpose(1, 2, 0).astype(BF16)
        wukv = w_ukv[l].reshape(kvl, heads, QK_NOPE + V_HEAD)
        wuk = wukv[:, :, :QK_NOPE].reshape(kvl, heads // 2, 2 * QK_NOPE)
        wuk = wuk.transpose(1, 0, 2).astype(BF16)
        wuvt = wukv[:, :, QK_NOPE:].transpose(1, 2, 0).astype(BF16)
        gqt = jnp.concatenate([q_nope_norm[l], q_rope_norm[l], jnp.zeros((QK_PAD - QK_HEAD,), F32)]) * c2
        qt, k, vt = _qkv(z, positions, inv,
                         q_latent_norm[l].reshape(1, ql), kv_latent_norm[l].reshape(1, kvl), gqt,
                         k_nope_norm[l].reshape(1, QK_NOPE), _pad_lanes(k_rope_norm[l], LANES),
                         wuqt, wuk, wuvt, batch, seq, pw, off_kv)
        later = (ffn2_w_gu, ffn2_w_down, w_out, w_branch_pool, w_branch_mla)
        b_out, w_gu2, w_down2, w_out_b, w_bp, w_bm = _flash(qt, k, vt, later, l)
        b_out = b_out.reshape(t, heads * V_HEAD)

        merged = _merge(a_out, b_out, w_bp, w_bm, z, gate_base)
        h = _proj_residual(merged, w_out_b, h)

        xk, xv = _mem_kv(mem, mem_norm[l], w_xkv[l].astype(BF16), xk_norm[l].reshape(1, X_HEAD_DIM))
        h = _cross(h, x_norm[l], w_xq[l].astype(BF16), xq_norm[l].reshape(1, X_HEAD_DIM), xk, xv,
                   w_xo[l].astype(BF16), seq)

        h = _ffn(h, ffn2_norm[l], w_gu2, w_down2)
    return h.reshape(batch, seq, d)
```

```python
import functools
import math

import jax
import jax.numpy as jnp
from jax import lax
from jax.experimental import pallas as pl
from jax.experimental.pallas import tpu as pltpu

F32 = jnp.float32
BF16 = jnp.bfloat16

POOL_WINDOWS = (2, 4, 8, 16)
POOL_HALO = 16
QK_NOPE = 128
QK_ROPE = 64
QK_HEAD = QK_NOPE + QK_ROPE
QK_PAD = 256
V_HEAD = 128
V_ROWS = V_HEAD + 16
X_HEADS = 4
X_HEAD_DIM = 128
ROPE_THETA = 10000.0
EPS = 1e-6
LANES = 128
BF16_ROWS = 16
VMEM_LIMIT = 62 * 1024 * 1024
NORM_ROWS = 256
IN_TILE = 1024
PAIRS_PER_TRIP = 4


def _params(*sem):
    return pltpu.CompilerParams(dimension_semantics=sem, vmem_limit_bytes=VMEM_LIMIT)


def _tile(n, pref):
    if n <= pref:
        return n
    t = (pref // LANES) * LANES
    while t >= LANES:
        if n % t == 0:
            return t
        t -= LANES
    raise ValueError(f"no lane-aligned tile of {n} below {pref}")


def _ceil_to(x, m):
    return -(-x // m) * m


def _rms(x, gain):
    ms = jnp.mean(x * x, axis=-1, keepdims=True)
    return x * lax.rsqrt(ms + EPS) * gain


def _row_chunk_copies(src_hbm, dst_ref, sem, row0):
    return [pltpu.make_async_copy(src_hbm.at[pl.ds(row0 + r, NORM_ROWS), :],
                                  dst_ref.at[pl.ds(r, NORM_ROWS), :], sem.at[c])
            for c, r in enumerate(range(0, dst_ref.shape[0], NORM_ROWS))]


def _ffn_kernel(x_hbm, gain_ref, wg_ref, wu_ref, wd_ref, *refs, regroup_blocks):
    if regroup_blocks is None:
        o_ref, xn_ref, sem = refs
    else:
        src_ref, o_ref, dst_ref, xn_ref, sem = refs
        step = pl.program_id(0) * pl.num_programs(1) + pl.program_id(1)
        dst_ref[...] = jnp.where(step < regroup_blocks, src_ref[...], 0.0).astype(BF16)

    def add_tile(base, xn):
        g = jnp.dot(xn, wg_ref[...], preferred_element_type=F32)
        u = jnp.dot(xn, wu_ref[...], preferred_element_type=F32)
        hid = (g * jax.nn.sigmoid(g) * u * 0.5).astype(BF16)
        return base + jnp.dot(hid, wd_ref[...], preferred_element_type=F32)

    @pl.when(pl.program_id(1) == 0)
    def _():
        copies = _row_chunk_copies(x_hbm, o_ref, sem, pl.program_id(0) * o_ref.shape[0])
        for copy in copies:
            copy.start()
        for c, copy in enumerate(copies):
            copy.wait()
            rows = slice(c * NORM_ROWS, (c + 1) * NORM_ROWS)
            x = o_ref[rows, :]
            xn = _rms(x, gain_ref[...]).astype(BF16)
            xn_ref[rows, :] = xn
            o_ref[rows, :] = add_tile(x, xn)

    @pl.when(pl.program_id(1) > 0)
    def _():
        o_ref[...] = add_tile(o_ref[...], xn_ref[...])


def _regroup_specs(wt, layer, split, base, steps):
    _, n, d = wt.shape
    assert base > split
    rb = math.gcd(split, base - split)
    nblk, first_gap, gap = pl.cdiv(n, rb), split // rb, (base - split) // rb
    assert rb % BF16_ROWS == 0 and nblk + gap <= steps[0] * steps[1]

    def src_block(i, j):
        return jnp.minimum(i * steps[1] + j, nblk - 1)

    def dst_block(i, j):
        s = i * steps[1] + j
        copied = jnp.where(s < first_gap, s, s + gap)
        zeroed = first_gap + jnp.minimum(s - nblk, gap - 1)
        return jnp.where(s < nblk, copied, zeroed)

    return (pl.BlockSpec((None, rb, d), lambda i, j: (layer, src_block(i, j), 0)),
            pl.BlockSpec((rb, d), lambda i, j: (dst_block(i, j), 0)),
            jax.ShapeDtypeStruct((n + base - split, d), BF16), nblk)


def _ffn(x, gain, w_gu, w_down, regroup=None, *, bm=1024, bf=256):
    t, d = x.shape
    f = w_down.shape[0]
    bm, bf = _tile(t, bm), _tile(f, bf)
    assert bm % NORM_ROWS == 0
    nf = f // bf
    grid = (t // bm, nf)
    cast_in, cast_out, cast_shape, cast_args, nblk = [], [], [], [], None
    if regroup is not None:
        src, dst, shape, nblk = _regroup_specs(*regroup, grid)
        cast_in, cast_out, cast_shape, cast_args = [src], [dst], [shape], [regroup[0]]
    out = pl.pallas_call(
        functools.partial(_ffn_kernel, regroup_blocks=nblk),
        grid=grid,
        in_specs=[
            pl.BlockSpec(memory_space=pl.ANY),
            pl.BlockSpec((1, d), lambda i, j: (0, 0)),
            pl.BlockSpec((d, bf), lambda i, j: (0, j)),
            pl.BlockSpec((d, bf), lambda i, j: (0, j + nf)),
            pl.BlockSpec((bf, d), lambda i, j: (j, 0)),
            *cast_in,
        ],
        out_specs=[pl.BlockSpec((bm, d), lambda i, j: (i, 0)), *cast_out],
        out_shape=[jax.ShapeDtypeStruct((t, d), F32), *cast_shape],
        scratch_shapes=[pltpu.VMEM((bm, d), BF16), pltpu.SemaphoreType.DMA((bm // NORM_ROWS,))],
        compiler_params=_params("arbitrary", "arbitrary"),
        name="ffn",
    )(x, gain.reshape(1, d), w_gu, w_gu, w_down, *cast_args)
    return out if regroup is not None else out[0]


def _cast_specs(ws, layer, gi, gj, rc):
    ins, outs = [], []
    for w in ws:
        _, r, c = w.shape
        br = _ceil_to(pl.cdiv(r, gi), BF16_ROWS)
        bc = _ceil_to(pl.cdiv(c, gj), LANES)
        last = pl.cdiv(c, bc) - 1
        assert (gi - 1) * br < r <= gi * br and last < gj

        def block(*g, last=last):
            i, j = rc(*g)
            return i, jnp.minimum(j, last)

        ins.append(pl.BlockSpec((None, br, bc), lambda *g, block=block: (layer, *block(*g))))
        outs.append(pl.BlockSpec((br, bc), block))
    return ins, outs


def _cast_block(cast_in, cast_out):
    for src, dst in zip(cast_in, cast_out):
        dst[...] = src[...].astype(BF16)


def _dot_nt(a, wt):
    return lax.dot_general(a, wt, (((1,), (1,)), ((), ())), preferred_element_type=F32)


def _norm_matmul_kernel(x_hbm, gain_ref, wt_ref, o_ref, xn_ref, x_ref, sem):
    @pl.when(pl.program_id(1) == 0)
    def _():
        copies = _row_chunk_copies(x_hbm, x_ref, sem, pl.program_id(0) * x_ref.shape[0])
        for copy in copies:
            copy.start()
        for c, copy in enumerate(copies):
            copy.wait()
            rows = slice(c * NORM_ROWS, (c + 1) * NORM_ROWS)
            xn = _rms(x_ref[rows, :], gain_ref[...]).astype(BF16)
            xn_ref[rows, :] = xn
            o_ref[rows, :] = _dot_nt(xn, wt_ref[...])

    @pl.when(pl.program_id(1) > 0)
    def _():
        o_ref[...] = _dot_nt(xn_ref[...], wt_ref[...])


def _norm_matmul(x, gain, wt, *, bm=1024, bn=1024, name):
    t, d = x.shape
    n = wt.shape[0]
    bm, bn = _tile(t, bm), _tile(n, bn)
    assert bm % NORM_ROWS == 0
    return pl.pallas_call(
        _norm_matmul_kernel,
        grid=(t // bm, n // bn),
        in_specs=[
            pl.BlockSpec(memory_space=pl.ANY),
            pl.BlockSpec((1, d), lambda i, j: (0, 0)),
            pl.BlockSpec((bn, d), lambda i, j: (j, 0)),
        ],
        out_specs=pl.BlockSpec((bm, bn), lambda i, j: (i, j)),
        out_shape=jax.ShapeDtypeStruct((t, n), F32),
        scratch_shapes=[pltpu.VMEM((bm, d), BF16), pltpu.VMEM((bm, d), F32),
                        pltpu.SemaphoreType.DMA((bm // NORM_ROWS,))],
        compiler_params=_params("parallel", "arbitrary"),
        name=name,
    )(x, gain.reshape(1, d), wt)


def _pool_kernel(zc_ref, zp_ref, wp_ref, sc_ref, o_ref, ext_ref, *, bm, c):
    i = pl.program_id(1)
    ext_ref[0:POOL_HALO, :] = jnp.where(i > 0, zp_ref[...], 0.0)
    ext_ref[POOL_HALO:POOL_HALO + bm, :] = zc_ref[...]
    t = i * bm + lax.broadcasted_iota(jnp.int32, (bm, 1), 0)
    for g, w in enumerate(POOL_WINDOWS):
        cols = slice(g * c, (g + 1) * c)
        x = zc_ref[:, cols]
        acc = x
        for k in range(1, w):
            acc = acc + ext_ref[POOL_HALO - k:POOL_HALO - k + bm, cols]
        cnt = jnp.minimum(t + 1, w).astype(F32)
        mixed = (acc / cnt - x).astype(BF16)
        y = jnp.dot(mixed, wp_ref[g], preferred_element_type=F32) * sc_ref[:, cols]
        o_ref[:, cols] = y.astype(BF16)


def _pool(z_main, w_pool, pool_scale, batch, seq, *, bm=512):
    g, c, _ = w_pool.shape
    pw = g * c
    bm = _tile(seq, bm)
    nb = seq // bm
    halo_blocks = bm // POOL_HALO
    return pl.pallas_call(
        functools.partial(_pool_kernel, bm=bm, c=c),
        grid=(batch, nb),
        in_specs=[
            pl.BlockSpec((bm, pw), lambda b, i: (b * nb + i, 0)),
            pl.BlockSpec((POOL_HALO, pw),
                         lambda b, i: (jnp.maximum((b * nb + i) * halo_blocks - 1, 0), 0)),
            pl.BlockSpec((g, c, c), lambda b, i: (0, 0, 0)),
            pl.BlockSpec((1, pw), lambda b, i: (0, 0)),
        ],
        out_specs=pl.BlockSpec((bm, pw), lambda b, i: (b * nb + i, 0)),
        out_shape=jax.ShapeDtypeStruct((batch * seq, pw), BF16),
        scratch_shapes=[pltpu.VMEM((POOL_HALO + bm, pw), F32)],
        compiler_params=_params("parallel", "parallel"),
        name="pool",
    )(z_main, z_main, w_pool, pool_scale.reshape(1, pw))


def _rope_tables(pos, inv):
    ang = pos.astype(F32) * inv
    cos, sin = jnp.cos(ang), jnp.sin(ang)
    lane = lax.broadcasted_iota(jnp.int32, ang.shape, 1)
    half = QK_ROPE // 2
    c = jnp.where(lane < QK_ROPE, cos, 0.0)
    sa = jnp.where(lane < half, -sin, 0.0)
    sb = jnp.where((lane >= half) & (lane < QK_ROPE), sin, 0.0)
    return c, sa, sb


def _rope_norm(r, gain, c, sa, sb):
    ms = jnp.sum(r * r, axis=-1, keepdims=True) * (1.0 / QK_ROPE)
    r = r * lax.rsqrt(ms + EPS) * gain
    half = QK_ROPE // 2
    return r * c + pltpu.roll(r, LANES - half, 1) * sa + pltpu.roll(r, half, 1) * sb


def _qkv_kernel(zq_ref, zkv_ref, zkr_ref, pos_ref, post_ref, inv_ref, invt_ref, gql_ref, gkvl_ref, gqt_ref,
                gkn_ref, gkr_ref, wuqt_ref, wuk_ref, wuvt_ref, qt_ref, k_ref, vt_ref,
                cqt_ref, ckv_ref, ckvt_ref, kr_ref, cost_ref, sint_ref):
    half = QK_ROPE // 2
    r1, r2, r3 = QK_NOPE, QK_NOPE + half, QK_HEAD

    @pl.when(pl.program_id(1) == 0)
    def _():
        cqt_ref[...] = _rms(zq_ref[...], gql_ref[...]).T.astype(BF16)
        ckv = _rms(zkv_ref[...], gkvl_ref[...])
        ckv_ref[...] = ckv.astype(BF16)
        ckvt_ref[...] = ckv.T.astype(BF16)
        c, sa, sb = _rope_tables(pos_ref[...], inv_ref[...])
        lane = lax.broadcasted_iota(jnp.int32, zkr_ref.shape, 1)
        zkr = jnp.where(lane < QK_ROPE, zkr_ref[...], 0.0)
        kr_ref[...] = _rope_norm(zkr, gkr_ref[...], c, sa, sb).astype(BF16)
        ang = post_ref[...].astype(F32) * invt_ref[...]
        cost_ref[...] = jnp.cos(ang)
        sint_ref[...] = jnp.sin(ang)

    bm = qt_ref.shape[-1]
    for s in range(qt_ref.shape[0]):
        q = jnp.dot(wuqt_ref[s], cqt_ref[...], preferred_element_type=F32)
        nope = q[:r1]
        rs = lax.rsqrt(jnp.mean(nope * nope, axis=0, keepdims=True) + EPS)
        qt_ref[s, :r1, :] = (nope * rs * gqt_ref[:r1, :]).astype(BF16)
        x1, x2 = q[r1:r2], q[r2:r3]
        ms = (jnp.sum(x1 * x1, axis=0, keepdims=True)
              + jnp.sum(x2 * x2, axis=0, keepdims=True)) * (1.0 / QK_ROPE)
        rs = lax.rsqrt(ms + EPS)
        y1, y2 = x1 * rs * gqt_ref[r1:r2, :], x2 * rs * gqt_ref[r2:r3, :]
        cos, sin = cost_ref[...], sint_ref[...]
        qt_ref[s, r1:r2, :] = (y1 * cos - y2 * sin).astype(BF16)
        qt_ref[s, r2:r3, :] = (y2 * cos + y1 * sin).astype(BF16)
        qt_ref[s, r3:, :] = jnp.zeros((QK_PAD - r3, bm), BF16)

        k = jnp.dot(ckv_ref[...], wuk_ref[s], preferred_element_type=F32)
        k_ref[s, :, :QK_NOPE] = _rms(k, gkn_ref[...]).astype(BF16)
        k_ref[s, :, QK_NOPE:] = kr_ref[...]
        vt = jnp.dot(wuvt_ref[s], ckvt_ref[...], preferred_element_type=F32)
        vt_ref[s, :V_HEAD, :] = vt.astype(BF16)
        vt_ref[s, V_HEAD:, :] = jnp.ones((V_ROWS - V_HEAD, bm), BF16)


def _qkv(z, pos, inv, gql, gkvl, gqt, gkn, gkr, wuqt, wuk, wuvt, batch, seq, off_q, off_kv, *, bm=512, hp=8):
    h, _, ql = wuqt.shape
    hp = min(hp, h)
    kvl = wuk.shape[1]
    bm = _tile(seq, bm)
    nb = seq // bm
    t = batch * seq
    half = QK_ROPE // 2
    assert off_q % ql == 0 and off_kv % kvl == 0 and (off_kv + kvl) % LANES == 0 and h % hp == 0
    const = lambda i, hh: (0, 0)
    head = lambda i, hh: (hh, 0, 0)
    out_map = lambda i, hh: (i // nb, hh, i % nb, 0)
    out_map_t = lambda i, hh: (i // nb, hh, i % nb, 0, 0)
    invt = jnp.broadcast_to(inv[0, :half].reshape(half, 1), (half, bm))
    gqt = jnp.broadcast_to(gqt.reshape(QK_PAD, 1), (QK_PAD, bm))
    return pl.pallas_call(
        _qkv_kernel,
        grid=(t // bm, h // hp),
        in_specs=[
            pl.BlockSpec((bm, ql), lambda i, hh: (i, off_q // ql)),
            pl.BlockSpec((bm, kvl), lambda i, hh: (i, off_kv // kvl)),
            pl.BlockSpec((bm, LANES), lambda i, hh: (i, (off_kv + kvl) // LANES)),
            pl.BlockSpec((bm, 1), lambda i, hh: (i, 0)),
            pl.BlockSpec((1, bm), lambda i, hh: (0, i)),
            pl.BlockSpec((1, LANES), const),
            pl.BlockSpec((half, bm), const),
            pl.BlockSpec((1, ql), const),
            pl.BlockSpec((1, kvl), const),
            pl.BlockSpec((QK_PAD, bm), const),
            pl.BlockSpec((1, QK_NOPE), const),
            pl.BlockSpec((1, LANES), const),
            pl.BlockSpec((hp, QK_PAD, ql), head),
            pl.BlockSpec((hp, kvl, QK_NOPE), head),
            pl.BlockSpec((hp, V_HEAD, kvl), head),
        ],
        out_specs=[
            pl.BlockSpec((None, hp, None, QK_PAD, bm), out_map_t),
            pl.BlockSpec((None, hp, bm, QK_PAD), out_map),
            pl.BlockSpec((None, hp, None, V_ROWS, bm), out_map_t),
        ],
        out_shape=[
            jax.ShapeDtypeStruct((batch, h, nb, QK_PAD, bm), BF16),
            jax.ShapeDtypeStruct((batch, h, seq, QK_PAD), BF16),
            jax.ShapeDtypeStruct((batch, h, nb, V_ROWS, bm), BF16),
        ],
        scratch_shapes=[
            pltpu.VMEM((ql, bm), BF16),
            pltpu.VMEM((bm, kvl), BF16),
            pltpu.VMEM((kvl, bm), BF16),
            pltpu.VMEM((bm, LANES), BF16),
            pltpu.VMEM((half, bm), F32),
            pltpu.VMEM((half, bm), F32),
        ],
        compiler_params=_params("parallel", "arbitrary"),
        name="qkv_prep",
    )(z, z, z, pos.reshape(t, 1), pos.reshape(1, t), inv, invt, gql, gkvl, gqt, gkn, gkr, wuqt, wuk, wuvt)


def _flash_kernel(qt_ref, k_ref, vt_ref, *refs, blk, chunk, n_cast):
    cast_in, o_ref, cast_out = refs[:n_cast], refs[n_cast], refs[n_cast + 1:2 * n_cast + 1]
    sa_ref, sb_ref, m_ref, acc_ref = refs[2 * n_cast + 1:]
    _cast_block(cast_in, cast_out)

    qi = pl.program_id(2)
    nch = 2 * blk // chunk
    per_blk = blk // chunk
    m_ref[...] = jnp.full(m_ref.shape, -jnp.inf, F32)
    acc_ref[...] = jnp.zeros(acc_ref.shape, F32)

    def visibility(c, diag):
        if diag is None or c * chunk >= (diag + 1) * blk:
            return "all"
        return "none" if (c + 1) * chunk <= diag * blk else "some"

    def scores(kj, s_ref, diag=None):
        start = pl.multiple_of(kj * blk, blk)
        k = k_ref[pl.ds(start, blk), :]
        for c in range(nch):
            if visibility(c, diag) != "none":
                qt = qt_ref[c // per_blk, :, (c % per_blk) * chunk:(c % per_blk + 1) * chunk]
                s_ref[c] = jnp.dot(k, qt, preferred_element_type=F32)

    def update(kj, s_ref, diag=None):
        vt = vt_ref[kj]
        for c in range(nch):
            see = visibility(c, diag)
            if see == "none":
                continue
            cols = slice(c * chunk, (c + 1) * chunk)
            st = s_ref[c]
            if see == "some":
                key = lax.broadcasted_iota(jnp.int32, st.shape, 0) + diag * blk
                qry = lax.broadcasted_iota(jnp.int32, st.shape, 1) + c * chunk
                st = jnp.where(key <= qry, st, -jnp.inf)
            m_old = m_ref[:, cols]
            m_new = jnp.maximum(m_old, jnp.max(st, axis=0, keepdims=True))
            p = jnp.exp2(st - m_new).astype(BF16)
            alpha = jnp.exp2(m_old - m_new)
            acc_ref[:, cols] = alpha * acc_ref[:, cols] + jnp.dot(vt, p, preferred_element_type=F32)
            m_ref[:, cols] = m_new

    scores(0, sa_ref)

    def pair(t):
        scores(2 * t + 1, sb_ref)
        update(2 * t, sa_ref)
        scores(2 * t + 2, sa_ref)
        update(2 * t + 1, sb_ref)

    def trip(u, carry):
        for t in range(PAIRS_PER_TRIP):
            pair(PAIRS_PER_TRIP * u + t)
        return carry

    lax.fori_loop(0, qi // PAIRS_PER_TRIP, trip, 0)
    done = qi // PAIRS_PER_TRIP * PAIRS_PER_TRIP
    for rest in range(PAIRS_PER_TRIP):
        @pl.when(qi - done == rest)
        def _():
            for t in range(rest):
                pair(done + t)
            scores(2 * qi + 1, sb_ref, diag=1)
            update(2 * qi, sa_ref, diag=0)
            update(2 * qi + 1, sb_ref, diag=1)
            o_ref[...] = (acc_ref[:V_HEAD, :] / acc_ref[V_HEAD:V_HEAD + 1, :]).T.astype(o_ref.dtype)


def _flash(qt, k, vt, cast=(), layer=0, *, chunk=256):
    b, h, nb, _, blk = qt.shape
    s = nb * blk
    chunk = min(chunk, blk)
    assert nb % 2 == 0
    cast_in, cast_out = _cast_specs(cast, layer, b * h, nb // 2, lambda bi, hi, qi: (bi * h + hi, qi))
    return pl.pallas_call(
        functools.partial(_flash_kernel, blk=blk, chunk=chunk, n_cast=len(cast)),
        grid=(b, h, nb // 2),
        in_specs=[
            pl.BlockSpec((None, None, 2, QK_PAD, blk), lambda bi, hi, qi: (bi, hi, qi, 0, 0)),
            pl.BlockSpec((None, None, s, QK_PAD), lambda bi, hi, qi: (bi, hi, 0, 0)),
            pl.BlockSpec((None, None, nb, V_ROWS, blk), lambda bi, hi, qi: (bi, hi, 0, 0, 0)),
            *cast_in,
        ],
        out_specs=[pl.BlockSpec((None, 2 * blk, V_HEAD), lambda bi, hi, qi: (bi, qi, hi)), *cast_out],
        out_shape=[jax.ShapeDtypeStruct((b, s, h * V_HEAD), BF16)]
        + [jax.ShapeDtypeStruct(w.shape[1:], BF16) for w in cast],
        scratch_shapes=[
            pltpu.VMEM((2 * blk // chunk, blk, chunk), F32),
            pltpu.VMEM((2 * blk // chunk, blk, chunk), F32),
            pltpu.VMEM((1, 2 * blk), F32),
            pltpu.VMEM((V_ROWS, 2 * blk), F32),
        ],
        compiler_params=_params("parallel", "parallel", "arbitrary"),
        name="mla_flash",
    )(qt, k, vt, *cast)


def _merge_kernel(a_ref, b_ref, wa_ref, wb_ref, gp_ref, gm_ref, o_ref):
    pa = jnp.dot(a_ref[...], wa_ref[...], preferred_element_type=F32)
    pb = jnp.dot(b_ref[...], wb_ref[...], preferred_element_type=F32)
    o_ref[...] = (jax.nn.sigmoid(gp_ref[...]) * pa + jax.nn.sigmoid(gm_ref[...]) * pb).astype(BF16)


def _merge(a, b, wa, wb, z, gate_off, *, bm=1024, bn=512):
    t, pw = a.shape
    mw = b.shape[1]
    d = wa.shape[1]
    bm, bn = _tile(t, bm), _tile(math.gcd(d, gate_off), bn)
    nj, g0 = d // bn, gate_off // bn
    return pl.pallas_call(
        _merge_kernel,
        grid=(t // bm, nj),
        in_specs=[
            pl.BlockSpec((bm, pw), lambda i, j: (i, 0)),
            pl.BlockSpec((bm, mw), lambda i, j: (i, 0)),
            pl.BlockSpec((pw, bn), lambda i, j: (0, j)),
            pl.BlockSpec((mw, bn), lambda i, j: (0, j)),
            pl.BlockSpec((bm, bn), lambda i, j: (i, g0 + j)),
            pl.BlockSpec((bm, bn), lambda i, j: (i, g0 + nj + j)),
        ],
        out_specs=pl.BlockSpec((bm, bn), lambda i, j: (i, j)),
        out_shape=jax.ShapeDtypeStruct((t, d), BF16),
        compiler_params=_params("parallel", "arbitrary"),
        name="branch_merge",
    )(a, b, wa, wb, z, z)


def _proj_residual_kernel(a_ref, w_ref, h_ref, o_ref):
    o_ref[...] = h_ref[...] + jnp.dot(a_ref[...], w_ref[...], preferred_element_type=F32)


def _proj_residual(a, w, h, *, bm=1024, bn=1024):
    t, kdim = a.shape
    d = w.shape[1]
    bm, bn = _tile(t, bm), _tile(d, bn)
    return pl.pallas_call(
        _proj_residual_kernel,
        grid=(t // bm, d // bn),
        in_specs=[
            pl.BlockSpec((bm, kdim), lambda i, j: (i, 0)),
            pl.BlockSpec((kdim, bn), lambda i, j: (0, j)),
            pl.BlockSpec((bm, bn), lambda i, j: (i, j)),
        ],
        out_specs=pl.BlockSpec((bm, bn), lambda i, j: (i, j)),
        out_shape=jax.ShapeDtypeStruct((t, d), F32),
        compiler_params=_params("parallel", "arbitrary"),
        name="out_proj",
    )(a, w, h)


def _mem_kv_kernel(mem_ref, gain_ref, w_ref, gk_ref, k_ref, v_ref):
    mn = _rms(mem_ref[...], gain_ref[...]).astype(BF16)
    kv = jnp.dot(mn, w_ref[...], preferred_element_type=F32)
    xw = X_HEADS * X_HEAD_DIM
    for hh in range(X_HEADS):
        cols = slice(hh * X_HEAD_DIM, (hh + 1) * X_HEAD_DIM)
        k_ref[:, cols] = _rms(kv[:, cols], gk_ref[...]).astype(BF16)
    v_ref[...] = kv[:, xw:].astype(BF16)


def _mem_kv(mem, gain, w_xkv, gk):
    b, m, d = mem.shape
    xw = X_HEADS * X_HEAD_DIM
    return pl.pallas_call(
        _mem_kv_kernel,
        grid=(b,),
        in_specs=[
            pl.BlockSpec((None, m, d), lambda bi: (bi, 0, 0)),
            pl.BlockSpec((1, d), lambda bi: (0, 0)),
            pl.BlockSpec((d, 2 * xw), lambda bi: (0, 0)),
            pl.BlockSpec((1, X_HEAD_DIM), lambda bi: (0, 0)),
        ],
        out_specs=[
            pl.BlockSpec((None, m, xw), lambda bi: (bi, 0, 0)),
            pl.BlockSpec((None, m, xw), lambda bi: (bi, 0, 0)),
        ],
        out_shape=[
            jax.ShapeDtypeStruct((b, m, xw), BF16),
            jax.ShapeDtypeStruct((b, m, xw), BF16),
        ],
        compiler_params=_params("parallel"),
        name="mem_kv",
    )(mem, gain.reshape(1, d), w_xkv, gk)


def _cross_kernel(h_ref, gain_ref, wq_ref, gq_ref, k_ref, v_ref, wo_ref, o_ref, xo_ref):
    h = h_ref[...]
    uq = _rms(h, gain_ref[...]).astype(BF16)
    xq = jnp.dot(uq, wq_ref[...], preferred_element_type=F32)
    scale = X_HEAD_DIM ** -0.5
    for hh in range(X_HEADS):
        cols = slice(hh * X_HEAD_DIM, (hh + 1) * X_HEAD_DIM)
        qh = _rms(xq[:, cols], gq_ref[...]).astype(BF16)
        s = lax.dot_general(qh, k_ref[:, cols], (((1,), (1,)), ((), ())),
                            preferred_element_type=F32) * scale
        p = jnp.exp(s - jnp.max(s, axis=-1, keepdims=True))
        p = (p / jnp.sum(p, axis=-1, keepdims=True)).astype(BF16)
        xo_ref[:, cols] = jnp.dot(p, v_ref[:, cols], preferred_element_type=F32).astype(BF16)
    o_ref[...] = h + jnp.dot(xo_ref[...], wo_ref[...], preferred_element_type=F32)


def _cross(h, gain, wq, gq, xk, xv, wo, seq, *, bm=512):
    t, d = h.shape
    m = xk.shape[1]
    xw = X_HEADS * X_HEAD_DIM
    bm = _tile(seq, bm)
    nb = seq // bm
    return pl.pallas_call(
        _cross_kernel,
        grid=(t // bm,),
        in_specs=[
            pl.BlockSpec((bm, d), lambda i: (i, 0)),
            pl.BlockSpec((1, d), lambda i: (0, 0)),
            pl.BlockSpec((d, xw), lambda i: (0, 0)),
            pl.BlockSpec((1, X_HEAD_DIM), lambda i: (0, 0)),
            pl.BlockSpec((None, m, xw), lambda i: (i // nb, 0, 0)),
            pl.BlockSpec((None, m, xw), lambda i: (i // nb, 0, 0)),
            pl.BlockSpec((xw, d), lambda i: (0, 0)),
        ],
        out_specs=pl.BlockSpec((bm, d), lambda i: (i, 0)),
        out_shape=jax.ShapeDtypeStruct((t, d), F32),
        scratch_shapes=[pltpu.VMEM((bm, xw), BF16)],
        compiler_params=_params("parallel"),
        name="cross_attn",
    )(h, gain.reshape(1, d), wq, gq, xk, xv, wo)


def _pad_lanes(v, width):
    return jnp.pad(v, (0, width - v.shape[0])).reshape(1, width)


def kernel(x, mem, positions, ffn1_norm, ffn1_w_gu, ffn1_w_down, mix_norm, w_in, w_pool, pool_scale,
           q_latent_norm, kv_latent_norm, w_uq, w_ukv, q_nope_norm, k_nope_norm, q_rope_norm,
           k_rope_norm, w_branch_pool, w_branch_mla, w_out, x_norm, mem_norm, w_xq, w_xkv, xq_norm,
           xk_norm, w_xo, ffn2_norm, ffn2_w_gu, ffn2_w_down):
    batch, seq, d = x.shape
    t = batch * seq
    depth = w_in.shape[0]
    pw = w_branch_pool.shape[1]
    ql = q_latent_norm.shape[1]
    kvl = kv_latent_norm.shape[1]
    heads = w_uq.shape[2] // QK_HEAD

    half = QK_ROPE // 2
    inv = 1.0 / (ROPE_THETA ** (jnp.arange(half, dtype=F32) * (2.0 / QK_ROPE)))
    inv = _pad_lanes(jnp.concatenate([inv, inv]), LANES)
    c2 = QK_HEAD ** -0.5 * math.log2(math.e)

    h = x.reshape(t, d)
    for l in range(depth):
        off_kv = pw + ql
        off_gate = off_kv + kvl + QK_ROPE
        gate_base = _ceil_to(off_gate, IN_TILE)
        h, w_in_t = _ffn(h, ffn1_norm[l], ffn1_w_gu[l].astype(BF16), ffn1_w_down[l].astype(BF16),
                         (jnp.swapaxes(w_in, 1, 2), l, off_gate, gate_base))
        z = _norm_matmul(h, mix_norm[l], w_in_t, bn=IN_TILE, name="in_proj")

        a_out = _pool(z, w_pool[l].astype(BF16), pool_scale[l], batch, seq)

        wuq = jnp.pad(w_uq[l].reshape(ql, heads, QK_HEAD), ((0, 0), (0, 0), (0, QK_PAD - QK_HEAD)))
        wuqt = wuq.transpose(1, 2, 0).astype(BF16)
        wukv = w_ukv[l].reshape(kvl, heads, QK_NOPE + V_HEAD)
        wuk = wukv[:, :, :QK_NOPE].transpose(1, 0, 2).astype(BF16)
        wuvt = wukv[:, :, QK_NOPE:].transpose(1, 2, 0).astype(BF16)
        gqt = jnp.concatenate([q_nope_norm[l], q_rope_norm[l], jnp.zeros((QK_PAD - QK_HEAD,), F32)]) * c2
        qt, k, vt = _qkv(z, positions, inv,
                         q_latent_norm[l].reshape(1, ql), kv_latent_norm[l].reshape(1, kvl), gqt,
                         k_nope_norm[l].reshape(1, QK_NOPE), _pad_lanes(k_rope_norm[l], LANES),
                         wuqt, wuk, wuvt, batch, seq, pw, off_kv)
        later = (ffn2_w_gu, ffn2_w_down, w_out, w_branch_pool, w_branch_mla)
        b_out, w_gu2, w_down2, w_out_b, w_bp, w_bm = _flash(qt, k, vt, later, l)
        b_out = b_out.reshape(t, heads * V_HEAD)

        merged = _merge(a_out, b_out, w_bp, w_bm, z, gate_base)
        h = _proj_residual(merged, w_out_b, h)

        xk, xv = _mem_kv(mem, mem_norm[l], w_xkv[l].astype(BF16), xk_norm[l].reshape(1, X_HEAD_DIM))
        h = _cross(h, x_norm[l], w_xq[l].astype(BF16), xq_norm[l].reshape(1, X_HEAD_DIM), xk, xv,
                   w_xo[l].astype(BF16), seq)

        h = _ffn(h, ffn2_norm[l], w_gu2, w_down2)
    return h.reshape(batch, seq, d)
```

```python
import functools
import math

import jax
import jax.numpy as jnp
from jax import lax
from jax.experimental import pallas as pl
from jax.experimental.pallas import tpu as pltpu

F32 = jnp.float32
BF16 = jnp.bfloat16

POOL_WINDOWS = (2, 4, 8, 16)
POOL_HALO = 16
QK_NOPE = 128
QK_ROPE = 64
QK_HEAD = QK_NOPE + QK_ROPE
QK_PAD = 256
V_HEAD = 128
V_ROWS = V_HEAD + 16
X_HEADS = 4
X_HEAD_DIM = 128
ROPE_THETA = 10000.0
EPS = 1e-6
LANES = 128
BF16_ROWS = 16
VMEM_LIMIT = 62 * 1024 * 1024
NORM_ROWS = 256
IN_TILE = 1024
PAIRS_PER_TRIP = 4


def _params(*sem):
    return pltpu.CompilerParams(dimension_semantics=sem, vmem_limit_bytes=VMEM_LIMIT)


def _tile(n, pref):
    if n <= pref:
        return n
    t = (pref // LANES) * LANES
    while t >= LANES:
        if n % t == 0:
            return t
        t -= LANES
    raise ValueError(f"no lane-aligned tile of {n} below {pref}")


def _ceil_to(x, m):
    return -(-x // m) * m


def _rms(x, gain):
    ms = jnp.mean(x * x, axis=-1, keepdims=True)
    return x * lax.rsqrt(ms + EPS) * gain


def _row_chunk_copies(src_hbm, dst_ref, sem, row0):
    return [pltpu.make_async_copy(src_hbm.at[pl.ds(row0 + r, NORM_ROWS), :],
                                  dst_ref.at[pl.ds(r, NORM_ROWS), :], sem.at[c])
            for c, r in enumerate(range(0, dst_ref.shape[0], NORM_ROWS))]


def _ffn_kernel(x_hbm, gain_ref, wg_ref, wu_ref, wd_ref, *refs, regroup_blocks):
    if regroup_blocks is None:
        o_ref, xn_ref, sem = refs
    else:
        src_ref, o_ref, dst_ref, xn_ref, sem = refs
        step = pl.program_id(0) * pl.num_programs(1) + pl.program_id(1)
        dst_ref[...] = jnp.where(step < regroup_blocks, src_ref[...], 0.0).astype(BF16)

    def add_tile(base, xn):
        g = jnp.dot(xn, wg_ref[...], preferred_element_type=F32)
        u = jnp.dot(xn, wu_ref[...], preferred_element_type=F32)
        hid = (g * jax.nn.sigmoid(g) * u * 0.5).astype(BF16)
        return base + jnp.dot(hid, wd_ref[...], preferred_element_type=F32)

    @pl.when(pl.program_id(1) == 0)
    def _():
        copies = _row_chunk_copies(x_hbm, o_ref, sem, pl.program_id(0) * o_ref.shape[0])
        for copy in copies:
            copy.start()
        for c, copy in enumerate(copies):
            copy.wait()
            rows = slice(c * NORM_ROWS, (c + 1) * NORM_ROWS)
            x = o_ref[rows, :]
            xn = _rms(x, gain_ref[...]).astype(BF16)
            xn_ref[rows, :] = xn
            o_ref[rows, :] = add_tile(x, xn)

    @pl.when(pl.program_id(1) > 0)
    def _():
        o_ref[...] = add_tile(o_ref[...], xn_ref[...])


def _regroup_specs(wt, layer, split, base, steps):
    _, n, d = wt.shape
    assert base > split
    rb = math.gcd(split, base - split)
    nblk, first_gap, gap = pl.cdiv(n, rb), split // rb, (base - split) // rb
    assert rb % BF16_ROWS == 0 and nblk + gap <= steps[0] * steps[1]

    def src_block(i, j):
        return jnp.minimum(i * steps[1] + j, nblk - 1)

    def dst_block(i, j):
        s = i * steps[1] + j
        copied = jnp.where(s < first_gap, s, s + gap)
        zeroed = first_gap + jnp.minimum(s - nblk, gap - 1)
        return jnp.where(s < nblk, copied, zeroed)

    return (pl.BlockSpec((None, rb, d), lambda i, j: (layer, src_block(i, j), 0)),
            pl.BlockSpec((rb, d), lambda i, j: (dst_block(i, j), 0)),
            jax.ShapeDtypeStruct((n + base - split, d), BF16), nblk)


def _ffn(x, gain, w_gu, w_down, regroup=None, *, bm=1024, bf=256):
    t, d = x.shape
    f = w_down.shape[0]
    bm, bf = _tile(t, bm), _tile(f, bf)
    assert bm % NORM_ROWS == 0
    nf = f // bf
    grid = (t // bm, nf)
    cast_in, cast_out, cast_shape, cast_args, nblk = [], [], [], [], None
    if regroup is not None:
        src, dst, shape, nblk = _regroup_specs(*regroup, grid)
        cast_in, cast_out, cast_shape, cast_args = [src], [dst], [shape], [regroup[0]]
    out = pl.pallas_call(
        functools.partial(_ffn_kernel, regroup_blocks=nblk),
        grid=grid,
        in_specs=[
            pl.BlockSpec(memory_space=pl.ANY),
            pl.BlockSpec((1, d), lambda i, j: (0, 0)),
            pl.BlockSpec((d, bf), lambda i, j: (0, j)),
            pl.BlockSpec((d, bf), lambda i, j: (0, j + nf)),
            pl.BlockSpec((bf, d), lambda i, j: (j, 0)),
            *cast_in,
        ],
        out_specs=[pl.BlockSpec((bm, d), lambda i, j: (i, 0)), *cast_out],
        out_shape=[jax.ShapeDtypeStruct((t, d), F32), *cast_shape],
        scratch_shapes=[pltpu.VMEM((bm, d), BF16), pltpu.SemaphoreType.DMA((bm // NORM_ROWS,))],
        compiler_params=_params("arbitrary", "arbitrary"),
        name="ffn",
    )(x, gain.reshape(1, d), w_gu, w_gu, w_down, *cast_args)
    return out if regroup is not None else out[0]


def _cast_specs(ws, layer, gi, gj, rc):
    ins, outs = [], []
    for w in ws:
        _, r, c = w.shape
        br = _ceil_to(pl.cdiv(r, gi), BF16_ROWS)
        bc = _ceil_to(pl.cdiv(c, gj), LANES)
        last = pl.cdiv(c, bc) - 1
        assert (gi - 1) * br < r <= gi * br and last < gj

        def block(*g, last=last):
            i, j = rc(*g)
            return i, jnp.minimum(j, last)

        ins.append(pl.BlockSpec((None, br, bc), lambda *g, block=block: (layer, *block(*g))))
        outs.append(pl.BlockSpec((br, bc), block))
    return ins, outs


def _cast_block(cast_in, cast_out):
    for src, dst in zip(cast_in, cast_out):
        dst[...] = src[...].astype(BF16)


def _dot_nt(a, wt):
    return lax.dot_general(a, wt, (((1,), (1,)), ((), ())), preferred_element_type=F32)


def _norm_matmul_kernel(x_hbm, gain_ref, wt_ref, o_ref, xn_ref, x_ref, sem):
    i, j = pl.program_id(0), pl.program_id(1)

    def copies(block):
        return _row_chunk_copies(x_hbm, x_ref, sem, block * x_ref.shape[0])

    @pl.when(j == 0)
    def _():
        @pl.when(i == 0)
        def _():
            for copy in copies(0):
                copy.start()

        for c, copy in enumerate(copies(i)):
            copy.wait()
            rows = slice(c * NORM_ROWS, (c + 1) * NORM_ROWS)
            xn = _rms(x_ref[rows, :], gain_ref[...]).astype(BF16)
            xn_ref[rows, :] = xn
            o_ref[rows, :] = _dot_nt(xn, wt_ref[...])

    @pl.when(j > 0)
    def _():
        @pl.when((j == 1) & (i + 1 < pl.num_programs(0)))
        def _():
            for copy in copies(i + 1):
                copy.start()

        o_ref[...] = _dot_nt(xn_ref[...], wt_ref[...])


def _norm_matmul(x, gain, wt, *, bm=1024, bn=1024, name):
    t, d = x.shape
    n = wt.shape[0]
    bm, bn = _tile(t, bm), _tile(n, bn)
    assert bm % NORM_ROWS == 0
    return pl.pallas_call(
        _norm_matmul_kernel,
        grid=(t // bm, n // bn),
        in_specs=[
            pl.BlockSpec(memory_space=pl.ANY),
            pl.BlockSpec((1, d), lambda i, j: (0, 0)),
            pl.BlockSpec((bn, d), lambda i, j: (j, 0)),
        ],
        out_specs=pl.BlockSpec((bm, bn), lambda i, j: (i, j)),
        out_shape=jax.ShapeDtypeStruct((t, n), F32),
        scratch_shapes=[pltpu.VMEM((bm, d), BF16), pltpu.VMEM((bm, d), F32),
                        pltpu.SemaphoreType.DMA((bm // NORM_ROWS,))],
        compiler_params=_params("arbitrary", "arbitrary"),
        name=name,
    )(x, gain.reshape(1, d), wt)


def _pool_kernel(zc_ref, zp_ref, wp_ref, sc_ref, o_ref, ext_ref, *, bm, c):
    i = pl.program_id(1)
    ext_ref[0:POOL_HALO, :] = jnp.where(i > 0, zp_ref[...], 0.0)
    ext_ref[POOL_HALO:POOL_HALO + bm, :] = zc_ref[...]
    t = i * bm + lax.broadcasted_iota(jnp.int32, (bm, 1), 0)
    for g, w in enumerate(POOL_WINDOWS):
        cols = slice(g * c, (g + 1) * c)
        x = zc_ref[:, cols]
        acc = x
        for k in range(1, w):
            acc = acc + ext_ref[POOL_HALO - k:POOL_HALO - k + bm, cols]
        cnt = jnp.minimum(t + 1, w).astype(F32)
        mixed = (acc / cnt - x).astype(BF16)
        y = jnp.dot(mixed, wp_ref[g], preferred_element_type=F32) * sc_ref[:, cols]
        o_ref[:, cols] = y.astype(BF16)


def _pool(z_main, w_pool, pool_scale, batch, seq, *, bm=512):
    g, c, _ = w_pool.shape
    pw = g * c
    bm = _tile(seq, bm)
    nb = seq // bm
    halo_blocks = bm // POOL_HALO
    return pl.pallas_call(
        functools.partial(_pool_kernel, bm=bm, c=c),
        grid=(batch, nb),
        in_specs=[
            pl.BlockSpec((bm, pw), lambda b, i: (b * nb + i, 0)),
            pl.BlockSpec((POOL_HALO, pw),
                         lambda b, i: (jnp.maximum((b * nb + i) * halo_blocks - 1, 0), 0)),
            pl.BlockSpec((g, c, c), lambda b, i: (0, 0, 0)),
            pl.BlockSpec((1, pw), lambda b, i: (0, 0)),
        ],
        out_specs=pl.BlockSpec((bm, pw), lambda b, i: (b * nb + i, 0)),
        out_shape=jax.ShapeDtypeStruct((batch * seq, pw), BF16),
        scratch_shapes=[pltpu.VMEM((POOL_HALO + bm, pw), F32)],
        compiler_params=_params("parallel", "parallel"),
        name="pool",
    )(z_main, z_main, w_pool, pool_scale.reshape(1, pw))


def _rope_tables(pos, inv):
    ang = pos.astype(F32) * inv
    cos, sin = jnp.cos(ang), jnp.sin(ang)
    lane = lax.broadcasted_iota(jnp.int32, ang.shape, 1)
    half = QK_ROPE // 2
    c = jnp.where(lane < QK_ROPE, cos, 0.0)
    sa = jnp.where(lane < half, -sin, 0.0)
    sb = jnp.where((lane >= half) & (lane < QK_ROPE), sin, 0.0)
    return c, sa, sb


def _rope_norm(r, gain, c, sa, sb):
    ms = jnp.sum(r * r, axis=-1, keepdims=True) * (1.0 / QK_ROPE)
    r = r * lax.rsqrt(ms + EPS) * gain
    half = QK_ROPE // 2
    return r * c + pltpu.roll(r, LANES - half, 1) * sa + pltpu.roll(r, half, 1) * sb


def _qkv_kernel(zq_ref, zkv_ref, zkr_ref, pos_ref, post_ref, inv_ref, invt_ref, gql_ref, gkvl_ref, gqt_ref,
                gkn_ref, gkr_ref, wuqt_ref, wuk_ref, wuvt_ref, qt_ref, k_ref, vt_ref,
                cqt_ref, ckv_ref, ckvt_ref, kr_ref, cost_ref, sint_ref):
    half = QK_ROPE // 2
    r1, r2, r3 = QK_NOPE, QK_NOPE + half, QK_HEAD

    @pl.when(pl.program_id(1) == 0)
    def _():
        cqt_ref[...] = _rms(zq_ref[...], gql_ref[...]).T.astype(BF16)
        ckv = _rms(zkv_ref[...], gkvl_ref[...])
        ckv_ref[...] = ckv.astype(BF16)
        ckvt_ref[...] = ckv.T.astype(BF16)
        c, sa, sb = _rope_tables(pos_ref[...], inv_ref[...])
        lane = lax.broadcasted_iota(jnp.int32, zkr_ref.shape, 1)
        zkr = jnp.where(lane < QK_ROPE, zkr_ref[...], 0.0)
        kr_ref[...] = _rope_norm(zkr, gkr_ref[...], c, sa, sb).astype(BF16)
        ang = post_ref[...].astype(F32) * invt_ref[...]
        cost_ref[...] = jnp.cos(ang)
        sint_ref[...] = jnp.sin(ang)

    bm = qt_ref.shape[-1]
    for s in range(qt_ref.shape[0]):
        q = jnp.dot(wuqt_ref[s], cqt_ref[...], preferred_element_type=F32)
        nope = q[:r1]
        rs = lax.rsqrt(jnp.mean(nope * nope, axis=0, keepdims=True) + EPS)
        qt_ref[s, :r1, :] = (nope * rs * gqt_ref[:r1, :]).astype(BF16)
        x1, x2 = q[r1:r2], q[r2:r3]
        ms = (jnp.sum(x1 * x1, axis=0, keepdims=True)
              + jnp.sum(x2 * x2, axis=0, keepdims=True)) * (1.0 / QK_ROPE)
        rs = lax.rsqrt(ms + EPS)
        y1, y2 = x1 * rs * gqt_ref[r1:r2, :], x2 * rs * gqt_ref[r2:r3, :]
        cos, sin = cost_ref[...], sint_ref[...]
        qt_ref[s, r1:r2, :] = (y1 * cos - y2 * sin).astype(BF16)
        qt_ref[s, r2:r3, :] = (y2 * cos + y1 * sin).astype(BF16)
        qt_ref[s, r3:, :] = jnp.zeros((QK_PAD - r3, bm), BF16)

        k = jnp.dot(ckv_ref[...], wuk_ref[s], preferred_element_type=F32)
        k_ref[s, :, :QK_NOPE] = _rms(k, gkn_ref[...]).astype(BF16)
        k_ref[s, :, QK_NOPE:] = kr_ref[...]
        vt = jnp.dot(wuvt_ref[s], ckvt_ref[...], preferred_element_type=F32)
        vt_ref[s, :V_HEAD, :] = vt.astype(BF16)
        vt_ref[s, V_HEAD:, :] = jnp.ones((V_ROWS - V_HEAD, bm), BF16)


def _qkv(z, pos, inv, gql, gkvl, gqt, gkn, gkr, wuqt, wuk, wuvt, batch, seq, off_q, off_kv, *, bm=512, hp=8):
    h, _, ql = wuqt.shape
    hp = min(hp, h)
    kvl = wuk.shape[1]
    bm = _tile(seq, bm)
    nb = seq // bm
    t = batch * seq
    half = QK_ROPE // 2
    assert off_q % ql == 0 and off_kv % kvl == 0 and (off_kv + kvl) % LANES == 0 and h % hp == 0
    const = lambda i, hh: (0, 0)
    head = lambda i, hh: (hh, 0, 0)
    out_map = lambda i, hh: (i // nb, hh, i % nb, 0)
    out_map_t = lambda i, hh: (i // nb, hh, i % nb, 0, 0)
    invt = jnp.broadcast_to(inv[0, :half].reshape(half, 1), (half, bm))
    gqt = jnp.broadcast_to(gqt.reshape(QK_PAD, 1), (QK_PAD, bm))
    return pl.pallas_call(
        _qkv_kernel,
        grid=(t // bm, h // hp),
        in_specs=[
            pl.BlockSpec((bm, ql), lambda i, hh: (i, off_q // ql)),
            pl.BlockSpec((bm, kvl), lambda i, hh: (i, off_kv // kvl)),
            pl.BlockSpec((bm, LANES), lambda i, hh: (i, (off_kv + kvl) // LANES)),
            pl.BlockSpec((bm, 1), lambda i, hh: (i, 0)),
            pl.BlockSpec((1, bm), lambda i, hh: (0, i)),
            pl.BlockSpec((1, LANES), const),
            pl.BlockSpec((half, bm), const),
            pl.BlockSpec((1, ql), const),
            pl.BlockSpec((1, kvl), const),
            pl.BlockSpec((QK_PAD, bm), const),
            pl.BlockSpec((1, QK_NOPE), const),
            pl.BlockSpec((1, LANES), const),
            pl.BlockSpec((hp, QK_PAD, ql), head),
            pl.BlockSpec((hp, kvl, QK_NOPE), head),
            pl.BlockSpec((hp, V_HEAD, kvl), head),
        ],
        out_specs=[
            pl.BlockSpec((None, hp, None, QK_PAD, bm), out_map_t),
            pl.BlockSpec((None, hp, bm, QK_PAD), out_map),
            pl.BlockSpec((None, hp, None, V_ROWS, bm), out_map_t),
        ],
        out_shape=[
            jax.ShapeDtypeStruct((batch, h, nb, QK_PAD, bm), BF16),
            jax.ShapeDtypeStruct((batch, h, seq, QK_PAD), BF16),
            jax.ShapeDtypeStruct((batch, h, nb, V_ROWS, bm), BF16),
        ],
        scratch_shapes=[
            pltpu.VMEM((ql, bm), BF16),
            pltpu.VMEM((bm, kvl), BF16),
            pltpu.VMEM((kvl, bm), BF16),
            pltpu.VMEM((bm, LANES), BF16),
            pltpu.VMEM((half, bm), F32),
            pltpu.VMEM((half, bm), F32),
        ],
        compiler_params=_params("parallel", "arbitrary"),
        name="qkv_prep",
    )(z, z, z, pos.reshape(t, 1), pos.reshape(1, t), inv, invt, gql, gkvl, gqt, gkn, gkr, wuqt, wuk, wuvt)


def _flash_kernel(qt_ref, k_ref, vt_ref, *refs, blk, chunk, n_cast):
    cast_in, o_ref, cast_out = refs[:n_cast], refs[n_cast], refs[n_cast + 1:2 * n_cast + 1]
    sa_ref, sb_ref, m_ref, acc_ref = refs[2 * n_cast + 1:]
    _cast_block(cast_in, cast_out)

    qi = pl.program_id(2)
    nch = 2 * blk // chunk
    per_blk = blk // chunk
    m_ref[...] = jnp.full(m_ref.shape, -jnp.inf, F32)
    acc_ref[...] = jnp.zeros(acc_ref.shape, F32)

    def visibility(c, diag):
        if diag is None or c * chunk >= (diag + 1) * blk:
            return "all"
        return "none" if (c + 1) * chunk <= diag * blk else "some"

    def scores(kj, s_ref, diag=None):
        start = pl.multiple_of(kj * blk, blk)
        k = k_ref[pl.ds(start, blk), :]
        for c in range(nch):
            if visibility(c, diag) != "none":
                qt = qt_ref[c // per_blk, :, (c % per_blk) * chunk:(c % per_blk + 1) * chunk]
                s_ref[c] = jnp.dot(k, qt, preferred_element_type=F32)

    def update(kj, s_ref, diag=None):
        vt = vt_ref[kj]
        for c in range(nch):
            see = visibility(c, diag)
            if see == "none":
                continue
            cols = slice(c * chunk, (c + 1) * chunk)
            st = s_ref[c]
            if see == "some":
                key = lax.broadcasted_iota(jnp.int32, st.shape, 0) + diag * blk
                qry = lax.broadcasted_iota(jnp.int32, st.shape, 1) + c * chunk
                st = jnp.where(key <= qry, st, -jnp.inf)
            m_old = m_ref[:, cols]
            m_new = jnp.maximum(m_old, jnp.max(st, axis=0, keepdims=True))
            p = jnp.exp2(st - m_new).astype(BF16)
            alpha = jnp.exp2(m_old - m_new)
            acc_ref[:, cols] = alpha * acc_ref[:, cols] + jnp.dot(vt, p, preferred_element_type=F32)
            m_ref[:, cols] = m_new

    scores(0, sa_ref)

    def pair(t):
        scores(2 * t + 1, sb_ref)
        update(2 * t, sa_ref)
        scores(2 * t + 2, sa_ref)
        update(2 * t + 1, sb_ref)

    def trip(u, carry):
        for t in range(PAIRS_PER_TRIP):
            pair(PAIRS_PER_TRIP * u + t)
        return carry

    lax.fori_loop(0, qi // PAIRS_PER_TRIP, trip, 0)
    done = qi // PAIRS_PER_TRIP * PAIRS_PER_TRIP
    for rest in range(PAIRS_PER_TRIP):
        @pl.when(qi - done == rest)
        def _():
            for t in range(rest):
                pair(done + t)
            scores(2 * qi + 1, sb_ref, diag=1)
            update(2 * qi, sa_ref, diag=0)
            update(2 * qi + 1, sb_ref, diag=1)
            o_ref[...] = (acc_ref[:V_HEAD, :] / acc_ref[V_HEAD:V_HEAD + 1, :]).T.astype(o_ref.dtype)


def _flash(qt, k, vt, cast=(), layer=0, *, chunk=256):
    b, h, nb, _, blk = qt.shape
    s = nb * blk
    chunk = min(chunk, blk)
    assert nb % 2 == 0
    cast_in, cast_out = _cast_specs(cast, layer, b * h, nb // 2, lambda bi, hi, qi: (bi * h + hi, qi))
    return pl.pallas_call(
        functools.partial(_flash_kernel, blk=blk, chunk=chunk, n_cast=len(cast)),
        grid=(b, h, nb // 2),
        in_specs=[
            pl.BlockSpec((None, None, 2, QK_PAD, blk), lambda bi, hi, qi: (bi, hi, qi, 0, 0)),
            pl.BlockSpec((None, None, s, QK_PAD), lambda bi, hi, qi: (bi, hi, 0, 0)),
            pl.BlockSpec((None, None, nb, V_ROWS, blk), lambda bi, hi, qi: (bi, hi, 0, 0, 0)),
            *cast_in,
        ],
        out_specs=[pl.BlockSpec((None, 2 * blk, V_HEAD), lambda bi, hi, qi: (bi, qi, hi)), *cast_out],
        out_shape=[jax.ShapeDtypeStruct((b, s, h * V_HEAD), BF16)]
        + [jax.ShapeDtypeStruct(w.shape[1:], BF16) for w in cast],
        scratch_shapes=[
            pltpu.VMEM((2 * blk // chunk, blk, chunk), F32),
            pltpu.VMEM((2 * blk // chunk, blk, chunk), F32),
            pltpu.VMEM((1, 2 * blk), F32),
            pltpu.VMEM((V_ROWS, 2 * blk), F32),
        ],
        compiler_params=_params("parallel", "parallel", "arbitrary"),
        name="mla_flash",
    )(qt, k, vt, *cast)


def _merge_kernel(a_ref, b_ref, wa_ref, wb_ref, gp_ref, gm_ref, o_ref):
    pa = jnp.dot(a_ref[...], wa_ref[...], preferred_element_type=F32)
    pb = jnp.dot(b_ref[...], wb_ref[...], preferred_element_type=F32)
    o_ref[...] = (jax.nn.sigmoid(gp_ref[...]) * pa + jax.nn.sigmoid(gm_ref[...]) * pb).astype(BF16)


def _merge(a, b, wa, wb, z, gate_off, *, bm=1024, bn=512):
    t, pw = a.shape
    mw = b.shape[1]
    d = wa.shape[1]
    bm, bn = _tile(t, bm), _tile(math.gcd(d, gate_off), bn)
    nj, g0 = d // bn, gate_off // bn
    return pl.pallas_call(
        _merge_kernel,
        grid=(t // bm, nj),
        in_specs=[
            pl.BlockSpec((bm, pw), lambda i, j: (i, 0)),
            pl.BlockSpec((bm, mw), lambda i, j: (i, 0)),
            pl.BlockSpec((pw, bn), lambda i, j: (0, j)),
            pl.BlockSpec((mw, bn), lambda i, j: (0, j)),
            pl.BlockSpec((bm, bn), lambda i, j: (i, g0 + j)),
            pl.BlockSpec((bm, bn), lambda i, j: (i, g0 + nj + j)),
        ],
        out_specs=pl.BlockSpec((bm, bn), lambda i, j: (i, j)),
        out_shape=jax.ShapeDtypeStruct((t, d), BF16),
        compiler_params=_params("parallel", "arbitrary"),
        name="branch_merge",
    )(a, b, wa, wb, z, z)


def _proj_residual_kernel(a_ref, w_ref, h_ref, o_ref):
    o_ref[...] = h_ref[...] + jnp.dot(a_ref[...], w_ref[...], preferred_element_type=F32)


def _proj_residual(a, w, h, *, bm=1024, bn=1024):
    t, kdim = a.shape
    d = w.shape[1]
    bm, bn = _tile(t, bm), _tile(d, bn)
    return pl.pallas_call(
        _proj_residual_kernel,
        grid=(t // bm, d // bn),
        in_specs=[
            pl.BlockSpec((bm, kdim), lambda i, j: (i, 0)),
            pl.BlockSpec((kdim, bn), lambda i, j: (0, j)),
            pl.BlockSpec((bm, bn), lambda i, j: (i, j)),
        ],
        out_specs=pl.BlockSpec((bm, bn), lambda i, j: (i, j)),
        out_shape=jax.ShapeDtypeStruct((t, d), F32),
        compiler_params=_params("parallel", "arbitrary"),
        name="out_proj",
    )(a, w, h)


def _mem_kv_kernel(mem_ref, gain_ref, w_ref, gk_ref, k_ref, v_ref):
    mn = _rms(mem_ref[...], gain_ref[...]).astype(BF16)
    kv = jnp.dot(mn, w_ref[...], preferred_element_type=F32)
    xw = X_HEADS * X_HEAD_DIM
    for hh in range(X_HEADS):
        cols = slice(hh * X_HEAD_DIM, (hh + 1) * X_HEAD_DIM)
        k_ref[:, cols] = _rms(kv[:, cols], gk_ref[...]).astype(BF16)
    v_ref[...] = kv[:, xw:].astype(BF16)


def _mem_kv(mem, gain, w_xkv, gk):
    b, m, d = mem.shape
    xw = X_HEADS * X_HEAD_DIM
    return pl.pallas_call(
        _mem_kv_kernel,
        grid=(b,),
        in_specs=[
            pl.BlockSpec((None, m, d), lambda bi: (bi, 0, 0)),
            pl.BlockSpec((1, d), lambda bi: (0, 0)),
            pl.BlockSpec((d, 2 * xw), lambda bi: (0, 0)),
            pl.BlockSpec((1, X_HEAD_DIM), lambda bi: (0, 0)),
        ],
        out_specs=[
            pl.BlockSpec((None, m, xw), lambda bi: (bi, 0, 0)),
            pl.BlockSpec((None, m, xw), lambda bi: (bi, 0, 0)),
        ],
        out_shape=[
            jax.ShapeDtypeStruct((b, m, xw), BF16),
            jax.ShapeDtypeStruct((b, m, xw), BF16),
        ],
        compiler_params=_params("parallel"),
        name="mem_kv",
    )(mem, gain.reshape(1, d), w_xkv, gk)


def _cross_kernel(h_ref, gain_ref, wq_ref, gq_ref, k_ref, v_ref, wo_ref, o_ref, xo_ref):
    h = h_ref[...]
    uq = _rms(h, gain_ref[...]).astype(BF16)
    xq = jnp.dot(uq, wq_ref[...], preferred_element_type=F32)
    scale = X_HEAD_DIM ** -0.5
    for hh in range(X_HEADS):
        cols = slice(hh * X_HEAD_DIM, (hh + 1) * X_HEAD_DIM)
        qh = _rms(xq[:, cols], gq_ref[...]).astype(BF16)
        s = lax.dot_general(qh, k_ref[:, cols], (((1,), (1,)), ((), ())),
                            preferred_element_type=F32) * scale
        p = jnp.exp(s - jnp.max(s, axis=-1, keepdims=True))
        p = (p / jnp.sum(p, axis=-1, keepdims=True)).astype(BF16)
        xo_ref[:, cols] = jnp.dot(p, v_ref[:, cols], preferred_element_type=F32).astype(BF16)
    o_ref[...] = h + jnp.dot(xo_ref[...], wo_ref[...], preferred_element_type=F32)


def _cross(h, gain, wq, gq, xk, xv, wo, seq, *, bm=512):
    t, d = h.shape
    m = xk.shape[1]
    xw = X_HEADS * X_HEAD_DIM
    bm = _tile(seq, bm)
    nb = seq // bm
    return pl.pallas_call(
        _cross_kernel,
        grid=(t // bm,),
        in_specs=[
            pl.BlockSpec((bm, d), lambda i: (i, 0)),
            pl.BlockSpec((1, d), lambda i: (0, 0)),
            pl.BlockSpec((d, xw), lambda i: (0, 0)),
            pl.BlockSpec((1, X_HEAD_DIM), lambda i: (0, 0)),
            pl.BlockSpec((None, m, xw), lambda i: (i // nb, 0, 0)),
            pl.BlockSpec((None, m, xw), lambda i: (i // nb, 0, 0)),
            pl.BlockSpec((xw, d), lambda i: (0, 0)),
        ],
        out_specs=pl.BlockSpec((bm, d), lambda i: (i, 0)),
        out_shape=jax.ShapeDtypeStruct((t, d), F32),
        scratch_shapes=[pltpu.VMEM((bm, xw), BF16)],
        compiler_params=_params("parallel"),
        name="cross_attn",
    )(h, gain.reshape(1, d), wq, gq, xk, xv, wo)


def _pad_lanes(v, width):
    return jnp.pad(v, (0, width - v.shape[0])).reshape(1, width)


def kernel(x, mem, positions, ffn1_norm, ffn1_w_gu, ffn1_w_down, mix_norm, w_in, w_pool, pool_scale,
           q_latent_norm, kv_latent_norm, w_uq, w_ukv, q_nope_norm, k_nope_norm, q_rope_norm,
           k_rope_norm, w_branch_pool, w_branch_mla, w_out, x_norm, mem_norm, w_xq, w_xkv, xq_norm,
           xk_norm, w_xo, ffn2_norm, ffn2_w_gu, ffn2_w_down):
    batch, seq, d = x.shape
    t = batch * seq
    depth = w_in.shape[0]
    pw = w_branch_pool.shape[1]
    ql = q_latent_norm.shape[1]
    kvl = kv_latent_norm.shape[1]
    heads = w_uq.shape[2] // QK_HEAD

    half = QK_ROPE // 2
    inv = 1.0 / (ROPE_THETA ** (jnp.arange(half, dtype=F32) * (2.0 / QK_ROPE)))
    inv = _pad_lanes(jnp.concatenate([inv, inv]), LANES)
    c2 = QK_HEAD ** -0.5 * math.log2(math.e)

    h = x.reshape(t, d)
    for l in range(depth):
        off_kv = pw + ql
        off_gate = off_kv + kvl + QK_ROPE
        gate_base = _ceil_to(off_gate, IN_TILE)
        h, w_in_t = _ffn(h, ffn1_norm[l], ffn1_w_gu[l].astype(BF16), ffn1_w_down[l].astype(BF16),
                         (jnp.swapaxes(w_in, 1, 2), l, off_gate, gate_base))
        z = _norm_matmul(h, mix_norm[l], w_in_t, bn=IN_TILE, name="in_proj")

        a_out = _pool(z, w_pool[l].astype(BF16), pool_scale[l], batch, seq)

        wuq = jnp.pad(w_uq[l].reshape(ql, heads, QK_HEAD), ((0, 0), (0, 0), (0, QK_PAD - QK_HEAD)))
        wuqt = wuq.transpose(1, 2, 0).astype(BF16)
        wukv = w_ukv[l].reshape(kvl, heads, QK_NOPE + V_HEAD)
        wuk = wukv[:, :, :QK_NOPE].transpose(1, 0, 2).astype(BF16)
        wuvt = wukv[:, :, QK_NOPE:].transpose(1, 2, 0).astype(BF16)
        gqt = jnp.concatenate([q_nope_norm[l], q_rope_norm[l], jnp.zeros((QK_PAD - QK_HEAD,), F32)]) * c2
        qt, k, vt = _qkv(z, positions, inv,
                         q_latent_norm[l].reshape(1, ql), kv_latent_norm[l].reshape(1, kvl), gqt,
                         k_nope_norm[l].reshape(1, QK_NOPE), _pad_lanes(k_rope_norm[l], LANES),
                         wuqt, wuk, wuvt, batch, seq, pw, off_kv)
        later = (ffn2_w_gu, ffn2_w_down, w_out, w_branch_pool, w_branch_mla)
        b_out, w_gu2, w_down2, w_out_b, w_bp, w_bm = _flash(qt, k, vt, later, l)
        b_out = b_out.reshape(t, heads * V_HEAD)

        merged = _merge(a_out, b_out, w_bp, w_bm, z, gate_base)
        h = _proj_residual(merged, w_out_b, h)

        xk, xv = _mem_kv(mem, mem_norm[l], w_xkv[l].astype(BF16), xk_norm[l].reshape(1, X_HEAD_DIM))
        h = _cross(h, x_norm[l], w_xq[l].astype(BF16), xq_norm[l].reshape(1, X_HEAD_DIM), xk, xv,
                   w_xo[l].astype(BF16), seq)

        h = _ffn(h, ffn2_norm[l], w_gu2, w_down2)
    return h.reshape(batch, seq, d)
```
